```python
import jax
import jax.numpy as jnp
from jax import lax
import numpy as np

D_MODEL = 2048
BATCH = 16
SEQ = 256
DEPTH = 2
DEC_BATCH = 4
DEC_SEQ = 2048
PAST_LEN = 512

GRID_W = 64
D_MIX = D_MODEL
D_CONV = D_MIX // 4
N_HEADS = 8
N_KV_HEADS = 2
HEAD_DIM = 128
GROUP = N_HEADS // N_KV_HEADS
D_ATTN = N_HEADS * HEAD_DIM
D_KV = N_KV_HEADS * HEAD_DIM
D_LRU = D_MIX - D_CONV - D_ATTN
CONV_K = 31
LRU_CONV_K = 4
LRU_BLOCKS = 8
LRU_BLK = D_LRU // LRU_BLOCKS
LRU_C = 8.0
WINDOW = 128
Q_BLOCK = 128
ROPE_BASE = 10000.0
ATTN_SCALE = HEAD_DIM ** -0.5
N_EXPERTS = 16
N_EXPERT_GROUPS = 4
EXPERTS_PER_GROUP = N_EXPERTS // N_EXPERT_GROUPS
TOP_K = 2
D_EXPERT = 512
EPS = 1e-6
NEG = -1e30
IN_SPLITS = [D_CONV, 2 * D_CONV, 2 * D_CONV + D_ATTN, 2 * D_CONV + D_ATTN + D_KV,
             2 * D_CONV + D_ATTN + 2 * D_KV, 2 * D_CONV + D_ATTN + 2 * D_KV + D_LRU]
D_IN = 2 * D_CONV + D_ATTN + 2 * D_KV + 2 * D_LRU

kernel_name = 'hybrid_flow_prefix_conv_swa_rglru_moe'


def rmsnorm(x, g):
    xf = x.astype(jnp.float32)
    y = xf * lax.rsqrt(jnp.mean(xf * xf, axis=-1, keepdims=True) + EPS)
    return (y * g.astype(jnp.float32)).astype(x.dtype)


def layernorm(x, g, b):
    xf = x.astype(jnp.float32)
    xc = xf - jnp.mean(xf, axis=-1, keepdims=True)
    y = xc * lax.rsqrt(jnp.mean(xc * xc, axis=-1, keepdims=True) + EPS)
    return (y * g.astype(jnp.float32) + b.astype(jnp.float32)).astype(x.dtype)


def modulation(cvec, w, b):
    m = jax.nn.silu(cvec) @ w + b
    return [mi[..., None, :] for mi in jnp.split(m, 6, axis=-1)]


def modulate(x, g, shift, scale):
    return rmsnorm(x, g) * (1 + scale) + shift


def dwconv(x, w, b, pad):
    y = lax.conv_general_dilated(x, w[:, None, :], window_strides=(1,), padding=[pad],
                                 dimension_numbers=('NWC', 'WIO', 'NWC'),
                                 feature_group_count=x.shape[-1])
    return y + b


def conv_module(a, g, p):
    z = a * jax.nn.sigmoid(g)
    z = dwconv(z, p['conv_dw'], p['conv_b'], ((CONV_K - 1) // 2, (CONV_K - 1) // 2))
    return jax.nn.silu(layernorm(z, p['conv_ln_g'], p['conv_ln_b']))


def rope_2d(x, pos_row, pos_col):
    seq = x.shape[1]
    n_freq = HEAD_DIM // 4
    inv = ROPE_BASE ** (-jnp.arange(n_freq, dtype=jnp.float32) / n_freq)
    bshape = (1, seq) + (1,) * (x.ndim - 3) + (n_freq,)

    def rot(xa, pos):
        ang = pos.astype(jnp.float32)[:, None] * inv[None, :]
        cos = jnp.cos(ang).reshape(bshape)
        sin = jnp.sin(ang).reshape(bshape)
        x1, x2 = xa[..., :n_freq], xa[..., n_freq:]
        return jnp.concatenate([x1 * cos - x2 * sin, x2 * cos + x1 * sin], axis=-1)

    xf = x.astype(jnp.float32)
    half = HEAD_DIM // 2
    out = jnp.concatenate([rot(xf[..., :half], pos_row), rot(xf[..., half:], pos_col)], axis=-1)
    return out.astype(x.dtype)


def sink_softmax(s, sink):
    sk = sink.astype(jnp.float32).reshape(N_KV_HEADS, GROUP)[None, :, :, None, None]
    m = jnp.maximum(jnp.max(s, axis=-1, keepdims=True), sk)
    p = jnp.exp(s - m)
    return p / (jnp.sum(p, axis=-1, keepdims=True) + jnp.exp(sk - m))


def context_attention(q, k, v, sink):
    bsz, lc = q.shape[:2]
    nb = lc // Q_BLOCK
    qb = q.reshape(bsz, nb, Q_BLOCK, N_KV_HEADS, GROUP, HEAD_DIM).swapaxes(0, 1)

    def block(qi):
        s = jnp.einsum('bqkgd,bskd->bkgqs', qi, k, preferred_element_type=jnp.float32) * ATTN_SCALE
        pr = sink_softmax(s, sink).astype(v.dtype)
        return jnp.einsum('bkgqs,bskd->bqkgd', pr, v)

    o = lax.map(block, qb)
    return o.swapaxes(0, 1).reshape(bsz, lc, D_ATTN)


def latent_attention(q, k, v, k_ctx, v_ctx, sink):
    bsz, seq = q.shape[:2]
    nb = seq // Q_BLOCK
    span = Q_BLOCK + 2 * WINDOW
    pad = ((0, 0), (WINDOW, WINDOW), (0, 0), (0, 0))
    kp = jnp.pad(k, pad)
    vp = jnp.pad(v, pad)
    qb = q.reshape(bsz, nb, Q_BLOCK, N_KV_HEADS, GROUP, HEAD_DIM).swapaxes(0, 1)
    q_off = jnp.arange(Q_BLOCK)
    k_off = jnp.arange(span) - WINDOW

    def block(args):
        qi, n = args
        start = n * Q_BLOCK
        kb = lax.dynamic_slice_in_dim(kp, start, span, axis=1)
        vb = lax.dynamic_slice_in_dim(vp, start, span, axis=1)
        qpos = start + q_off
        kpos = start + k_off
        ok = ((jnp.abs(qpos[:, None] - kpos[None, :]) <= WINDOW)
              & (kpos >= 0)[None, :] & (kpos < seq)[None, :])
        s_loc = jnp.einsum('bqkgd,bskd->bkgqs', qi, kb, preferred_element_type=jnp.float32) * ATTN_SCALE
        s_loc = jnp.where(ok, s_loc, NEG)
        s_ctx = jnp.einsum('bqkgd,bskd->bkgqs', qi, k_ctx, preferred_element_type=jnp.float32) * ATTN_SCALE
        pr = sink_softmax(jnp.concatenate([s_loc, s_ctx], axis=-1), sink).astype(v.dtype)
        vcat = jnp.concatenate([vb, v_ctx], axis=1)
        return jnp.einsum('bkgqs,bskd->bqkgd', pr, vcat)

    o = lax.map(block, (qb, jnp.arange(nb)))
    return o.swapaxes(0, 1).reshape(bsz, seq, D_ATTN)


def _lin_combine(e1, e2):
    a1, b1 = e1
    a2, b2 = e2
    return a1 * a2, a2 * b1 + b2


def rglru(x, wa, ba, wx, bx, lam, h0, reverse):
    bsz, seq, _ = x.shape
    xb = x.reshape(bsz, seq, LRU_BLOCKS, LRU_BLK)
    r = jax.nn.sigmoid(jnp.einsum('blnk,nkj->blnj', xb, wa).reshape(bsz, seq, D_LRU) + ba)
    i = jax.nn.sigmoid(jnp.einsum('blnk,nkj->blnj', xb, wx).reshape(bsz, seq, D_LRU) + bx)
    log_a = -LRU_C * r.astype(jnp.float32) * jax.nn.softplus(-lam.astype(jnp.float32))
    a = jnp.exp(log_a)
    b = jnp.sqrt(-jnp.expm1(2.0 * log_a)) * (i * x).astype(jnp.float32)
    if reverse:
        a = jnp.flip(a, axis=1)
        b = jnp.flip(b, axis=1)
    b = b.at[:, 0].add(a[:, 0] * h0.astype(jnp.float32))
    _, h = lax.associative_scan(_lin_combine, (a, b), axis=1)
    final = h[:, -1]
    if reverse:
        h = jnp.flip(h, axis=1)
    return h.astype(x.dtype), final.astype(x.dtype)


def recurrent_mixer(gate_in, x_in, p, h0_f, h0_b):
    xc = dwconv(x_in, p['lru_conv_w'], p['lru_conv_b'], (2, 1))
    hf, fin_f = rglru(xc, p['lru_wa'][0], p['lru_ba'][0], p['lru_wx'][0], p['lru_bx'][0], p['lru_lam'][0], h0_f, False)
    hb, fin_b = rglru(xc, p['lru_wa'][1], p['lru_ba'][1], p['lru_wx'][1], p['lru_bx'][1], p['lru_lam'][1], h0_b, True)
    return (hf + hb) * jax.nn.gelu(gate_in), fin_f, fin_b


def moe(h, w_router, b_router, wg, wu, wd):
    bsz, seq, d = h.shape
    t = h.reshape(bsz * seq, d)
    scores = jax.nn.sigmoid((t @ w_router).astype(jnp.float32))
    biased = scores + b_router.astype(jnp.float32)
    grouped = biased.reshape(-1, N_EXPERT_GROUPS, EXPERTS_PER_GROUP)
    group_score = jnp.sum(lax.top_k(grouped, TOP_K)[0], axis=-1)
    g_sel = jnp.argmax(group_score, axis=-1)
    in_group = (jnp.arange(N_EXPERTS) // EXPERTS_PER_GROUP)[None, :] == g_sel[:, None]
    _, idx = lax.top_k(jnp.where(in_group, biased, NEG), TOP_K)
    w_sel = jnp.take_along_axis(scores, idx, axis=-1)
    w_sel = w_sel / jnp.sum(w_sel, axis=-1, keepdims=True)
    gates = jnp.sum(jax.nn.one_hot(idx, N_EXPERTS, dtype=jnp.float32) * w_sel[..., None], axis=1)
    act = jax.nn.silu(jnp.einsum('td,edf->tef', t, wg)) * jnp.einsum('td,edf->tef', t, wu)
    act = act * gates.astype(act.dtype)[:, :, None]
    out = jnp.einsum('tef,efd->td', act, wd)
    return out.reshape(bsz, seq, d)


def mixer_inputs(x, p, mods):
    h = modulate(x, p['norm1_g'], mods[0], mods[1])
    u = h @ p['w_in']
    ca, cg, q, k, v, lg, lx = jnp.split(u, IN_SPLITS, axis=-1)
    bsz, seq = x.shape[:2]
    q = rmsnorm(q.reshape(bsz, seq, N_KV_HEADS, GROUP, HEAD_DIM), p['q_norm_g'])
    k = rmsnorm(k.reshape(bsz, seq, N_KV_HEADS, HEAD_DIM), p['k_norm_g'])
    v = v.reshape(bsz, seq, N_KV_HEADS, HEAD_DIM)
    return ca, cg, q, k, v, lg, lx


def finish_layer(x, conv_out, attn_out, lru_out, p, mods, w_router, b_router):
    mix = jnp.concatenate([conv_out, attn_out, lru_out], axis=-1) @ p['w_out']
    x = x + mods[2] * mix
    h = modulate(x, p['norm2_g'], mods[3], mods[4])
    return x + mods[5] * moe(h, w_router, b_router, p['w_gate_e'], p['w_up_e'], p['w_down_e'])


def context_layer(x, p, mods, w_router, b_router):
    ca, cg, q, k, v, lg, lx = mixer_inputs(x, p, mods)
    conv_out = conv_module(ca, cg, p)
    attn_out = context_attention(q, k, v, p['attn_sink'])
    h0 = jnp.zeros((x.shape[0], D_LRU), x.dtype)
    lru_out, fin_f, fin_b = recurrent_mixer(lg, lx, p, h0, h0)
    x = finish_layer(x, conv_out, attn_out, lru_out, p, mods, w_router, b_router)
    return x, k, v, jnp.stack([fin_f, fin_b], axis=1)


def latent_layer(x, p, mods, k_ctx, v_ctx, h_ctx, pos_row, pos_col, w_router, b_router):
    ca, cg, q, k, v, lg, lx = mixer_inputs(x, p, mods)
    conv_out = conv_module(ca, cg, p)
    attn_out = latent_attention(rope_2d(q, pos_row, pos_col), rope_2d(k, pos_row, pos_col),
                                v, k_ctx, v_ctx, p['attn_sink'])
    lru_out, _, _ = recurrent_mixer(lg, lx, p, h_ctx[:, 0], h_ctx[:, 1])
    return finish_layer(x, conv_out, attn_out, lru_out, p, mods, w_router, b_router)


def setup_inputs(seed: int = 0) -> dict:
    key = jax.random.key(seed)
    ks = jax.random.split(key, 40)
    f32 = jnp.float32

    def nrm(k, shape, scale):
        return jax.random.normal(k, shape, f32) * scale

    u = jax.random.uniform(ks[26], (DEPTH, 2, D_LRU), f32, 0.9, 0.999)
    s = u ** (1.0 / LRU_C)
    return {
        'x_prompt': nrm(ks[0], (BATCH, SEQ, D_MODEL), 1.0),
        'x_sample': nrm(ks[1], (DEC_BATCH, DEC_SEQ, D_MODEL), 1.0),
        'c': nrm(ks[2], (DEC_BATCH, D_MODEL), 1.0),
        'cache_k': nrm(ks[3], (DEC_BATCH, DEPTH, PAST_LEN, N_KV_HEADS, HEAD_DIM), 1.0),
        'cache_v': nrm(ks[4], (DEC_BATCH, DEPTH, PAST_LEN, N_KV_HEADS, HEAD_DIM), 1.0),
        'state_lru': nrm(ks[5], (DEC_BATCH, DEPTH, 2, D_LRU), 0.5),
        'c_ctx': nrm(ks[6], (D_MODEL,), 1.0),
        'w_mod': nrm(ks[7], (DEPTH, D_MODEL, 6 * D_MODEL), 0.5 * D_MODEL ** -0.5),
        'b_mod': nrm(ks[8], (DEPTH, 6 * D_MODEL), 0.02),
        'norm1_g': 1.0 + nrm(ks[9], (DEPTH, D_MODEL), 0.02),
        'norm2_g': 1.0 + nrm(ks[10], (DEPTH, D_MODEL), 0.02),
        'w_in': nrm(ks[11], (DEPTH, D_MODEL, D_IN), D_MODEL ** -0.5),
        'conv_dw': nrm(ks[12], (DEPTH, CONV_K, D_CONV), CONV_K ** -0.5),
        'conv_b': nrm(ks[13], (DEPTH, D_CONV), 0.02),
        'conv_ln_g': 1.0 + nrm(ks[14], (DEPTH, D_CONV), 0.02),
        'conv_ln_b': nrm(ks[15], (DEPTH, D_CONV), 0.02),
        'q_norm_g': 1.0 + nrm(ks[16], (DEPTH, HEAD_DIM), 0.02),
        'k_norm_g': 1.0 + nrm(ks[17], (DEPTH, HEAD_DIM), 0.02),
        'attn_sink': nrm(ks[18], (DEPTH, N_HEADS), 1.0),
        'lru_conv_w': nrm(ks[19], (DEPTH, LRU_CONV_K, D_LRU), LRU_CONV_K ** -0.5),
        'lru_conv_b': nrm(ks[20], (DEPTH, D_LRU), 0.02),
        'lru_wa': nrm(ks[21], (DEPTH, 2, LRU_BLOCKS, LRU_BLK, LRU_BLK), LRU_BLK ** -0.5),
        'lru_ba': nrm(ks[22], (DEPTH, 2, D_LRU), 0.02),
        'lru_wx': nrm(ks[23], (DEPTH, 2, LRU_BLOCKS, LRU_BLK, LRU_BLK), LRU_BLK ** -0.5),
        'lru_bx': nrm(ks[24], (DEPTH, 2, D_LRU), 0.02),
        'lru_lam': jnp.log(s) - jnp.log1p(-s),
        'w_out': nrm(ks[27], (DEPTH, D_MIX, D_MODEL), D_MIX ** -0.5),
        'w_router': nrm(ks[28], (D_MODEL, N_EXPERTS), D_MODEL ** -0.5),
        'b_router': nrm(ks[29], (N_EXPERTS,), 0.01),
        'w_gate_e': nrm(ks[30], (DEPTH, N_EXPERTS, D_MODEL, D_EXPERT), D_MODEL ** -0.5),
        'w_up_e': nrm(ks[31], (DEPTH, N_EXPERTS, D_MODEL, D_EXPERT), D_MODEL ** -0.5),
        'w_down_e': nrm(ks[32], (DEPTH, N_EXPERTS, D_EXPERT, D_MODEL), D_EXPERT ** -0.5),
    }


def reference(x_prompt, x_sample, c, cache_k, cache_v, state_lru, c_ctx, w_mod, b_mod,
              norm1_g, norm2_g, w_in, conv_dw, conv_b, conv_ln_g, conv_ln_b, q_norm_g,
              k_norm_g, attn_sink, lru_conv_w, lru_conv_b, lru_wa, lru_ba, lru_wx, lru_bx,
              lru_lam, w_out, w_router, b_router, w_gate_e, w_up_e, w_down_e):
    def layer_params(l):
        return {'norm1_g': norm1_g[l], 'norm2_g': norm2_g[l], 'w_in': w_in[l],
                'conv_dw': conv_dw[l], 'conv_b': conv_b[l], 'conv_ln_g': conv_ln_g[l],
                'conv_ln_b': conv_ln_b[l], 'q_norm_g': q_norm_g[l], 'k_norm_g': k_norm_g[l],
                'attn_sink': attn_sink[l], 'lru_conv_w': lru_conv_w[l], 'lru_conv_b': lru_conv_b[l],
                'lru_wa': lru_wa[l], 'lru_ba': lru_ba[l], 'lru_wx': lru_wx[l], 'lru_bx': lru_bx[l],
                'lru_lam': lru_lam[l], 'w_out': w_out[l], 'w_gate_e': w_gate_e[l],
                'w_up_e': w_up_e[l], 'w_down_e': w_down_e[l]}

    y = x_prompt
    ks_, vs_, hs_ = [], [], []
    for l in range(DEPTH):
        mods = modulation(c_ctx, w_mod[l], b_mod[l])
        y, k_l, v_l, h_l = context_layer(y, layer_params(l), mods, w_router, b_router)
        ks_.append(k_l)
        vs_.append(v_l)
        hs_.append(h_l)
    new_cache_k = jnp.stack(ks_, axis=1)
    new_cache_v = jnp.stack(vs_, axis=1)
    new_state_lru = jnp.stack(hs_, axis=1)

    n_tok = x_sample.shape[1]
    rows = n_tok // GRID_W
    pos_row = jnp.repeat(jnp.arange(rows), GRID_W)
    pos_col = jnp.tile(jnp.arange(GRID_W), rows)
    z = x_sample
    for l in range(DEPTH):
        mods = modulation(c, w_mod[l], b_mod[l])
        z = latent_layer(z, layer_params(l), mods, cache_k[:, l], cache_v[:, l], state_lru[:, l],
                         pos_row, pos_col, w_router, b_router)

    return (x_prompt * 0 + y if False else y, z, new_cache_k, new_cache_v, new_state_lru)
```

```python
import functools

import numpy as np
import jax
import jax.numpy as jnp
from jax import lax
from jax.experimental import pallas as pl
from jax.experimental.pallas import tpu as pltpu

D_MODEL = 2048
DEPTH = 2
GRID_W = 64
D_CONV = 512
N_HEADS = 8
N_KV_HEADS = 2
HEAD_DIM = 128
GROUP = N_HEADS // N_KV_HEADS
D_ATTN = N_HEADS * HEAD_DIM
D_KV = N_KV_HEADS * HEAD_DIM
D_LRU = 512
CONV_K = 31
LRU_CONV_K = 4
LRU_BLOCKS = 8
LRU_BLK = D_LRU // LRU_BLOCKS
LRU_C = 8.0
WINDOW = 128
ROPE_BASE = 10000.0
ATTN_SCALE = HEAD_DIM ** -0.5
N_EXPERTS = 16
N_EXPERT_GROUPS = 4
EXPERTS_PER_GROUP = N_EXPERTS // N_EXPERT_GROUPS
D_EXPERT = 512
EPS = 1e-6
NEG = -1e30
D_IN = 2 * D_CONV + D_ATTN + 2 * D_KV + 2 * D_LRU
N_MOD = 6 * D_MODEL
SUB = 8
MOD_ROWS = SUB

V7X_VMEM_BYTES = 64 * 1024 * 1024
VMEM_LIMIT = V7X_VMEM_BYTES - 8 * 1024 * 1024

F32 = jnp.float32
BF16 = jnp.bfloat16
HIGHEST = lax.Precision.HIGHEST


def _cparams(*sem):
    return pltpu.CompilerParams(dimension_semantics=sem, vmem_limit_bytes=VMEM_LIMIT)


def _sigmoid(x):
    return 0.5 * jnp.tanh(0.5 * x) + 0.5


def _silu(x):
    return x * _sigmoid(x)


def _const_spec(shape):
    return pl.BlockSpec(shape, lambda *_: (0,) * len(shape), pipeline_mode=pl.Buffered(1))


def _mod_kernel(c_ref, w_ref, b_ref, o_ref):
    s = _silu(c_ref[...])
    o_ref[0] = jnp.dot(s, w_ref[0], precision=HIGHEST, preferred_element_type=F32) + b_ref[0]


def _modulation(cvec, w_mod, b_mod):
    tn = 1536
    return pl.pallas_call(
        _mod_kernel,
        out_shape=jax.ShapeDtypeStruct((DEPTH, MOD_ROWS, N_MOD), F32),
        grid=(DEPTH, N_MOD // tn),
        in_specs=[
            _const_spec((MOD_ROWS, D_MODEL)),
            pl.BlockSpec((1, D_MODEL, tn), lambda l, j: (l, 0, j)),
            pl.BlockSpec((1, 1, tn), lambda l, j: (l, 0, j)),
        ],
        out_specs=pl.BlockSpec((1, MOD_ROWS, tn), lambda l, j: (l, 0, j)),
        compiler_params=_cparams("parallel", "parallel"),
        name="modulation",
    )(cvec, w_mod, b_mod.reshape(DEPTH, 1, N_MOD))


def _head_norm(x, g):
    return x * lax.rsqrt(jnp.mean(x * x, axis=-1, keepdims=True) + EPS) * g


def _rope(x, cos, sin_signed):
    lane = lax.broadcasted_iota(jnp.int32, x.shape, 1)
    partner = jnp.where((lane % 64) < 32, pltpu.roll(x, 96, 1), pltpu.roll(x, 32, 1))
    return x * cos + partner * sin_signed


def _inproj_kernel(*refs, rope):
    if rope:
        (x_ref, mod_ref, g1_ref, w_ref, qg_ref, kg_ref, cos_ref, sin_ref,
         z_ref, q_ref, k_ref, v_ref, lg_ref, lx_ref) = refs
    else:
        (x_ref, mod_ref, g1_ref, w_ref, qg_ref, kg_ref,
         z_ref, q_ref, k_ref, v_ref, lg_ref, lx_ref) = refs
    x = x_ref[0]
    mod = mod_ref[0]
    shift, scale = mod[0:1], mod[1:2]
    y = x * lax.rsqrt(jnp.mean(x * x, axis=-1, keepdims=True) + EPS) * g1_ref[...]
    h = (y * (1.0 + scale) + shift).astype(BF16)

    def proj(c0, c1):
        return jnp.dot(h, w_ref[:, c0:c1], preferred_element_type=F32)

    glu = proj(0, 2 * D_CONV)
    z_ref[0] = glu[:, :D_CONV] * _sigmoid(glu[:, D_CONV:])

    c0 = 2 * D_CONV
    q = proj(c0, c0 + D_ATTN)
    qg = qg_ref[...]
    for hd in range(N_HEADS):
        qh = _head_norm(q[:, hd * HEAD_DIM:(hd + 1) * HEAD_DIM], qg)
        if rope:
            qh = _rope(qh, cos_ref[...], sin_ref[...])
        q_ref[0, :, hd * HEAD_DIM:(hd + 1) * HEAD_DIM] = (qh * ATTN_SCALE).astype(BF16)

    c0 += D_ATTN
    kv = proj(c0, c0 + 2 * D_KV)
    kg = kg_ref[...]
    for hd in range(N_KV_HEADS):
        kh = _head_norm(kv[:, hd * HEAD_DIM:(hd + 1) * HEAD_DIM], kg)
        if rope:
            kh = _rope(kh, cos_ref[...], sin_ref[...])
        k_ref[0, :, hd * HEAD_DIM:(hd + 1) * HEAD_DIM] = kh
    v_ref[0] = kv[:, D_KV:]

    c0 += 2 * D_KV
    l2 = proj(c0, c0 + 2 * D_LRU)
    lg_ref[0] = l2[:, :D_LRU]
    lx_ref[0] = l2[:, D_LRU:]


def _rope_tables(seq):
    n_freq = HEAD_DIM // 4
    inv = (ROPE_BASE ** (-np.arange(n_freq, dtype=np.float32) / n_freq)).astype(np.float32)
    t = np.arange(seq)
    ang_r = (t // GRID_W).astype(np.float32)[:, None] * inv[None, :]
    ang_c = (t % GRID_W).astype(np.float32)[:, None] * inv[None, :]
    cos = np.concatenate([np.cos(ang_r)] * 2 + [np.cos(ang_c)] * 2, axis=-1)
    sin = np.concatenate([-np.sin(ang_r), np.sin(ang_r), -np.sin(ang_c), np.sin(ang_c)], axis=-1)
    return jnp.asarray(cos, F32), jnp.asarray(sin, F32)


def _inproj(x, mods, row0, row_step, g1, w_in_bf, qg, kg, rope, tm):
    bsz, seq, _ = x.shape
    tok = lambda width: pl.BlockSpec((1, tm, width), lambda b, i: (b, i, 0))
    in_specs = [
        tok(D_MODEL),
        pl.BlockSpec((1, 6, D_MODEL), lambda b, i: (row0 + row_step * b, 0, 0)),
        _const_spec((1, D_MODEL)),
        _const_spec((D_MODEL, D_IN)),
        _const_spec((1, HEAD_DIM)),
        _const_spec((1, HEAD_DIM)),
    ]
    args = [x, mods, g1.reshape(1, D_MODEL), w_in_bf, qg.reshape(1, HEAD_DIM), kg.reshape(1, HEAD_DIM)]
    if rope:
        cos, sin = _rope_tables(seq)
        in_specs += [pl.BlockSpec((tm, HEAD_DIM), lambda b, i: (i, 0))] * 2
        args += [cos, sin]
    widths = (D_CONV, D_ATTN, D_KV, D_KV, D_LRU, D_LRU)
    dtypes = (F32, BF16, F32, F32, F32, F32)
    return pl.pallas_call(
        functools.partial(_inproj_kernel, rope=rope),
        out_shape=[jax.ShapeDtypeStruct((bsz, seq, w), dt) for w, dt in zip(widths, dtypes)],
        grid=(bsz, seq // tm),
        in_specs=in_specs,
        out_specs=[tok(w) for w in widths],
        compiler_params=_cparams("parallel", "parallel"),
        name="inproj_rope" if rope else "inproj",
    )(*args)


CONV_PAD = (CONV_K - 1) // 2
CONV_HALO = 16
CONV_CHUNK = 32


def _conv_kernel(z_ref, w_ref, b_ref, g_ref, beta_ref, o_ref, zp_ref, *, seq):
    zeros = jnp.zeros((CONV_HALO, D_CONV), F32)
    zp_ref[0:CONV_HALO, :] = zeros
    zp_ref[CONV_HALO + seq:2 * CONV_HALO + seq, :] = zeros
    zp_ref[CONV_HALO:CONV_HALO + seq, :] = z_ref[0]

    def body(c, carry):
        r0 = pl.multiple_of(c * CONV_CHUNK, CONV_CHUNK)
        acc = jnp.broadcast_to(b_ref[...], (CONV_CHUNK, D_CONV))
        win = zp_ref[pl.ds(r0, CONV_CHUNK + 2 * CONV_HALO), :]
        for r in range(SUB):
            taps = [k for k in range(CONV_K) if (CONV_HALO - CONV_PAD + k) % SUB == r]
            span = max(CONV_HALO - CONV_PAD + k for k in taps) - r + CONV_CHUNK
            shifted = win[r:r + span, :]
            for k in taps:
                a0 = CONV_HALO - CONV_PAD + k - r
                acc = acc + w_ref[k:k + 1, :] * shifted[a0:a0 + CONV_CHUNK, :]
        mu = jnp.mean(acc, axis=-1, keepdims=True)
        xc = acc - mu
        var = jnp.mean(xc * xc, axis=-1, keepdims=True)
        y = xc * lax.rsqrt(var + EPS) * g_ref[...] + beta_ref[...]
        o_ref[0, pl.ds(r0, CONV_CHUNK), :] = _silu(y).astype(BF16)
        return carry

    lax.fori_loop(0, seq // CONV_CHUNK, body, 0)


def _conv_module(z, w, b, g, beta):
    bsz, seq, _ = z.shape
    row = lambda a: a.reshape(1, D_CONV)
    return pl.pallas_call(
        functools.partial(_conv_kernel, seq=seq),
        out_shape=jax.ShapeDtypeStruct((bsz, seq, D_CONV), BF16),
        grid=(bsz,),
        in_specs=[
            pl.BlockSpec((1, seq, D_CONV), lambda i: (i, 0, 0)),
            _const_spec((CONV_K, D_CONV)),
            _const_spec((1, D_CONV)), _const_spec((1, D_CONV)), _const_spec((1, D_CONV)),
        ],
        out_specs=pl.BlockSpec((1, seq, D_CONV), lambda i: (i, 0, 0)),
        scratch_shapes=[pltpu.VMEM((seq + 2 * CONV_HALO, D_CONV), F32)],
        compiler_params=_cparams("parallel"),
        name="conv_module",
    )(z, w, row(b), row(g), row(beta))


def _attn_kernel(*refs, latent, seq, tq):
    if latent:
        sink_ref, q_ref, k_ref, v_ref, kc_ref, vc_ref, o_ref = refs
    else:
        sink_ref, q_ref, k_ref, v_ref, o_ref = refs
    q = q_ref[0]
    if latent:
        n = pl.program_id(1)
        nblk = seq // tq
        blocks = (jnp.maximum(n - 1, 0), n, jnp.minimum(n + 1, nblk - 1))

        def window(ref):
            parts = [ref[0, pl.ds(pl.multiple_of(i * tq, tq), tq), :] for i in blocks]
            return jnp.concatenate(parts, axis=0)

        k_all = jnp.concatenate([window(k_ref), kc_ref[0]], axis=0).astype(BF16)
        v_all = jnp.concatenate([window(v_ref), vc_ref[0]], axis=0).astype(BF16)
        qpos = n * tq + lax.broadcasted_iota(jnp.int32, (tq, 3 * tq), 0)
        kpos = (n - 1) * tq + lax.broadcasted_iota(jnp.int32, (tq, 3 * tq), 1)
        ok = (jnp.abs(qpos - kpos) <= WINDOW) & (kpos >= 0) & (kpos < seq)
        bias = jnp.where(ok, 0.0, NEG).astype(F32)
        bias = jnp.concatenate([bias] * GROUP, axis=0)
    else:
        k_all = k_ref[0].astype(BF16)
        v_all = v_ref[0].astype(BF16)

    for j in range(N_KV_HEADS):
        heads = [j * GROUP + g for g in range(GROUP)]
        qs = jnp.concatenate([q[:, h * HEAD_DIM:(h + 1) * HEAD_DIM] for h in heads], axis=0)
        kj = k_all[:, j * HEAD_DIM:(j + 1) * HEAD_DIM]
        vj = v_all[:, j * HEAD_DIM:(j + 1) * HEAD_DIM]
        s = lax.dot_general(qs, kj, (((1,), (1,)), ((), ())), preferred_element_type=F32)
        if latent:
            s = jnp.concatenate([s[:, :3 * tq] + bias, s[:, 3 * tq:]], axis=1)
        sk = jnp.concatenate([jnp.full((tq, 1), sink_ref[h], F32) for h in heads], axis=0)
        m = jnp.maximum(jnp.max(s, axis=-1, keepdims=True), sk)
        p = jnp.exp(s - m)
        den = jnp.sum(p, axis=-1, keepdims=True) + jnp.exp(sk - m)
        o = jnp.dot(p.astype(BF16), vj, preferred_element_type=F32) / den
        for g, h in enumerate(heads):
            o_ref[0, :, h * HEAD_DIM:(h + 1) * HEAD_DIM] = o[g * tq:(g + 1) * tq].astype(BF16)


def _attention(q, k, v, sink, k_ctx=None, v_ctx=None):
    bsz, seq, _ = q.shape
    latent = k_ctx is not None
    tq = WINDOW if latent else seq
    seq_spec = pl.BlockSpec((1, seq, D_KV), lambda b, i: (b, 0, 0))
    in_specs = [
        pl.BlockSpec(memory_space=pltpu.SMEM),
        pl.BlockSpec((1, tq, D_ATTN), lambda b, i: (b, i, 0)),
        seq_spec, seq_spec,
    ]
    args = [sink, q, k, v]
    if latent:
        past = k_ctx.shape[1]
        ctx_spec = pl.BlockSpec((1, past, D_KV), lambda b, i: (b, 0, 0))
        in_specs += [ctx_spec, ctx_spec]
        args += [k_ctx, v_ctx]
    return pl.pallas_call(
        functools.partial(_attn_kernel, latent=latent, seq=seq, tq=tq),
        out_shape=jax.ShapeDtypeStruct((bsz, seq, D_ATTN), BF16),
        grid=(bsz, seq // tq),
        in_specs=in_specs,
        out_specs=pl.BlockSpec((1, tq, D_ATTN), lambda b, i: (b, i, 0)),
        compiler_params=_cparams("parallel", "parallel"),
        name="attn_latent" if latent else "attn_context",
    )(*args)


LRU_HALO = 8
LRU_CHUNK = 128
LRU_HALF = D_LRU // 2


def _softplus(x):
    return jnp.maximum(x, 0.0) + jnp.log(1.0 + jnp.exp(-jnp.abs(x)))


def _gelu_tanh(x):
    return 0.5 * x * (1.0 + jnp.tanh(0.7978845608028654 * (x + 0.044715 * (x * x * x))))


def _scan_tile(a, b, carry, reverse):
    row = lax.broadcasted_iota(jnp.int32, a.shape, 0)
    for d in (1, 2, 4):
        if reverse:
            valid = row < SUB - d
            shift = SUB - d
        else:
            valid = row >= d
            shift = d
        a_prev = jnp.where(valid, pltpu.roll(a, shift, 0), 1.0)
        b_prev = jnp.where(valid, pltpu.roll(b, shift, 0), 0.0)
        b = a * b_prev + b
        a = a * a_prev
    h = a * carry + b
    last = h[0:1, :] if reverse else h[SUB - 1:SUB, :]
    return h, last


def _lru_kernel(lx_ref, lg_ref, h0_ref, cw_ref, cb_ref, wbd_ref, gbias_ref, lam_ref,
                o_ref, fin_ref, xp_ref, af_ref, bf_ref, ab_ref, bb_ref, *, seq):
    zeros = jnp.zeros((LRU_HALO, D_LRU), F32)
    xp_ref[0:LRU_HALO, :] = zeros
    xp_ref[LRU_HALO + seq:2 * LRU_HALO + seq, :] = zeros
    xp_ref[LRU_HALO:LRU_HALO + seq, :] = lx_ref[0]
    a_refs = (af_ref, ab_ref)
    b_refs = (bf_ref, bb_ref)
    sp = _softplus(-lam_ref[...])

    def gates(c, carry):
        r0 = pl.multiple_of(c * LRU_CHUNK, LRU_CHUNK)
        xc = jnp.broadcast_to(cb_ref[...], (LRU_CHUNK, D_LRU))
        win = xp_ref[pl.ds(r0, LRU_CHUNK + 2 * LRU_HALO), :]
        for k in range(LRU_CONV_K):
            off = LRU_HALO - 2 + k
            xc = xc + cw_ref[k:k + 1, :] * win[off:off + LRU_CHUNK, :]
        for s in range(2):
            cols = slice(s * LRU_HALF, (s + 1) * LRU_HALF)
            xs = xc[:, cols]
            g = jnp.dot(xs.astype(BF16), wbd_ref[s], preferred_element_type=F32)
            g = g + gbias_ref[s:s + 1, :]
            for d in range(2):
                base = d * 2 * LRU_HALF
                r = _sigmoid(g[:, base:base + LRU_HALF])
                i = _sigmoid(g[:, base + LRU_HALF:base + 2 * LRU_HALF])
                log_a = -LRU_C * r * sp[d:d + 1, cols]
                a = jnp.exp(log_a)
                t = jnp.tanh(log_a)
                b = jnp.sqrt(-2.0 * t / (1.0 - t)) * (i * xs)
                a_refs[d][pl.ds(r0, LRU_CHUNK), cols] = a
                b_refs[d][pl.ds(r0, LRU_CHUNK), cols] = b
        return carry

    lax.fori_loop(0, seq // LRU_CHUNK, gates, 0)

    ntile = seq // SUB

    def scan(t, carry):
        cf, cb = carry
        rf = pl.multiple_of(t * SUB, SUB)
        rb = pl.multiple_of((ntile - 1 - t) * SUB, SUB)
        hf, cf = _scan_tile(af_ref[pl.ds(rf, SUB), :], bf_ref[pl.ds(rf, SUB), :], cf, False)
        hb, cb = _scan_tile(ab_ref[pl.ds(rb, SUB), :], bb_ref[pl.ds(rb, SUB), :], cb, True)
        bf_ref[pl.ds(rf, SUB), :] = hf
        bb_ref[pl.ds(rb, SUB), :] = hb
        return cf, cb

    h0 = h0_ref[0]
    cf, cb = lax.fori_loop(0, ntile, scan, (h0[0:1, :], h0[1:2, :]))
    fin_ref[0, 0:1, :] = cf
    fin_ref[0, 1:2, :] = cb

    def finish(c, carry):
        r0 = pl.multiple_of(c * LRU_CHUNK, LRU_CHUNK)
        rows = pl.ds(r0, LRU_CHUNK)
        o_ref[0, rows, :] = ((bf_ref[rows, :] + bb_ref[rows, :]) * _gelu_tanh(lg_ref[0, rows, :])).astype(BF16)
        return carry

    lax.fori_loop(0, seq // LRU_CHUNK, finish, 0)


def _lru_gate_weights(wa, ba, wx, bx):
    per_half = LRU_HALF // LRU_BLK

    def dense(w, s):
        out = jnp.zeros((LRU_HALF, LRU_HALF), F32)
        for n in range(per_half):
            out = lax.dynamic_update_slice(out, w[s * per_half + n], (n * LRU_BLK, n * LRU_BLK))
        return out

    wbd, gbias = [], []
    for s in range(2):
        cols = slice(s * LRU_HALF, (s + 1) * LRU_HALF)
        wbd.append(jnp.concatenate([dense(wa[0], s), dense(wx[0], s), dense(wa[1], s), dense(wx[1], s)], axis=1))
        gbias.append(jnp.concatenate([ba[0, cols], bx[0, cols], ba[1, cols], bx[1, cols]]))
    return jnp.stack(wbd).astype(BF16), jnp.stack(gbias)


def _recurrent_mixer(lx, lg, h0, cw, cb, wa, ba, wx, bx, lam):
    bsz, seq, _ = lx.shape
    wbd, gbias = _lru_gate_weights(wa, ba, wx, bx)
    seq_spec = pl.BlockSpec((1, seq, D_LRU), lambda i: (i, 0, 0))
    state_spec = pl.BlockSpec((1, 2, D_LRU), lambda i: (i, 0, 0))
    return pl.pallas_call(
        functools.partial(_lru_kernel, seq=seq),
        out_shape=[jax.ShapeDtypeStruct((bsz, seq, D_LRU), BF16),
                   jax.ShapeDtypeStruct((bsz, 2, D_LRU), F32)],
        grid=(bsz,),
        in_specs=[
            seq_spec, seq_spec, state_spec,
            _const_spec((LRU_CONV_K, D_LRU)), _const_spec((1, D_LRU)),
            _const_spec((2, LRU_HALF, 4 * LRU_HALF)), _const_spec((2, 4 * LRU_HALF)),
            _const_spec((2, D_LRU)),
        ],
        out_specs=[seq_spec, state_spec],
        scratch_shapes=[pltpu.VMEM((seq + 2 * LRU_HALO, D_LRU), F32)] + [pltpu.VMEM((seq, D_LRU), F32)] * 4,
        compiler_params=_cparams("parallel"),
        name="rglru",
    )(lx, lg, h0, cw, cb.reshape(1, D_LRU), wbd, gbias, lam)


def _route(scores, biased):
    rows = [biased[e:e + 1, :] for e in range(N_EXPERTS)]
    group_score = []
    for g in range(N_EXPERT_GROUPS):
        a, b, c, d = rows[4 * g:4 * g + 4]
        hi1, lo1 = jnp.maximum(a, b), jnp.minimum(a, b)
        hi2, lo2 = jnp.maximum(c, d), jnp.minimum(c, d)
        top = jnp.maximum(hi1, hi2)
        second = jnp.maximum(jnp.minimum(hi1, hi2), jnp.maximum(lo1, lo2))
        group_score.append(top + second)
    best = group_score[0]
    g_sel = jnp.zeros_like(best, dtype=jnp.int32)
    for g in range(1, N_EXPERT_GROUPS):
        better = group_score[g] > best
        g_sel = jnp.where(better, g, g_sel)
        best = jnp.where(better, group_score[g], best)
    picked = []
    for e in range(N_EXPERTS):
        g = e // EXPERTS_PER_GROUP
        rank = jnp.zeros_like(g_sel)
        for o in range(g * EXPERTS_PER_GROUP, (g + 1) * EXPERTS_PER_GROUP):
            if o == e:
                continue
            ahead = (rows[o] >= rows[e]) if o < e else (rows[o] > rows[e])
            rank = rank + ahead.astype(jnp.int32)
        sel = jnp.where(g_sel == g, rank, 2) < 2
        picked.append(jnp.where(sel, scores[e:e + 1, :], 0.0))
    total = picked[0]
    for e in range(1, N_EXPERTS):
        total = total + picked[e]
    return jnp.concatenate(picked, axis=0) / total


def _outproj_kernel(conv_ref, attn_ref, lru_ref, x_ref, mod_ref, g2_ref, w_ref, wr_ref, br_ref,
                    x1_ref, h2_ref, gates_ref):
    mix = jnp.dot(conv_ref[0], w_ref[0:D_CONV, :], preferred_element_type=F32)
    mix = mix + jnp.dot(attn_ref[0], w_ref[D_CONV:D_CONV + D_ATTN, :], preferred_element_type=F32)
    mix = mix + jnp.dot(lru_ref[0], w_ref[D_CONV + D_ATTN:, :], preferred_element_type=F32)
    mod = mod_ref[0]
    x1 = x_ref[0] + mod[2:3] * mix
    x1_ref[0] = x1
    y = x1 * lax.rsqrt(jnp.mean(x1 * x1, axis=-1, keepdims=True) + EPS) * g2_ref[...]
    h2 = y * (1.0 + mod[4:5]) + mod[3:4]
    h2_ref[0] = h2.astype(BF16)
    logits = lax.dot_general(wr_ref[...], h2, (((1,), (1,)), ((), ())),
                             precision=HIGHEST, preferred_element_type=F32)
    scores = _sigmoid(logits)
    gates_ref[0] = _route(scores, scores + br_ref[...])


def _outproj(conv, attn, lru, x, mods, row0, row_step, g2, w_out_bf, w_router_t, b_router, tm):
    bsz, seq, _ = x.shape
    tok = lambda width: pl.BlockSpec((1, tm, width), lambda b, i: (b, i, 0))
    return pl.pallas_call(
        _outproj_kernel,
        out_shape=[jax.ShapeDtypeStruct((bsz, seq, D_MODEL), F32),
                   jax.ShapeDtypeStruct((bsz, seq, D_MODEL), BF16),
                   jax.ShapeDtypeStruct((bsz, N_EXPERTS, seq), F32)],
        grid=(bsz, seq // tm),
        in_specs=[
            tok(D_CONV), tok(D_ATTN), tok(D_LRU), tok(D_MODEL),
            pl.BlockSpec((1, 6, D_MODEL), lambda b, i: (row0 + row_step * b, 0, 0)),
            _const_spec((1, D_MODEL)),
            _const_spec((D_MODEL, D_MODEL)),
            _const_spec((N_EXPERTS, D_MODEL)),
            _const_spec((N_EXPERTS, 1)),
        ],
        out_specs=[tok(D_MODEL), tok(D_MODEL),
                   pl.BlockSpec((1, N_EXPERTS, tm), lambda b, i: (b, 0, i))],
        compiler_params=_cparams("parallel", "parallel"),
        name="outproj_router",
    )(conv, attn, lru, x, mods, g2.reshape(1, D_MODEL), w_out_bf, w_router_t, b_router.reshape(N_EXPERTS, 1))


def _moe_kernel(h_ref, gates_ref, x1_ref, mod_ref, wg_ref, wu_ref, wd_ref, o_ref, acc_ref):
    e = pl.program_id(2)

    @pl.when(e == 0)
    def _():
        acc_ref[...] = jnp.zeros_like(acc_ref)

    h = h_ref[0]
    gates = gates_ref[0]
    lane = lax.broadcasted_iota(jnp.int32, gates.shape, 1)
    gate = jnp.sum(jnp.where(lane == e, gates, 0.0), axis=1, keepdims=True)
    act = _silu(jnp.dot(h, wg_ref[0], preferred_element_type=F32))
    act = act * jnp.dot(h, wu_ref[0], preferred_element_type=F32) * gate
    acc_ref[...] += jnp.dot(act.astype(BF16), wd_ref[0], preferred_element_type=F32)

    @pl.when(e == N_EXPERTS - 1)
    def _():
        o_ref[0] = x1_ref[0] + mod_ref[0][5:6] * acc_ref[...]


def _moe(h2, gates, x1, mods, row0, row_step, wg, wu, wd, tm):
    bsz, seq, _ = x1.shape
    tok = lambda width: pl.BlockSpec((1, tm, width), lambda b, i, e: (b, i, 0))
    return pl.pallas_call(
        _moe_kernel,
        out_shape=jax.ShapeDtypeStruct((bsz, seq, D_MODEL), F32),
        grid=(bsz, seq // tm, N_EXPERTS),
        in_specs=[
            tok(D_MODEL), tok(N_EXPERTS), tok(D_MODEL),
            pl.BlockSpec((1, 6, D_MODEL), lambda b, i, e: (row0 + row_step * b, 0, 0)),
            pl.BlockSpec((1, D_MODEL, D_EXPERT), lambda b, i, e: (e, 0, 0)),
            pl.BlockSpec((1, D_MODEL, D_EXPERT), lambda b, i, e: (e, 0, 0)),
            pl.BlockSpec((1, D_EXPERT, D_MODEL), lambda b, i, e: (e, 0, 0)),
        ],
        out_specs=tok(D_MODEL),
        scratch_shapes=[pltpu.VMEM((tm, D_MODEL), F32)],
        compiler_params=_cparams("parallel", "parallel", "arbitrary"),
        name="moe_dense",
    )(h2, gates, x1, mods, wg, wu, wd)


def _layer(x, p, mods, row0, row_step, h0, k_ctx=None, v_ctx=None):
    latent = k_ctx is not None
    bsz, seq, _ = x.shape
    tm = min(seq, 512)
    z, q, k, v, lg, lx = _inproj(x, mods, row0, row_step, p["norm1_g"], p["w_in"], p["q_norm_g"],
                                 p["k_norm_g"], latent, tm)
    conv = _conv_module(z, p["conv_dw"], p["conv_b"], p["conv_ln_g"], p["conv_ln_b"])
    attn = _attention(q, k, v, p["attn_sink"], k_ctx, v_ctx)
    lru, fin = _recurrent_mixer(lx, lg, h0, p["lru_conv_w"], p["lru_conv_b"], p["lru_wa"], p["lru_ba"],
                                p["lru_wx"], p["lru_bx"], p["lru_lam"])
    x1, h2, gates_t = _outproj(conv, attn, lru, x, mods, row0, row_step, p["norm2_g"], p["w_out"],
                               p["w_router_t"], p["b_router"], tm)
    gates = jnp.swapaxes(gates_t, 1, 2)
    out = _moe(h2, gates, x1, mods, row0, row_step, p["w_gate_e"], p["w_up_e"], p["w_down_e"], tm)
    return out, k, v, fin


def kernel(x_prompt, x_sample, c, cache_k, cache_v, state_lru, c_ctx, w_mod, b_mod, norm1_g, norm2_g, w_in,
           conv_dw, conv_b, conv_ln_g, conv_ln_b, q_norm_g, k_norm_g, attn_sink, lru_conv_w, lru_conv_b,
           lru_wa, lru_ba, lru_wx, lru_bx, lru_lam, w_out, w_router, b_router, w_gate_e, w_up_e, w_down_e):
    bsz = x_prompt.shape[0]
    dec_bsz = x_sample.shape[0]
    past = cache_k.shape[2]

    cvec = jnp.zeros((MOD_ROWS, D_MODEL), F32).at[0].set(c_ctx).at[1:1 + dec_bsz].set(c)
    mods_all = _modulation(cvec, w_mod, b_mod).reshape(DEPTH, MOD_ROWS, 6, D_MODEL)

    w_router_t = w_router.T
    layers = []
    for l in range(DEPTH):
        layers.append({
            "norm1_g": norm1_g[l], "norm2_g": norm2_g[l], "w_in": w_in[l].astype(BF16),
            "conv_dw": conv_dw[l], "conv_b": conv_b[l], "conv_ln_g": conv_ln_g[l], "conv_ln_b": conv_ln_b[l],
            "q_norm_g": q_norm_g[l], "k_norm_g": k_norm_g[l], "attn_sink": attn_sink[l],
            "lru_conv_w": lru_conv_w[l], "lru_conv_b": lru_conv_b[l], "lru_wa": lru_wa[l], "lru_ba": lru_ba[l],
            "lru_wx": lru_wx[l], "lru_bx": lru_bx[l], "lru_lam": lru_lam[l],
            "w_out": w_out[l].astype(BF16), "w_router_t": w_router_t, "b_router": b_router,
            "w_gate_e": w_gate_e[l].astype(BF16), "w_up_e": w_up_e[l].astype(BF16),
            "w_down_e": w_down_e[l].astype(BF16),
        })

    y = x_prompt
    ks, vs, hs = [], [], []
    h0_ctx = jnp.zeros((bsz, 2, D_LRU), F32)
    for l in range(DEPTH):
        y, k_l, v_l, h_l = _layer(y, layers[l], mods_all[l], 0, 0, h0_ctx)
        ks.append(k_l.reshape(bsz, -1, N_KV_HEADS, HEAD_DIM))
        vs.append(v_l.reshape(bsz, -1, N_KV_HEADS, HEAD_DIM))
        hs.append(h_l)
    new_cache_k = jnp.stack(ks, axis=1)
    new_cache_v = jnp.stack(vs, axis=1)
    new_state_lru = jnp.stack(hs, axis=1)

    z = x_sample
    for l in range(DEPTH):
        z, _, _, _ = _layer(z, layers[l], mods_all[l], 1, 1, state_lru[:, l],
                            cache_k[:, l].reshape(dec_bsz, past, D_KV),
                            cache_v[:, l].reshape(dec_bsz, past, D_KV))
    return y, z, new_cache_k, new_cache_v, new_state_lru
```

```python
import functools

import numpy as np
import jax
import jax.numpy as jnp
from jax import lax
from jax.experimental import pallas as pl
from jax.experimental.pallas import tpu as pltpu

D_MODEL = 2048
DEPTH = 2
GRID_W = 64
D_CONV = 512
N_HEADS = 8
N_KV_HEADS = 2
HEAD_DIM = 128
GROUP = N_HEADS // N_KV_HEADS
D_ATTN = N_HEADS * HEAD_DIM
D_KV = N_KV_HEADS * HEAD_DIM
D_LRU = 512
CONV_K = 31
LRU_CONV_K = 4
LRU_BLOCKS = 8
LRU_BLK = D_LRU // LRU_BLOCKS
LRU_C = 8.0
WINDOW = 128
ROPE_BASE = 10000.0
ATTN_SCALE = HEAD_DIM ** -0.5
N_EXPERTS = 16
N_EXPERT_GROUPS = 4
EXPERTS_PER_GROUP = N_EXPERTS // N_EXPERT_GROUPS
D_EXPERT = 512
EPS = 1e-6
NEG = -1e30
D_IN = 2 * D_CONV + D_ATTN + 2 * D_KV + 2 * D_LRU
N_MOD = 6 * D_MODEL
SUB = 8
MOD_ROWS = SUB

V7X_VMEM_BYTES = 64 * 1024 * 1024
VMEM_LIMIT = V7X_VMEM_BYTES - 8 * 1024 * 1024

F32 = jnp.float32
BF16 = jnp.bfloat16
HIGHEST = lax.Precision.HIGHEST


def _cparams(*sem):
    return pltpu.CompilerParams(dimension_semantics=sem, vmem_limit_bytes=VMEM_LIMIT)


def _sigmoid(x):
    return 0.5 * jnp.tanh(0.5 * x) + 0.5


def _silu(x):
    return x * _sigmoid(x)


def _const_spec(shape):
    return pl.BlockSpec(shape, lambda *_: (0,) * len(shape), pipeline_mode=pl.Buffered(1))


def _mod_kernel(c_ref, w_ref, b_ref, o_ref):
    s = _silu(c_ref[...])
    o_ref[0] = jnp.dot(s, w_ref[0], precision=HIGHEST, preferred_element_type=F32) + b_ref[0]


def _modulation(cvec, w_mod, b_mod):
    tn = 1536
    return pl.pallas_call(
        _mod_kernel,
        out_shape=jax.ShapeDtypeStruct((DEPTH, MOD_ROWS, N_MOD), F32),
        grid=(DEPTH, N_MOD // tn),
        in_specs=[
            _const_spec((MOD_ROWS, D_MODEL)),
            pl.BlockSpec((1, D_MODEL, tn), lambda l, j: (l, 0, j)),
            pl.BlockSpec((1, 1, tn), lambda l, j: (l, 0, j)),
        ],
        out_specs=pl.BlockSpec((1, MOD_ROWS, tn), lambda l, j: (l, 0, j)),
        compiler_params=_cparams("parallel", "parallel"),
        name="modulation",
    )(cvec, w_mod, b_mod.reshape(DEPTH, 1, N_MOD))


def _head_norm(x, g):
    return x * lax.rsqrt(jnp.mean(x * x, axis=-1, keepdims=True) + EPS) * g


def _rope(x, cos, sin_signed):
    lane = lax.broadcasted_iota(jnp.int32, x.shape, 1)
    partner = jnp.where((lane % 64) < 32, pltpu.roll(x, 96, 1), pltpu.roll(x, 32, 1))
    return x * cos + partner * sin_signed


def _inproj_kernel(*refs, rope):
    if rope:
        (x_ref, mod_ref, g1_ref, w_ref, qg_ref, kg_ref, cos_ref, sin_ref,
         z_ref, q_ref, k_ref, v_ref, lg_ref, lx_ref) = refs
    else:
        (x_ref, mod_ref, g1_ref, w_ref, qg_ref, kg_ref,
         z_ref, q_ref, k_ref, v_ref, lg_ref, lx_ref) = refs
    x = x_ref[0]
    mod = mod_ref[0]
    shift, scale = mod[0:1], mod[1:2]
    y = x * lax.rsqrt(jnp.mean(x * x, axis=-1, keepdims=True) + EPS) * g1_ref[...]
    h = (y * (1.0 + scale) + shift).astype(BF16)

    def proj(c0, c1):
        return jnp.dot(h, w_ref[:, c0:c1], preferred_element_type=F32)

    glu = proj(0, 2 * D_CONV)
    z_ref[0] = glu[:, :D_CONV] * _sigmoid(glu[:, D_CONV:])

    c0 = 2 * D_CONV
    q = proj(c0, c0 + D_ATTN)
    qg = qg_ref[...]
    for hd in range(N_HEADS):
        qh = _head_norm(q[:, hd * HEAD_DIM:(hd + 1) * HEAD_DIM], qg)
        if rope:
            qh = _rope(qh, cos_ref[...], sin_ref[...])
        q_ref[0, :, hd * HEAD_DIM:(hd + 1) * HEAD_DIM] = (qh * ATTN_SCALE).astype(BF16)

    c0 += D_ATTN
    kv = proj(c0, c0 + 2 * D_KV)
    kg = kg_ref[...]
    for hd in range(N_KV_HEADS):
        kh = _head_norm(kv[:, hd * HEAD_DIM:(hd + 1) * HEAD_DIM], kg)
        if rope:
            kh = _rope(kh, cos_ref[...], sin_ref[...])
        k_ref[0, :, hd * HEAD_DIM:(hd + 1) * HEAD_DIM] = kh
    v_ref[0] = kv[:, D_KV:]

    c0 += 2 * D_KV
    l2 = proj(c0, c0 + 2 * D_LRU)
    lg_ref[0] = l2[:, :D_LRU]
    lx_ref[0] = l2[:, D_LRU:]


def _rope_tables(seq):
    n_freq = HEAD_DIM // 4
    inv = (ROPE_BASE ** (-np.arange(n_freq, dtype=np.float32) / n_freq)).astype(np.float32)
    t = np.arange(seq)
    ang_r = (t // GRID_W).astype(np.float32)[:, None] * inv[None, :]
    ang_c = (t % GRID_W).astype(np.float32)[:, None] * inv[None, :]
    cos = np.concatenate([np.cos(ang_r)] * 2 + [np.cos(ang_c)] * 2, axis=-1)
    sin = np.concatenate([-np.sin(ang_r), np.sin(ang_r), -np.sin(ang_c), np.sin(ang_c)], axis=-1)
    return jnp.asarray(cos, F32), jnp.asarray(sin, F32)


def _inproj(x, mods, row0, row_step, g1, w_in_bf, qg, kg, rope, tm):
    bsz, seq, _ = x.shape
    tok = lambda width: pl.BlockSpec((1, tm, width), lambda b, i: (b, i, 0))
    in_specs = [
        tok(D_MODEL),
        pl.BlockSpec((1, 6, D_MODEL), lambda b, i: (row0 + row_step * b, 0, 0)),
        _const_spec((1, D_MODEL)),
        _const_spec((D_MODEL, D_IN)),
        _const_spec((1, HEAD_DIM)),
        _const_spec((1, HEAD_DIM)),
    ]
    args = [x, mods, g1.reshape(1, D_MODEL), w_in_bf, qg.reshape(1, HEAD_DIM), kg.reshape(1, HEAD_DIM)]
    if rope:
        cos, sin = _rope_tables(seq)
        in_specs += [pl.BlockSpec((tm, HEAD_DIM), lambda b, i: (i, 0))] * 2
        args += [cos, sin]
    widths = (D_CONV, D_ATTN, D_KV, D_KV, D_LRU, D_LRU)
    dtypes = (F32, BF16, F32, F32, F32, F32)
    return pl.pallas_call(
        functools.partial(_inproj_kernel, rope=rope),
        out_shape=[jax.ShapeDtypeStruct((bsz, seq, w), dt) for w, dt in zip(widths, dtypes)],
        grid=(bsz, seq // tm),
        in_specs=in_specs,
        out_specs=[tok(w) for w in widths],
        compiler_params=_cparams("parallel", "parallel"),
        name="inproj_rope" if rope else "inproj",
    )(*args)


CONV_PAD = (CONV_K - 1) // 2
CONV_HALO = 16
CONV_CHUNK = 32


def _conv_kernel(z_ref, w_ref, b_ref, g_ref, beta_ref, o_ref, zp_ref, *, seq):
    zeros = jnp.zeros((CONV_HALO, D_CONV), F32)
    zp_ref[0:CONV_HALO, :] = zeros
    zp_ref[CONV_HALO + seq:2 * CONV_HALO + seq, :] = zeros
    zp_ref[CONV_HALO:CONV_HALO + seq, :] = z_ref[0]

    def body(c, carry):
        r0 = pl.multiple_of(c * CONV_CHUNK, CONV_CHUNK)
        acc = jnp.broadcast_to(b_ref[...], (CONV_CHUNK, D_CONV))
        win = zp_ref[pl.ds(r0, CONV_CHUNK + 2 * CONV_HALO), :]
        for r in range(SUB):
            taps = [k for k in range(CONV_K) if (CONV_HALO - CONV_PAD + k) % SUB == r]
            span = max(CONV_HALO - CONV_PAD + k for k in taps) - r + CONV_CHUNK
            shifted = win[r:r + span, :]
            for k in taps:
                a0 = CONV_HALO - CONV_PAD + k - r
                acc = acc + w_ref[k:k + 1, :] * shifted[a0:a0 + CONV_CHUNK, :]
        mu = jnp.mean(acc, axis=-1, keepdims=True)
        xc = acc - mu
        var = jnp.mean(xc * xc, axis=-1, keepdims=True)
        y = xc * lax.rsqrt(var + EPS) * g_ref[...] + beta_ref[...]
        o_ref[0, pl.ds(r0, CONV_CHUNK), :] = _silu(y).astype(BF16)
        return carry

    lax.fori_loop(0, seq // CONV_CHUNK, body, 0)


def _conv_module(z, w, b, g, beta):
    bsz, seq, _ = z.shape
    row = lambda a: a.reshape(1, D_CONV)
    return pl.pallas_call(
        functools.partial(_conv_kernel, seq=seq),
        out_shape=jax.ShapeDtypeStruct((bsz, seq, D_CONV), BF16),
        grid=(bsz,),
        in_specs=[
            pl.BlockSpec((1, seq, D_CONV), lambda i: (i, 0, 0)),
            _const_spec((CONV_K, D_CONV)),
            _const_spec((1, D_CONV)), _const_spec((1, D_CONV)), _const_spec((1, D_CONV)),
        ],
        out_specs=pl.BlockSpec((1, seq, D_CONV), lambda i: (i, 0, 0)),
        scratch_shapes=[pltpu.VMEM((seq + 2 * CONV_HALO, D_CONV), F32)],
        compiler_params=_cparams("parallel"),
        name="conv_module",
    )(z, w, row(b), row(g), row(beta))


def _attn_kernel(*refs, latent, seq, tq):
    if latent:
        sink_ref, q_ref, k_ref, v_ref, kc_ref, vc_ref, o_ref = refs
    else:
        sink_ref, q_ref, k_ref, v_ref, o_ref = refs
    q = q_ref[0]
    if latent:
        n = pl.program_id(1)
        nblk = seq // tq
        blocks = (jnp.maximum(n - 1, 0), n, jnp.minimum(n + 1, nblk - 1))

        def window(ref):
            parts = [ref[0, pl.ds(pl.multiple_of(i * tq, tq), tq), :] for i in blocks]
            return jnp.concatenate(parts, axis=0)

        k_all = jnp.concatenate([window(k_ref), kc_ref[0]], axis=0).astype(BF16)
        v_all = jnp.concatenate([window(v_ref), vc_ref[0]], axis=0).astype(BF16)
        qpos = n * tq + lax.broadcasted_iota(jnp.int32, (tq, 3 * tq), 0)
        kpos = (n - 1) * tq + lax.broadcasted_iota(jnp.int32, (tq, 3 * tq), 1)
        ok = (jnp.abs(qpos - kpos) <= WINDOW) & (kpos >= 0) & (kpos < seq)
        bias = jnp.where(ok, 0.0, NEG).astype(F32)
        bias = jnp.concatenate([bias] * GROUP, axis=0)
    else:
        k_all = k_ref[0].astype(BF16)
        v_all = v_ref[0].astype(BF16)

    for j in range(N_KV_HEADS):
        heads = [j * GROUP + g for g in range(GROUP)]
        qs = jnp.concatenate([q[:, h * HEAD_DIM:(h + 1) * HEAD_DIM] for h in heads], axis=0)
        kj = k_all[:, j * HEAD_DIM:(j + 1) * HEAD_DIM]
        vj = v_all[:, j * HEAD_DIM:(j + 1) * HEAD_DIM]
        s = lax.dot_general(qs, kj, (((1,), (1,)), ((), ())), preferred_element_type=F32)
        if latent:
            s = jnp.concatenate([s[:, :3 * tq] + bias, s[:, 3 * tq:]], axis=1)
        sk = jnp.concatenate([jnp.full((tq, 1), sink_ref[h], F32) for h in heads], axis=0)
        m = jnp.maximum(jnp.max(s, axis=-1, keepdims=True), sk)
        p = jnp.exp(s - m)
        den = jnp.sum(p, axis=-1, keepdims=True) + jnp.exp(sk - m)
        o = jnp.dot(p.astype(BF16), vj, preferred_element_type=F32) / den
        for g, h in enumerate(heads):
            o_ref[0, :, h * HEAD_DIM:(h + 1) * HEAD_DIM] = o[g * tq:(g + 1) * tq].astype(BF16)


def _attention(q, k, v, sink, k_ctx=None, v_ctx=None):
    bsz, seq, _ = q.shape
    latent = k_ctx is not None
    tq = WINDOW if latent else seq
    seq_spec = pl.BlockSpec((1, seq, D_KV), lambda b, i: (b, 0, 0))
    in_specs = [
        pl.BlockSpec(memory_space=pltpu.SMEM),
        pl.BlockSpec((1, tq, D_ATTN), lambda b, i: (b, i, 0)),
        seq_spec, seq_spec,
    ]
    args = [sink, q, k, v]
    if latent:
        past = k_ctx.shape[1]
        ctx_spec = pl.BlockSpec((1, past, D_KV), lambda b, i: (b, 0, 0))
        in_specs += [ctx_spec, ctx_spec]
        args += [k_ctx, v_ctx]
    return pl.pallas_call(
        functools.partial(_attn_kernel, latent=latent, seq=seq, tq=tq),
        out_shape=jax.ShapeDtypeStruct((bsz, seq, D_ATTN), BF16),
        grid=(bsz, seq // tq),
        in_specs=in_specs,
        out_specs=pl.BlockSpec((1, tq, D_ATTN), lambda b, i: (b, i, 0)),
        compiler_params=_cparams("parallel", "parallel"),
        name="attn_latent" if latent else "attn_context",
    )(*args)


LRU_HALO = 8
LRU_CHUNK = 128
LRU_HALF = D_LRU // 2


def _softplus(x):
    return jnp.maximum(x, 0.0) + jnp.log(1.0 + jnp.exp(-jnp.abs(x)))


def _gelu_tanh(x):
    return 0.5 * x * (1.0 + jnp.tanh(0.7978845608028654 * (x + 0.044715 * (x * x * x))))


def _scan_tile(a, b, carry, reverse):
    row = lax.broadcasted_iota(jnp.int32, a.shape, 0)
    for d in (1, 2, 4):
        if reverse:
            valid = row < SUB - d
            shift = SUB - d
        else:
            valid = row >= d
            shift = d
        a_prev = jnp.where(valid, pltpu.roll(a, shift, 0), 1.0)
        b_prev = jnp.where(valid, pltpu.roll(b, shift, 0), 0.0)
        b = a * b_prev + b
        a = a * a_prev
    h = a * carry + b
    last = h[0:1, :] if reverse else h[SUB - 1:SUB, :]
    return h, last


def _lru_kernel(lx_ref, lg_ref, h0_ref, cw_ref, cb_ref, wbd_ref, gbias_ref, lam_ref,
                o_ref, fin_ref, xp_ref, af_ref, bf_ref, ab_ref, bb_ref, *, seq):
    zeros = jnp.zeros((LRU_HALO, D_LRU), F32)
    xp_ref[0:LRU_HALO, :] = zeros
    xp_ref[LRU_HALO + seq:2 * LRU_HALO + seq, :] = zeros
    xp_ref[LRU_HALO:LRU_HALO + seq, :] = lx_ref[0]
    a_refs = (af_ref, ab_ref)
    b_refs = (bf_ref, bb_ref)
    sp = _softplus(-lam_ref[...])

    def gates(c, carry):
        r0 = pl.multiple_of(c * LRU_CHUNK, LRU_CHUNK)
        xc = jnp.broadcast_to(cb_ref[...], (LRU_CHUNK, D_LRU))
        win = xp_ref[pl.ds(r0, LRU_CHUNK + 2 * LRU_HALO), :]
        for k in range(LRU_CONV_K):
            off = LRU_HALO - 2 + k
            xc = xc + cw_ref[k:k + 1, :] * win[off:off + LRU_CHUNK, :]
        for s in range(2):
            cols = slice(s * LRU_HALF, (s + 1) * LRU_HALF)
            xs = xc[:, cols]
            g = jnp.dot(xs.astype(BF16), wbd_ref[s], preferred_element_type=F32)
            g = g + gbias_ref[s:s + 1, :]
            for d in range(2):
                base = d * 2 * LRU_HALF
                r = _sigmoid(g[:, base:base + LRU_HALF])
                i = _sigmoid(g[:, base + LRU_HALF:base + 2 * LRU_HALF])
                log_a = -LRU_C * r * sp[d:d + 1, cols]
                a = jnp.exp(log_a)
                t = jnp.tanh(log_a)
                b = jnp.sqrt(-2.0 * t / (1.0 - t)) * (i * xs)
                a_refs[d][pl.ds(r0, LRU_CHUNK), cols] = a
                b_refs[d][pl.ds(r0, LRU_CHUNK), cols] = b
        return carry

    lax.fori_loop(0, seq // LRU_CHUNK, gates, 0)

    ntile = seq // SUB

    def scan(t, carry):
        cf, cb = carry
        rf = pl.multiple_of(t * SUB, SUB)
        rb = pl.multiple_of((ntile - 1 - t) * SUB, SUB)
        hf, cf = _scan_tile(af_ref[pl.ds(rf, SUB), :], bf_ref[pl.ds(rf, SUB), :], cf, False)
        hb, cb = _scan_tile(ab_ref[pl.ds(rb, SUB), :], bb_ref[pl.ds(rb, SUB), :], cb, True)
        bf_ref[pl.ds(rf, SUB), :] = hf
        bb_ref[pl.ds(rb, SUB), :] = hb
        return cf, cb

    h0 = h0_ref[0]
    cf, cb = lax.fori_loop(0, ntile, scan, (h0[0:1, :], h0[1:2, :]))
    fin_ref[0, 0:1, :] = cf
    fin_ref[0, 1:2, :] = cb

    def finish(c, carry):
        r0 = pl.multiple_of(c * LRU_CHUNK, LRU_CHUNK)
        rows = pl.ds(r0, LRU_CHUNK)
        o_ref[0, rows, :] = ((bf_ref[rows, :] + bb_ref[rows, :]) * _gelu_tanh(lg_ref[0, rows, :])).astype(BF16)
        return carry

    lax.fori_loop(0, seq // LRU_CHUNK, finish, 0)


def _lru_gate_weights(wa, ba, wx, bx):
    per_half = LRU_HALF // LRU_BLK

    def dense(w, s):
        out = jnp.zeros((LRU_HALF, LRU_HALF), F32)
        for n in range(per_half):
            out = lax.dynamic_update_slice(out, w[s * per_half + n], (n * LRU_BLK, n * LRU_BLK))
        return out

    wbd, gbias = [], []
    for s in range(2):
        cols = slice(s * LRU_HALF, (s + 1) * LRU_HALF)
        wbd.append(jnp.concatenate([dense(wa[0], s), dense(wx[0], s), dense(wa[1], s), dense(wx[1], s)], axis=1))
        gbias.append(jnp.concatenate([ba[0, cols], bx[0, cols], ba[1, cols], bx[1, cols]]))
    return jnp.stack(wbd).astype(BF16), jnp.stack(gbias)


def _recurrent_mixer(lx, lg, h0, cw, cb, wa, ba, wx, bx, lam):
    bsz, seq, _ = lx.shape
    wbd, gbias = _lru_gate_weights(wa, ba, wx, bx)
    seq_spec = pl.BlockSpec((1, seq, D_LRU), lambda i: (i, 0, 0))
    state_spec = pl.BlockSpec((1, 2, D_LRU), lambda i: (i, 0, 0))
    return pl.pallas_call(
        functools.partial(_lru_kernel, seq=seq),
        out_shape=[jax.ShapeDtypeStruct((bsz, seq, D_LRU), BF16),
                   jax.ShapeDtypeStruct((bsz, 2, D_LRU), F32)],
        grid=(bsz,),
        in_specs=[
            seq_spec, seq_spec, state_spec,
            _const_spec((LRU_CONV_K, D_LRU)), _const_spec((1, D_LRU)),
            _const_spec((2, LRU_HALF, 4 * LRU_HALF)), _const_spec((2, 4 * LRU_HALF)),
            _const_spec((2, D_LRU)),
        ],
        out_specs=[seq_spec, state_spec],
        scratch_shapes=[pltpu.VMEM((seq + 2 * LRU_HALO, D_LRU), F32)] + [pltpu.VMEM((seq, D_LRU), F32)] * 4,
        compiler_params=_cparams("parallel"),
        name="rglru",
    )(lx, lg, h0, cw, cb.reshape(1, D_LRU), wbd, gbias, lam)


N_PAIRS = EXPERTS_PER_GROUP * (EXPERTS_PER_GROUP - 1) // 2
N_BUCKETS = N_EXPERT_GROUPS * N_PAIRS
PAIR_LO = (0, 0, 0, 1, 1, 2)
PAIR_HI = (1, 2, 3, 2, 3, 3)
ROUTE_ROWS = SUB


def _route(scores, biased):
    rows = [biased[e:e + 1, :] for e in range(N_EXPERTS)]
    group_score = []
    for g in range(N_EXPERT_GROUPS):
        a, b, c, d = rows[4 * g:4 * g + 4]
        hi1, lo1 = jnp.maximum(a, b), jnp.minimum(a, b)
        hi2, lo2 = jnp.maximum(c, d), jnp.minimum(c, d)
        top = jnp.maximum(hi1, hi2)
        second = jnp.maximum(jnp.minimum(hi1, hi2), jnp.maximum(lo1, lo2))
        group_score.append(top + second)
    best = group_score[0]
    g_sel = jnp.zeros_like(best, dtype=jnp.int32)
    for g in range(1, N_EXPERT_GROUPS):
        better = group_score[g] > best
        g_sel = jnp.where(better, g, g_sel)
        best = jnp.where(better, group_score[g], best)
    sel = []
    for e in range(N_EXPERTS):
        g = e // EXPERTS_PER_GROUP
        rank = jnp.zeros_like(g_sel)
        for o in range(g * EXPERTS_PER_GROUP, (g + 1) * EXPERTS_PER_GROUP):
            if o == e:
                continue
            ahead = (rows[o] >= rows[e]) if o < e else (rows[o] > rows[e])
            rank = rank + ahead.astype(jnp.int32)
        sel.append(jnp.where(jnp.where(g_sel == g, rank, 2) < 2, 1, 0))
    zero = jnp.zeros_like(best)
    lo_w, hi_w = zero, zero
    lo_idx = jnp.zeros_like(g_sel)
    hi_idx = jnp.zeros_like(g_sel)
    for g in range(N_EXPERT_GROUPS):
        seen = jnp.zeros_like(g_sel)
        for j in range(EXPERTS_PER_GROUP):
            e = g * EXPERTS_PER_GROUP + j
            order = jnp.where(sel[e] == 1, seen, 2)
            lo_w = jnp.where(order == 0, scores[e:e + 1, :], lo_w)
            hi_w = jnp.where(order == 1, scores[e:e + 1, :], hi_w)
            lo_idx = jnp.where(order == 0, j, lo_idx)
            hi_idx = jnp.where(order == 1, j, hi_idx)
            seen = seen + sel[e]
    pair_base = jnp.where(lo_idx == 0, 0, jnp.where(lo_idx == 1, 3, 5))
    bucket = g_sel * N_PAIRS + pair_base + hi_idx - lo_idx - 1
    total = lo_w + hi_w
    pad = jnp.zeros((ROUTE_ROWS - 3, best.shape[1]), F32)
    return jnp.concatenate([bucket.astype(F32), lo_w / total, hi_w / total, pad], axis=0)


def _modulated_norm(x, g, shift, scale):
    y = x * lax.rsqrt(jnp.mean(x * x, axis=-1, keepdims=True) + EPS) * g
    return y * (1.0 + scale) + shift


def _outproj_kernel(conv_ref, attn_ref, lru_ref, x_ref, mod_ref, g2_ref, w_ref, wr_ref, br_ref,
                    x1_ref, route_ref):
    mix = jnp.dot(conv_ref[0], w_ref[0:D_CONV, :], preferred_element_type=F32)
    mix = mix + jnp.dot(attn_ref[0], w_ref[D_CONV:D_CONV + D_ATTN, :], preferred_element_type=F32)
    mix = mix + jnp.dot(lru_ref[0], w_ref[D_CONV + D_ATTN:, :], preferred_element_type=F32)
    mod = mod_ref[0]
    x1 = x_ref[0] + mod[2:3] * mix
    x1_ref[0] = x1
    h2 = _modulated_norm(x1, g2_ref[...], mod[3:4], mod[4:5])
    logits = lax.dot_general(wr_ref[...], h2, (((1,), (1,)), ((), ())),
                             precision=HIGHEST, preferred_element_type=F32)
    scores = _sigmoid(logits)
    route_ref[0] = _route(scores, scores + br_ref[...])


def _outproj(conv, attn, lru, x, mods, row0, row_step, g2, w_out_bf, w_router_t, b_router, tm):
    bsz, seq, _ = x.shape
    tok = lambda width: pl.BlockSpec((1, tm, width), lambda b, i: (b, i, 0))
    return pl.pallas_call(
        _outproj_kernel,
        out_shape=[jax.ShapeDtypeStruct((bsz, seq, D_MODEL), F32),
                   jax.ShapeDtypeStruct((bsz, ROUTE_ROWS, seq), F32)],
        grid=(bsz, seq // tm),
        in_specs=[
            tok(D_CONV), tok(D_ATTN), tok(D_LRU), tok(D_MODEL),
            pl.BlockSpec((1, 6, D_MODEL), lambda b, i: (row0 + row_step * b, 0, 0)),
            _const_spec((1, D_MODEL)),
            _const_spec((D_MODEL, D_MODEL)),
            _const_spec((N_EXPERTS, D_MODEL)),
            _const_spec((N_EXPERTS, 1)),
        ],
        out_specs=[tok(D_MODEL),
                   pl.BlockSpec((1, ROUTE_ROWS, tm), lambda b, i: (b, 0, i))],
        compiler_params=_cparams("parallel", "parallel"),
        name="outproj_router",
    )(conv, attn, lru, x, mods, g2.reshape(1, D_MODEL), w_out_bf, w_router_t, b_router.reshape(N_EXPERTS, 1))


MOE_TILE = 256
META_COLS = 4


def _moe_plan(bucket, w_lo, w_hi, mod_row, t_ctx, tm):
    t_all = bucket.shape[0]
    n_tiles = (t_all + N_BUCKETS * (tm - 1) + tm - 1) // tm
    ids = jnp.arange(N_BUCKETS, dtype=jnp.int32)
    order = jnp.argsort(bucket, stable=True).astype(jnp.int32)
    counts = jnp.sum((bucket[None, :] == ids[:, None]).astype(jnp.int32), axis=1)
    tiles_per = (counts + tm - 1) // tm
    tile_end = jnp.cumsum(tiles_per)
    tile_start = tile_end - tiles_per
    sorted_start = jnp.cumsum(counts) - counts
    n_used = tile_end[-1]
    j = jnp.arange(n_tiles, dtype=jnp.int32)
    jj = jnp.minimum(j, n_used - 1)
    b_of = jnp.sum((tile_end[None, :] <= jj[:, None]).astype(jnp.int32), axis=1)
    local = jj - tile_start[b_of]
    n_valid = jnp.where(j < n_used, jnp.clip(counts[b_of] - local * tm, 0, tm), 0)
    rows = (sorted_start[b_of] + local * tm)[:, None] + jnp.arange(tm, dtype=jnp.int32)[None, :]
    tok = order[jnp.clip(rows, 0, t_all - 1)]
    valid = jnp.arange(tm, dtype=jnp.int32)[None, :] < n_valid[:, None]
    n_ctx = jnp.sum((valid & (tok < t_ctx)).astype(jnp.int32), axis=1)
    group, pair = b_of // N_PAIRS, b_of % N_PAIRS
    e_lo = group * EXPERTS_PER_GROUP + jnp.asarray(PAIR_LO, jnp.int32)[pair]
    e_hi = group * EXPERTS_PER_GROUP + jnp.asarray(PAIR_HI, jnp.int32)[pair]
    meta = jnp.stack([w_lo[tok], w_hi[tok], mod_row[tok], jnp.zeros(tok.shape, F32)], axis=-1)
    i32 = lambda a: a.astype(jnp.int32)
    return i32(tok.reshape(-1)), i32(e_lo), i32(e_hi), i32(n_valid), i32(n_ctx), meta


def _moe_kernel(tok_ref, elo_ref, ehi_ref, nv_ref, nc_ref,
                xc_ref, xl_ref, meta_ref, mods_ref, g2_ref,
                wg_lo, wu_lo, wd_lo, wg_hi, wu_hi, wd_hi,
                oc_ref, ol_ref, xbuf, ybuf, gsem, ssem, *, tm, t_ctx, n_mod_rows):
    i = pl.program_id(0)
    n = pl.num_programs(0)
    slot = lax.rem(i, 2)

    def gather(hbm, t, r, s, size):
        return pltpu.make_async_copy(hbm.at[pl.ds(t, size), :], xbuf.at[s, pl.ds(r, size), :], gsem.at[s])

    def scatter(hbm, t, r, s, size):
        return pltpu.make_async_copy(ybuf.at[s, pl.ds(r, size), :], hbm.at[pl.ds(t, size), :], ssem.at[s])

    def start_rows(j, s, copy, hbm_ctx, hbm_lat):
        base = j * tm

        def ctx_row(r, c):
            copy(hbm_ctx, tok_ref[base + r], r, s, 1).start()
            return c

        def lat_row(r, c):
            copy(hbm_lat, tok_ref[base + r] - t_ctx, r, s, 1).start()
            return c

        lax.fori_loop(0, nc_ref[j], ctx_row, 0)
        lax.fori_loop(nc_ref[j], nv_ref[j], lat_row, 0)

    def wait_rows(j, s, copy, hbm):
        cnt = nv_ref[j]
        bulk = pl.multiple_of(lax.shift_left(lax.shift_right_logical(cnt, 3), 3), SUB)

        @pl.when(bulk > 0)
        def _():
            copy(hbm, 0, 0, s, bulk).wait()

        def one(r, c):
            copy(hbm, 0, 0, s, 1).wait()
            return c

        lax.fori_loop(bulk, cnt, one, 0)

    @pl.when(i == 0)
    def _():
        xbuf[...] = jnp.zeros_like(xbuf)
        start_rows(0, 0, gather, xc_ref, xl_ref)

    @pl.when(i + 1 < n)
    def _():
        start_rows(i + 1, 1 - slot, gather, xc_ref, xl_ref)

    wait_rows(i, slot, gather, xl_ref)

    @pl.when(i >= 2)
    def _():
        wait_rows(i - 2, slot, scatter, ol_ref)

    @pl.when(nv_ref[i] > 0)
    def _():
        x = xbuf[slot]
        meta = meta_ref[0]
        w_lo, w_hi, mrow = meta[:, 0:1], meta[:, 1:2], meta[:, 2:3]

        def per_row(k):
            out = jnp.broadcast_to(mods_ref[0, k:k + 1, :], x.shape)
            for r in range(1, n_mod_rows):
                out = jnp.where(mrow == float(r), mods_ref[r, k:k + 1, :], out)
            return out

        h = _modulated_norm(x, g2_ref[...], per_row(3), per_row(4)).astype(BF16)

        def expert(wg, wu, w):
            act = _silu(jnp.dot(h, wg[0, 0], preferred_element_type=F32))
            act = act * jnp.dot(h, wu[0, 0], preferred_element_type=F32) * w
            return act.astype(BF16)

        y = jnp.dot(expert(wg_lo, wu_lo, w_lo), wd_lo[0, 0], preferred_element_type=F32)
        y = y + jnp.dot(expert(wg_hi, wu_hi, w_hi), wd_hi[0, 0], preferred_element_type=F32)
        ybuf[slot] = x + per_row(5) * y

    start_rows(i, slot, scatter, oc_ref, ol_ref)

    @pl.when(i == n - 1)
    def _():
        wait_rows(i, slot, scatter, ol_ref)

        @pl.when(i >= 1)
        def _():
            wait_rows(i - 1, 1 - slot, scatter, ol_ref)


def _moe(x1c, x1l, route_c, route_l, mods, g2, wg, wu, wd, layer, dec_seq):
    t_ctx, t_lat = x1c.shape[0], x1l.shape[0]
    tm = MOE_TILE
    flat = lambda route, k: route[:, k, :].reshape(-1)
    both = lambda k: jnp.concatenate([flat(route_c, k), flat(route_l, k)])
    mod_row = np.concatenate([np.zeros(t_ctx), 1 + np.arange(t_lat) // dec_seq]).astype(np.float32)
    n_mod_rows = 1 + t_lat // dec_seq
    tok, e_lo, e_hi, n_valid, n_ctx, meta = _moe_plan(
        both(0).astype(jnp.int32), both(1), both(2), jnp.asarray(mod_row), t_ctx, tm)
    n_tiles = n_valid.shape[0]
    w_up = lambda sel: pl.BlockSpec((1, 1, D_MODEL, D_EXPERT),
                                    lambda i, tok, lo, hi, nv, nc: (layer, (lo, hi)[sel][i], 0, 0))
    w_dn = lambda sel: pl.BlockSpec((1, 1, D_EXPERT, D_MODEL),
                                    lambda i, tok, lo, hi, nv, nc: (layer, (lo, hi)[sel][i], 0, 0))
    hbm = pl.BlockSpec(memory_space=pl.ANY)
    return pl.pallas_call(
        functools.partial(_moe_kernel, tm=tm, t_ctx=t_ctx, n_mod_rows=n_mod_rows),
        out_shape=[jax.ShapeDtypeStruct((t_ctx, D_MODEL), F32), jax.ShapeDtypeStruct((t_lat, D_MODEL), F32)],
        grid_spec=pltpu.PrefetchScalarGridSpec(
            num_scalar_prefetch=5,
            grid=(n_tiles,),
            in_specs=[
                hbm, hbm,
                pl.BlockSpec((1, tm, META_COLS), lambda i, *_: (i, 0, 0)),
                _const_spec((MOD_ROWS, 6, D_MODEL)),
                _const_spec((1, D_MODEL)),
                w_up(0), w_up(0), w_dn(0), w_up(1), w_up(1), w_dn(1),
            ],
            out_specs=[hbm, hbm],
            scratch_shapes=[
                pltpu.VMEM((2, tm, D_MODEL), F32), pltpu.VMEM((2, tm, D_MODEL), F32),
                pltpu.SemaphoreType.DMA((2,)), pltpu.SemaphoreType.DMA((2,)),
            ],
        ),
        compiler_params=_cparams("arbitrary"),
        name="moe_pairs",
    )(tok, e_lo, e_hi, n_valid, n_ctx, x1c, x1l, meta, mods, g2.reshape(1, D_MODEL), wg, wu, wd, wg, wu, wd)


def _mixers(x, p, mods, row0, row_step, h0, k_ctx=None, v_ctx=None):
    latent = k_ctx is not None
    bsz, seq, _ = x.shape
    tm = min(seq, 512)
    z, q, k, v, lg, lx = _inproj(x, mods, row0, row_step, p["norm1_g"], p["w_in"], p["q_norm_g"],
                                 p["k_norm_g"], latent, tm)
    conv = _conv_module(z, p["conv_dw"], p["conv_b"], p["conv_ln_g"], p["conv_ln_b"])
    attn = _attention(q, k, v, p["attn_sink"], k_ctx, v_ctx)
    lru, fin = _recurrent_mixer(lx, lg, h0, p["lru_conv_w"], p["lru_conv_b"], p["lru_wa"], p["lru_ba"],
                                p["lru_wx"], p["lru_bx"], p["lru_lam"])
    x1, route = _outproj(conv, attn, lru, x, mods, row0, row_step, p["norm2_g"], p["w_out"],
                         p["w_router_t"], p["b_router"], tm)
    return x1, route, k, v, fin


def kernel(x_prompt, x_sample, c, cache_k, cache_v, state_lru, c_ctx, w_mod, b_mod, norm1_g, norm2_g, w_in,
           conv_dw, conv_b, conv_ln_g, conv_ln_b, q_norm_g, k_norm_g, attn_sink, lru_conv_w, lru_conv_b,
           lru_wa, lru_ba, lru_wx, lru_bx, lru_lam, w_out, w_router, b_router, w_gate_e, w_up_e, w_down_e):
    bsz, seq, _ = x_prompt.shape
    dec_bsz, dec_seq, _ = x_sample.shape
    past = cache_k.shape[2]

    cvec = jnp.zeros((MOD_ROWS, D_MODEL), F32).at[0].set(c_ctx).at[1:1 + dec_bsz].set(c)
    mods_all = _modulation(cvec, w_mod, b_mod).reshape(DEPTH, MOD_ROWS, 6, D_MODEL)

    w_router_t = w_router.T
    layers = []
    for l in range(DEPTH):
        layers.append({
            "norm1_g": norm1_g[l], "norm2_g": norm2_g[l], "w_in": w_in[l].astype(BF16),
            "conv_dw": conv_dw[l], "conv_b": conv_b[l], "conv_ln_g": conv_ln_g[l], "conv_ln_b": conv_ln_b[l],
            "q_norm_g": q_norm_g[l], "k_norm_g": k_norm_g[l], "attn_sink": attn_sink[l],
            "lru_conv_w": lru_conv_w[l], "lru_conv_b": lru_conv_b[l], "lru_wa": lru_wa[l], "lru_ba": lru_ba[l],
            "lru_wx": lru_wx[l], "lru_bx": lru_bx[l], "lru_lam": lru_lam[l],
            "w_out": w_out[l].astype(BF16), "w_router_t": w_router_t, "b_router": b_router,
        })
    wg_bf, wu_bf, wd_bf = w_gate_e.astype(BF16), w_up_e.astype(BF16), w_down_e.astype(BF16)

    y, z = x_prompt, x_sample
    ks, vs, hs = [], [], []
    h0_ctx = jnp.zeros((bsz, 2, D_LRU), F32)
    for l in range(DEPTH):
        p, mods = layers[l], mods_all[l]
        y1, route_c, k_l, v_l, h_l = _mixers(y, p, mods, 0, 0, h0_ctx)
        ks.append(k_l.reshape(bsz, seq, N_KV_HEADS, HEAD_DIM))
        vs.append(v_l.reshape(bsz, seq, N_KV_HEADS, HEAD_DIM))
        hs.append(h_l)
        z1, route_l, _, _, _ = _mixers(z, p, mods, 1, 1, state_lru[:, l],
                                       cache_k[:, l].reshape(dec_bsz, past, D_KV),
                                       cache_v[:, l].reshape(dec_bsz, past, D_KV))
        y, z = _moe(y1.reshape(bsz * seq, D_MODEL), z1.reshape(dec_bsz * dec_seq, D_MODEL), route_c, route_l,
                    mods, p["norm2_g"], wg_bf, wu_bf, wd_bf, l, dec_seq)
        y, z = y.reshape(bsz, seq, D_MODEL), z.reshape(dec_bsz, dec_seq, D_MODEL)
    new_cache_k = jnp.stack(ks, axis=1)
    new_cache_v = jnp.stack(vs, axis=1)
    new_state_lru = jnp.stack(hs, axis=1)
    return y, z, new_cache_k, new_cache_v, new_state_lru
```

```python
import functools

import numpy as np
import jax
import jax.numpy as jnp
from jax import lax
from jax.experimental import pallas as pl
from jax.experimental.pallas import tpu as pltpu

D_MODEL = 2048
DEPTH = 2
GRID_W = 64
D_CONV = 512
N_HEADS = 8
N_KV_HEADS = 2
HEAD_DIM = 128
GROUP = N_HEADS // N_KV_HEADS
D_ATTN = N_HEADS * HEAD_DIM
D_KV = N_KV_HEADS * HEAD_DIM
D_LRU = 512
CONV_K = 31
LRU_CONV_K = 4
LRU_BLOCKS = 8
LRU_BLK = D_LRU // LRU_BLOCKS
LRU_C = 8.0
WINDOW = 128
ROPE_BASE = 10000.0
ATTN_SCALE = HEAD_DIM ** -0.5
N_EXPERTS = 16
N_EXPERT_GROUPS = 4
EXPERTS_PER_GROUP = N_EXPERTS // N_EXPERT_GROUPS
D_EXPERT = 512
EPS = 1e-6
NEG = -1e30
D_IN = 2 * D_CONV + D_ATTN + 2 * D_KV + 2 * D_LRU
N_MOD = 6 * D_MODEL
SUB = 8
MOD_ROWS = SUB

V7X_VMEM_BYTES = 64 * 1024 * 1024
VMEM_LIMIT = V7X_VMEM_BYTES - 8 * 1024 * 1024

F32 = jnp.float32
BF16 = jnp.bfloat16
HIGHEST = lax.Precision.HIGHEST


def _cparams(*sem):
    return pltpu.CompilerParams(dimension_semantics=sem, vmem_limit_bytes=VMEM_LIMIT)


def _sigmoid(x):
    return 0.5 * jnp.tanh(0.5 * x) + 0.5


def _silu(x):
    return x * _sigmoid(x)


def _const_spec(shape):
    return pl.BlockSpec(shape, lambda *_: (0,) * len(shape), pipeline_mode=pl.Buffered(1))


def _mod_kernel(c_ref, w_ref, b_ref, o_ref):
    s = _silu(c_ref[...])
    o_ref[0] = jnp.dot(s, w_ref[0], precision=HIGHEST, preferred_element_type=F32) + b_ref[0]


def _modulation(cvec, w_mod, b_mod):
    tn = 1536
    return pl.pallas_call(
        _mod_kernel,
        out_shape=jax.ShapeDtypeStruct((DEPTH, MOD_ROWS, N_MOD), F32),
        grid=(DEPTH, N_MOD // tn),
        in_specs=[
            _const_spec((MOD_ROWS, D_MODEL)),
            pl.BlockSpec((1, D_MODEL, tn), lambda l, j: (l, 0, j)),
            pl.BlockSpec((1, 1, tn), lambda l, j: (l, 0, j)),
        ],
        out_specs=pl.BlockSpec((1, MOD_ROWS, tn), lambda l, j: (l, 0, j)),
        compiler_params=_cparams("parallel", "parallel"),
        name="modulation",
    )(cvec, w_mod, b_mod.reshape(DEPTH, 1, N_MOD))


def _head_norm(x, g):
    return x * lax.rsqrt(jnp.mean(x * x, axis=-1, keepdims=True) + EPS) * g


def _rope(x, cos, sin_signed):
    lane = lax.broadcasted_iota(jnp.int32, x.shape, 1)
    partner = jnp.where((lane % 64) < 32, pltpu.roll(x, 96, 1), pltpu.roll(x, 32, 1))
    return x * cos + partner * sin_signed


def _inproj_kernel(*refs, rope):
    if rope:
        (x_ref, mod_ref, g1_ref, w_ref, qg_ref, kg_ref, cos_ref, sin_ref,
         z_ref, q_ref, k_ref, v_ref, lg_ref, lx_ref) = refs
    else:
        (x_ref, mod_ref, g1_ref, w_ref, qg_ref, kg_ref,
         z_ref, q_ref, k_ref, v_ref, lg_ref, lx_ref) = refs
    x = x_ref[0]
    mod = mod_ref[0]
    shift, scale = mod[0:1], mod[1:2]
    y = x * lax.rsqrt(jnp.mean(x * x, axis=-1, keepdims=True) + EPS) * g1_ref[...]
    h = (y * (1.0 + scale) + shift).astype(BF16)

    def proj(c0, c1):
        return jnp.dot(h, w_ref[:, c0:c1], preferred_element_type=F32)

    glu = proj(0, 2 * D_CONV)
    z_ref[0] = glu[:, :D_CONV] * _sigmoid(glu[:, D_CONV:])

    c0 = 2 * D_CONV
    q = proj(c0, c0 + D_ATTN)
    qg = qg_ref[...]
    for hd in range(N_HEADS):
        qh = _head_norm(q[:, hd * HEAD_DIM:(hd + 1) * HEAD_DIM], qg)
        if rope:
            qh = _rope(qh, cos_ref[...], sin_ref[...])
        q_ref[0, :, hd * HEAD_DIM:(hd + 1) * HEAD_DIM] = (qh * ATTN_SCALE).astype(BF16)

    c0 += D_ATTN
    kv = proj(c0, c0 + 2 * D_KV)
    kg = kg_ref[...]
    for hd in range(N_KV_HEADS):
        kh = _head_norm(kv[:, hd * HEAD_DIM:(hd + 1) * HEAD_DIM], kg)
        if rope:
            kh = _rope(kh, cos_ref[...], sin_ref[...])
        k_ref[0, :, hd * HEAD_DIM:(hd + 1) * HEAD_DIM] = kh
    v_ref[0] = kv[:, D_KV:]

    c0 += 2 * D_KV
    l2 = proj(c0, c0 + 2 * D_LRU)
    lg_ref[0] = l2[:, :D_LRU]
    lx_ref[0] = l2[:, D_LRU:]


def _rope_tables(seq):
    n_freq = HEAD_DIM // 4
    inv = (ROPE_BASE ** (-np.arange(n_freq, dtype=np.float32) / n_freq)).astype(np.float32)
    t = np.arange(seq)
    ang_r = (t // GRID_W).astype(np.float32)[:, None] * inv[None, :]
    ang_c = (t % GRID_W).astype(np.float32)[:, None] * inv[None, :]
    cos = np.concatenate([np.cos(ang_r)] * 2 + [np.cos(ang_c)] * 2, axis=-1)
    sin = np.concatenate([-np.sin(ang_r), np.sin(ang_r), -np.sin(ang_c), np.sin(ang_c)], axis=-1)
    return jnp.asarray(cos, F32), jnp.asarray(sin, F32)


def _inproj(x, mods, row0, row_step, g1, w_in_bf, qg, kg, rope, tm):
    bsz, seq, _ = x.shape
    tok = lambda width: pl.BlockSpec((1, tm, width), lambda b, i: (b, i, 0))
    in_specs = [
        tok(D_MODEL),
        pl.BlockSpec((1, 6, D_MODEL), lambda b, i: (row0 + row_step * b, 0, 0)),
        _const_spec((1, D_MODEL)),
        _const_spec((D_MODEL, D_IN)),
        _const_spec((1, HEAD_DIM)),
        _const_spec((1, HEAD_DIM)),
    ]
    args = [x, mods, g1.reshape(1, D_MODEL), w_in_bf, qg.reshape(1, HEAD_DIM), kg.reshape(1, HEAD_DIM)]
    if rope:
        cos, sin = _rope_tables(seq)
        in_specs += [pl.BlockSpec((tm, HEAD_DIM), lambda b, i: (i, 0))] * 2
        args += [cos, sin]
    widths = (D_CONV, D_ATTN, D_KV, D_KV, D_LRU, D_LRU)
    dtypes = (F32, BF16, F32, F32, F32, F32)
    return pl.pallas_call(
        functools.partial(_inproj_kernel, rope=rope),
        out_shape=[jax.ShapeDtypeStruct((bsz, seq, w), dt) for w, dt in zip(widths, dtypes)],
        grid=(bsz, seq // tm),
        in_specs=in_specs,
        out_specs=[tok(w) for w in widths],
        compiler_params=_cparams("parallel", "parallel"),
        name="inproj_rope" if rope else "inproj",
    )(*args)


CONV_PAD = (CONV_K - 1) // 2
CONV_HALO = 16
CONV_CHUNK = 64


def _conv_kernel(z_ref, w_ref, b_ref, g_ref, beta_ref, o_ref, zp_ref, win_ref, *, seq):
    zeros = jnp.zeros((CONV_HALO, D_CONV), F32)
    zp_ref[0:CONV_HALO, :] = zeros
    zp_ref[CONV_HALO + seq:2 * CONV_HALO + seq, :] = zeros
    zp_ref[CONV_HALO:CONV_HALO + seq, :] = z_ref[0]

    def body(c, carry):
        r0 = pl.multiple_of(c * CONV_CHUNK, CONV_CHUNK)
        win_ref[...] = zp_ref[pl.ds(r0, CONV_CHUNK + 2 * CONV_HALO), :]
        acc = jnp.broadcast_to(b_ref[...], (CONV_CHUNK, D_CONV))
        for k in range(CONV_K):
            off = CONV_HALO - CONV_PAD + k
            acc = acc + jnp.tile(w_ref[k], (CONV_CHUNK // SUB, 1)) * win_ref[off:off + CONV_CHUNK, :]
        mu = jnp.mean(acc, axis=-1, keepdims=True)
        xc = acc - mu
        var = jnp.mean(xc * xc, axis=-1, keepdims=True)
        y = xc * lax.rsqrt(var + EPS) * g_ref[...] + beta_ref[...]
        o_ref[0, pl.ds(r0, CONV_CHUNK), :] = _silu(y).astype(BF16)
        return carry

    lax.fori_loop(0, seq // CONV_CHUNK, body, 0)


def _conv_module(z, w, b, g, beta):
    bsz, seq, _ = z.shape
    row = lambda a: a.reshape(1, D_CONV)
    return pl.pallas_call(
        functools.partial(_conv_kernel, seq=seq),
        out_shape=jax.ShapeDtypeStruct((bsz, seq, D_CONV), BF16),
        grid=(bsz,),
        in_specs=[
            pl.BlockSpec((1, seq, D_CONV), lambda i: (i, 0, 0)),
            _const_spec((CONV_K, SUB, D_CONV)),
            _const_spec((1, D_CONV)), _const_spec((1, D_CONV)), _const_spec((1, D_CONV)),
        ],
        out_specs=pl.BlockSpec((1, seq, D_CONV), lambda i: (i, 0, 0)),
        scratch_shapes=[pltpu.VMEM((seq + 2 * CONV_HALO, D_CONV), F32),
                        pltpu.VMEM((CONV_CHUNK + 2 * CONV_HALO, D_CONV), F32)],
        compiler_params=_cparams("parallel"),
        name="conv_module",
    )(z, jnp.broadcast_to(w[:, None, :], (CONV_K, SUB, D_CONV)), row(b), row(g), row(beta))


ATTN_ROWS = 64


def _attn_kernel(*refs, latent, seq, tq):
    if latent:
        sink_ref, q_ref, k_ref, v_ref, kc_ref, vc_ref, o_ref, s_ref, p_ref, bias_ref = refs
    else:
        sink_ref, q_ref, k_ref, v_ref, o_ref, s_ref, p_ref = refs
    q = q_ref[0]
    n_loc = 3 * tq if latent else 0
    if latent:
        n = pl.program_id(1)
        nblk = seq // tq
        blocks = (jnp.maximum(n - 1, 0), n, jnp.minimum(n + 1, nblk - 1))

        def window(ref):
            parts = [ref[0, pl.ds(pl.multiple_of(i * tq, tq), tq), :] for i in blocks]
            return jnp.concatenate(parts, axis=0)

        k_all = jnp.concatenate([window(k_ref), kc_ref[0]], axis=0).astype(BF16)
        v_all = jnp.concatenate([window(v_ref), vc_ref[0]], axis=0).astype(BF16)
        qpos = n * tq + lax.broadcasted_iota(jnp.int32, (tq, 3 * tq), 0)
        kpos = (n - 1) * tq + lax.broadcasted_iota(jnp.int32, (tq, 3 * tq), 1)
        ok = (jnp.abs(qpos - kpos) <= WINDOW) & (kpos >= 0) & (kpos < seq)
        bias_ref[...] = jnp.where(ok, 0.0, NEG).astype(F32)
    else:
        k_all = k_ref[0].astype(BF16)
        v_all = v_ref[0].astype(BF16)

    for j in range(N_KV_HEADS):
        heads = [j * GROUP + g for g in range(GROUP)]
        qs = jnp.concatenate([q[:, h * HEAD_DIM:(h + 1) * HEAD_DIM] for h in heads], axis=0)
        kj = k_all[:, j * HEAD_DIM:(j + 1) * HEAD_DIM]
        vj = v_all[:, j * HEAD_DIM:(j + 1) * HEAD_DIM]
        s_ref[...] = lax.dot_general(qs, kj, (((1,), (1,)), ((), ())), preferred_element_type=F32)
        inv = []
        for rb in range(GROUP * tq // ATTN_ROWS):
            rows = slice(rb * ATTN_ROWS, (rb + 1) * ATTN_ROWS)
            sk = sink_ref[heads[rb * ATTN_ROWS // tq]]
            parts = [s_ref[rows, n_loc:]]
            if latent:
                q0 = rb * ATTN_ROWS % tq
                parts.insert(0, s_ref[rows, :n_loc] + bias_ref[q0:q0 + ATTN_ROWS, :])
            m = sk
            for s in parts:
                m = jnp.maximum(m, jnp.max(s, axis=-1, keepdims=True))
            den = jnp.exp(sk - m)
            c0 = 0
            for s in parts:
                p = jnp.exp(s - m)
                den = den + jnp.sum(p, axis=-1, keepdims=True)
                p_ref[rows, c0:c0 + s.shape[1]] = p.astype(BF16)
                c0 += s.shape[1]
            inv.append(1.0 / den)
        o = jnp.dot(p_ref[...], vj, preferred_element_type=F32) * jnp.concatenate(inv, axis=0)
        for g, h in enumerate(heads):
            o_ref[0, :, h * HEAD_DIM:(h + 1) * HEAD_DIM] = o[g * tq:(g + 1) * tq].astype(BF16)


def _attention(q, k, v, sink, k_ctx=None, v_ctx=None):
    bsz, seq, _ = q.shape
    latent = k_ctx is not None
    tq = WINDOW if latent else seq
    seq_spec = pl.BlockSpec((1, seq, D_KV), lambda b, i: (b, 0, 0))
    in_specs = [
        pl.BlockSpec(memory_space=pltpu.SMEM),
        pl.BlockSpec((1, tq, D_ATTN), lambda b, i: (b, i, 0)),
        seq_spec, seq_spec,
    ]
    args = [sink, q, k, v]
    n_keys = seq
    if latent:
        past = k_ctx.shape[1]
        ctx_spec = pl.BlockSpec((1, past, D_KV), lambda b, i: (b, 0, 0))
        in_specs += [ctx_spec, ctx_spec]
        args += [k_ctx, v_ctx]
        n_keys = 3 * tq + past
    scratch = [pltpu.VMEM((GROUP * tq, n_keys), F32), pltpu.VMEM((GROUP * tq, n_keys), BF16)]
    if latent:
        scratch.append(pltpu.VMEM((tq, 3 * tq), F32))
    return pl.pallas_call(
        functools.partial(_attn_kernel, latent=latent, seq=seq, tq=tq),
        out_shape=jax.ShapeDtypeStruct((bsz, seq, D_ATTN), BF16),
        grid=(bsz, seq // tq),
        in_specs=in_specs,
        out_specs=pl.BlockSpec((1, tq, D_ATTN), lambda b, i: (b, i, 0)),
        scratch_shapes=scratch,
        compiler_params=_cparams("parallel", "parallel"),
        name="attn_latent" if latent else "attn_context",
    )(*args)


LRU_HALO = 8
LRU_CHUNK = 128
LRU_HALF = D_LRU // 2


def _softplus(x):
    return jnp.maximum(x, 0.0) + jnp.log(1.0 + jnp.exp(-jnp.abs(x)))


def _gelu_tanh(x):
    return 0.5 * x * (1.0 + jnp.tanh(0.7978845608028654 * (x + 0.044715 * (x * x * x))))


def _scan_tile(a, b, carry, reverse):
    row = lax.broadcasted_iota(jnp.int32, a.shape, 0)
    for d in (1, 2, 4):
        if reverse:
            valid = row < SUB - d
            shift = SUB - d
        else:
            valid = row >= d
            shift = d
        a_prev = jnp.where(valid, pltpu.roll(a, shift, 0), 1.0)
        b_prev = jnp.where(valid, pltpu.roll(b, shift, 0), 0.0)
        b = a * b_prev + b
        a = a * a_prev
    h = a * carry + b
    last = h[0:1, :] if reverse else h[SUB - 1:SUB, :]
    return h, last


def _lru_kernel(lx_ref, lg_ref, h0_ref, cw_ref, cb_ref, wbd_ref, gbias_ref, lam_ref,
                o_ref, fin_ref, xp_ref, af_ref, bf_ref, ab_ref, bb_ref, *, seq):
    zeros = jnp.zeros((LRU_HALO, D_LRU), F32)
    xp_ref[0:LRU_HALO, :] = zeros
    xp_ref[LRU_HALO + seq:2 * LRU_HALO + seq, :] = zeros
    xp_ref[LRU_HALO:LRU_HALO + seq, :] = lx_ref[0]
    a_refs = (af_ref, ab_ref)
    b_refs = (bf_ref, bb_ref)
    sp = _softplus(-lam_ref[...])

    def gates(c, carry):
        r0 = pl.multiple_of(c * LRU_CHUNK, LRU_CHUNK)
        xc = jnp.broadcast_to(cb_ref[...], (LRU_CHUNK, D_LRU))
        win = xp_ref[pl.ds(r0, LRU_CHUNK + 2 * LRU_HALO), :]
        for k in range(LRU_CONV_K):
            off = LRU_HALO - 2 + k
            xc = xc + cw_ref[k:k + 1, :] * win[off:off + LRU_CHUNK, :]
        for s in range(2):
            cols = slice(s * LRU_HALF, (s + 1) * LRU_HALF)
            xs = xc[:, cols]
            g = jnp.dot(xs.astype(BF16), wbd_ref[s], preferred_element_type=F32)
            g = g + gbias_ref[s:s + 1, :]
            for d in range(2):
                base = d * 2 * LRU_HALF
                r = _sigmoid(g[:, base:base + LRU_HALF])
                i = _sigmoid(g[:, base + LRU_HALF:base + 2 * LRU_HALF])
                log_a = -LRU_C * r * sp[d:d + 1, cols]
                a = jnp.exp(log_a)
                t = jnp.tanh(log_a)
                b = jnp.sqrt(-2.0 * t / (1.0 - t)) * (i * xs)
                a_refs[d][pl.ds(r0, LRU_CHUNK), cols] = a
                b_refs[d][pl.ds(r0, LRU_CHUNK), cols] = b
        return carry

    lax.fori_loop(0, seq // LRU_CHUNK, gates, 0)

    ntile = seq // SUB

    def scan(t, carry):
        cf, cb = carry
        rf = pl.multiple_of(t * SUB, SUB)
        rb = pl.multiple_of((ntile - 1 - t) * SUB, SUB)
        hf, cf = _scan_tile(af_ref[pl.ds(rf, SUB), :], bf_ref[pl.ds(rf, SUB), :], cf, False)
        hb, cb = _scan_tile(ab_ref[pl.ds(rb, SUB), :], bb_ref[pl.ds(rb, SUB), :], cb, True)
        bf_ref[pl.ds(rf, SUB), :] = hf
        bb_ref[pl.ds(rb, SUB), :] = hb
        return cf, cb

    h0 = h0_ref[0]
    cf, cb = lax.fori_loop(0, ntile, scan, (h0[0:1, :], h0[1:2, :]))
    fin_ref[0, 0:1, :] = cf
    fin_ref[0, 1:2, :] = cb

    def finish(c, carry):
        r0 = pl.multiple_of(c * LRU_CHUNK, LRU_CHUNK)
        rows = pl.ds(r0, LRU_CHUNK)
        o_ref[0, rows, :] = ((bf_ref[rows, :] + bb_ref[rows, :]) * _gelu_tanh(lg_ref[0, rows, :])).astype(BF16)
        return carry

    lax.fori_loop(0, seq // LRU_CHUNK, finish, 0)


def _lru_gate_weights(wa, ba, wx, bx):
    per_half = LRU_HALF // LRU_BLK

    def dense(w, s):
        out = jnp.zeros((LRU_HALF, LRU_HALF), F32)
        for n in range(per_half):
            out = lax.dynamic_update_slice(out, w[s * per_half + n], (n * LRU_BLK, n * LRU_BLK))
        return out

    wbd, gbias = [], []
    for s in range(2):
        cols = slice(s * LRU_HALF, (s + 1) * LRU_HALF)
        wbd.append(jnp.concatenate([dense(wa[0], s), dense(wx[0], s), dense(wa[1], s), dense(wx[1], s)], axis=1))
        gbias.append(jnp.concatenate([ba[0, cols], bx[0, cols], ba[1, cols], bx[1, cols]]))
    return jnp.stack(wbd).astype(BF16), jnp.stack(gbias)


def _recurrent_mixer(lx, lg, h0, cw, cb, wa, ba, wx, bx, lam):
    bsz, seq, _ = lx.shape
    wbd, gbias = _lru_gate_weights(wa, ba, wx, bx)
    seq_spec = pl.BlockSpec((1, seq, D_LRU), lambda i: (i, 0, 0))
    state_spec = pl.BlockSpec((1, 2, D_LRU), lambda i: (i, 0, 0))
    return pl.pallas_call(
        functools.partial(_lru_kernel, seq=seq),
        out_shape=[jax.ShapeDtypeStruct((bsz, seq, D_LRU), BF16),
                   jax.ShapeDtypeStruct((bsz, 2, D_LRU), F32)],
        grid=(bsz,),
        in_specs=[
            seq_spec, seq_spec, state_spec,
            _const_spec((LRU_CONV_K, D_LRU)), _const_spec((1, D_LRU)),
            _const_spec((2, LRU_HALF, 4 * LRU_HALF)), _const_spec((2, 4 * LRU_HALF)),
            _const_spec((2, D_LRU)),
        ],
        out_specs=[seq_spec, state_spec],
        scratch_shapes=[pltpu.VMEM((seq + 2 * LRU_HALO, D_LRU), F32)] + [pltpu.VMEM((seq, D_LRU), F32)] * 4,
        compiler_params=_cparams("parallel"),
        name="rglru",
    )(lx, lg, h0, cw, cb.reshape(1, D_LRU), wbd, gbias, lam)


N_PAIRS = EXPERTS_PER_GROUP * (EXPERTS_PER_GROUP - 1) // 2
N_BUCKETS = N_EXPERT_GROUPS * N_PAIRS
PAIR_LO = (0, 0, 0, 1, 1, 2)
PAIR_HI = (1, 2, 3, 2, 3, 3)
ROUTE_ROWS = SUB


def _route(scores, biased):
    rows = [biased[e:e + 1, :] for e in range(N_EXPERTS)]
    group_score = []
    for g in range(N_EXPERT_GROUPS):
        a, b, c, d = rows[4 * g:4 * g + 4]
        hi1, lo1 = jnp.maximum(a, b), jnp.minimum(a, b)
        hi2, lo2 = jnp.maximum(c, d), jnp.minimum(c, d)
        top = jnp.maximum(hi1, hi2)
        second = jnp.maximum(jnp.minimum(hi1, hi2), jnp.maximum(lo1, lo2))
        group_score.append(top + second)
    best = group_score[0]
    g_sel = jnp.zeros_like(best, dtype=jnp.int32)
    for g in range(1, N_EXPERT_GROUPS):
        better = group_score[g] > best
        g_sel = jnp.where(better, g, g_sel)
        best = jnp.where(better, group_score[g], best)
    sel = []
    for e in range(N_EXPERTS):
        g = e // EXPERTS_PER_GROUP
        rank = jnp.zeros_like(g_sel)
        for o in range(g * EXPERTS_PER_GROUP, (g + 1) * EXPERTS_PER_GROUP):
            if o == e:
                continue
            ahead = (rows[o] >= rows[e]) if o < e else (rows[o] > rows[e])
            rank = rank + ahead.astype(jnp.int32)
        sel.append(jnp.where(jnp.where(g_sel == g, rank, 2) < 2, 1, 0))
    zero = jnp.zeros_like(best)
    lo_w, hi_w = zero, zero
    lo_idx = jnp.zeros_like(g_sel)
    hi_idx = jnp.zeros_like(g_sel)
    for g in range(N_EXPERT_GROUPS):
        seen = jnp.zeros_like(g_sel)
        for j in range(EXPERTS_PER_GROUP):
            e = g * EXPERTS_PER_GROUP + j
            order = jnp.where(sel[e] == 1, seen, 2)
            lo_w = jnp.where(order == 0, scores[e:e + 1, :], lo_w)
            hi_w = jnp.where(order == 1, scores[e:e + 1, :], hi_w)
            lo_idx = jnp.where(order == 0, j, lo_idx)
            hi_idx = jnp.where(order == 1, j, hi_idx)
            seen = seen + sel[e]
    pair_base = jnp.where(lo_idx == 0, 0, jnp.where(lo_idx == 1, 3, 5))
    bucket = g_sel * N_PAIRS + pair_base + hi_idx - lo_idx - 1
    total = lo_w + hi_w
    pad = jnp.zeros((ROUTE_ROWS - 3, best.shape[1]), F32)
    return jnp.concatenate([bucket.astype(F32), lo_w / total, hi_w / total, pad], axis=0)


def _modulated_norm(x, g, shift, scale):
    y = x * lax.rsqrt(jnp.mean(x * x, axis=-1, keepdims=True) + EPS) * g
    return y * (1.0 + scale) + shift


def _split_bf16(x):
    hi = x.astype(BF16)
    return hi, (x - hi.astype(F32)).astype(BF16)


def _outproj_kernel(conv_ref, attn_ref, lru_ref, x_ref, mod_ref, g2_ref, w_ref, wr_ref, br_ref,
                    x1_ref, route_ref, cat_ref):
    cat_ref[:, 0:D_CONV] = conv_ref[0]
    cat_ref[:, D_CONV:D_CONV + D_ATTN] = attn_ref[0]
    cat_ref[:, D_CONV + D_ATTN:] = lru_ref[0]
    mix = jnp.dot(cat_ref[...], w_ref[...], preferred_element_type=F32)
    mod = mod_ref[0]
    x1 = x_ref[0] + mod[2:3] * mix
    x1_ref[0] = x1
    h2 = _modulated_norm(x1, g2_ref[...], mod[3:4], mod[4:5])
    h_hi, h_lo = _split_bf16(h2)
    nt = (((1,), (1,)), ((), ()))
    by_hi = lax.dot_general(wr_ref[...], h_hi, nt, preferred_element_type=F32)
    by_lo = lax.dot_general(wr_ref[0:N_EXPERTS, :], h_lo, nt, preferred_element_type=F32)
    logits = by_hi[0:N_EXPERTS] + by_hi[N_EXPERTS:] + by_lo
    scores = _sigmoid(logits)
    route_ref[0] = _route(scores, scores + br_ref[...])


def _outproj(conv, attn, lru, x, mods, row0, row_step, g2, w_out_bf, w_router_t, b_router, tm):
    bsz, seq, _ = x.shape
    tok = lambda width: pl.BlockSpec((1, tm, width), lambda b, i: (b, i, 0))
    return pl.pallas_call(
        _outproj_kernel,
        out_shape=[jax.ShapeDtypeStruct((bsz, seq, D_MODEL), F32),
                   jax.ShapeDtypeStruct((bsz, ROUTE_ROWS, seq), F32)],
        grid=(bsz, seq // tm),
        in_specs=[
            tok(D_CONV), tok(D_ATTN), tok(D_LRU), tok(D_MODEL),
            pl.BlockSpec((1, 6, D_MODEL), lambda b, i: (row0 + row_step * b, 0, 0)),
            _const_spec((1, D_MODEL)),
            _const_spec((D_MODEL, D_MODEL)),
            _const_spec((2 * N_EXPERTS, D_MODEL)),
            _const_spec((N_EXPERTS, 1)),
        ],
        out_specs=[tok(D_MODEL),
                   pl.BlockSpec((1, ROUTE_ROWS, tm), lambda b, i: (b, 0, i))],
        scratch_shapes=[pltpu.VMEM((tm, D_MODEL), BF16)],
        compiler_params=_cparams("parallel", "parallel"),
        name="outproj_router",
    )(conv, attn, lru, x, mods, g2.reshape(1, D_MODEL), w_out_bf, w_router_t, b_router.reshape(N_EXPERTS, 1))


MOE_TILE = 256
MOE_UNROLL = 8
META_COLS = 4


def _moe_plan(bucket, w_lo, w_hi, mod_row, t_ctx, tm):
    t_all = bucket.shape[0]
    n_tiles = (t_all + N_BUCKETS * (tm - 1) + tm - 1) // tm
    ids = jnp.arange(N_BUCKETS, dtype=jnp.int32)
    order = jnp.argsort(bucket, stable=True).astype(jnp.int32)
    counts = jnp.sum((bucket[None, :] == ids[:, None]).astype(jnp.int32), axis=1)
    tiles_per = (counts + tm - 1) // tm
    tile_end = jnp.cumsum(tiles_per)
    tile_start = tile_end - tiles_per
    sorted_start = jnp.cumsum(counts) - counts
    n_used = tile_end[-1]
    j = jnp.arange(n_tiles, dtype=jnp.int32)
    jj = jnp.minimum(j, n_used - 1)
    b_of = jnp.sum((tile_end[None, :] <= jj[:, None]).astype(jnp.int32), axis=1)
    local = jj - tile_start[b_of]
    n_valid = jnp.where(j < n_used, jnp.clip(counts[b_of] - local * tm, 0, tm), 0)
    rows = (sorted_start[b_of] + local * tm)[:, None] + jnp.arange(tm, dtype=jnp.int32)[None, :]
    tok = order[jnp.clip(rows, 0, t_all - 1)]
    valid = jnp.arange(tm, dtype=jnp.int32)[None, :] < n_valid[:, None]
    n_ctx = jnp.sum((valid & (tok < t_ctx)).astype(jnp.int32), axis=1)
    group, pair = b_of // N_PAIRS, b_of % N_PAIRS
    e_lo = group * EXPERTS_PER_GROUP + jnp.asarray(PAIR_LO, jnp.int32)[pair]
    e_hi = group * EXPERTS_PER_GROUP + jnp.asarray(PAIR_HI, jnp.int32)[pair]
    meta = jnp.stack([w_lo[tok], w_hi[tok], mod_row[tok], jnp.zeros(tok.shape, F32)], axis=-1)
    i32 = lambda a: a.astype(jnp.int32)
    return i32(tok.reshape(-1)), i32(e_lo), i32(e_hi), i32(n_valid), i32(n_ctx), meta


def _moe_kernel(tok_ref, elo_ref, ehi_ref, nv_ref, nc_ref,
                xc_ref, xl_ref, meta_ref, mods_ref, g2_ref,
                wg_lo, wu_lo, wd_lo, wg_hi, wu_hi, wd_hi,
                oc_ref, ol_ref, xbuf, ybuf, gsem, ssem, *, tm, t_ctx, n_mod_rows):
    i = pl.program_id(0)
    n = pl.num_programs(0)
    slot = lax.rem(i, 2)

    def gather(hbm, t, r, s, size):
        return pltpu.make_async_copy(hbm.at[pl.ds(t, size), :], xbuf.at[s, pl.ds(r, size), :], gsem.at[s])

    def scatter(hbm, t, r, s, size):
        return pltpu.make_async_copy(ybuf.at[s, pl.ds(r, size), :], hbm.at[pl.ds(t, size), :], ssem.at[s])

    def start_rows(j, s, copy, hbm_ctx, hbm_lat):
        base = j * tm

        def ctx_row(r):
            copy(hbm_ctx, tok_ref[base + r], r, s, 1).start()

        def lat_row(r):
            copy(hbm_lat, tok_ref[base + r] - t_ctx, r, s, 1).start()

        def rows(lo, hi, one_row):
            groups = lax.shift_right_logical(hi - lo, MOE_UNROLL.bit_length() - 1)

            def group(g, c):
                for u in range(MOE_UNROLL):
                    one_row(lo + g * MOE_UNROLL + u)
                return c

            def single(r, c):
                one_row(r)
                return c

            lax.fori_loop(0, groups, group, 0)
            lax.fori_loop(lo + groups * MOE_UNROLL, hi, single, 0)

        rows(0, nc_ref[j], ctx_row)
        rows(nc_ref[j], nv_ref[j], lat_row)

    def wait_rows(j, s, copy, hbm):
        cnt = nv_ref[j]
        bulk = pl.multiple_of(lax.shift_left(lax.shift_right_logical(cnt, 3), 3), SUB)

        @pl.when(bulk > 0)
        def _():
            copy(hbm, 0, 0, s, bulk).wait()

        def one(r, c):
            copy(hbm, 0, 0, s, 1).wait()
            return c

        lax.fori_loop(bulk, cnt, one, 0)

    @pl.when(i == 0)
    def _():
        xbuf[...] = jnp.zeros_like(xbuf)
        start_rows(0, 0, gather, xc_ref, xl_ref)

    @pl.when(i + 1 < n)
    def _():
        start_rows(i + 1, 1 - slot, gather, xc_ref, xl_ref)

    wait_rows(i, slot, gather, xl_ref)

    @pl.when(i >= 2)
    def _():
        wait_rows(i - 2, slot, scatter, ol_ref)

    @pl.when(nv_ref[i] > 0)
    def _():
        x = xbuf[slot]
        meta = meta_ref[0]
        w_lo, w_hi, mrow = meta[:, 0:1], meta[:, 1:2], meta[:, 2:3]

        def per_row(k):
            out = jnp.broadcast_to(mods_ref[0, k:k + 1, :], x.shape)
            for r in range(1, n_mod_rows):
                out = jnp.where(mrow == float(r), mods_ref[r, k:k + 1, :], out)
            return out

        h = _modulated_norm(x, g2_ref[...], per_row(3), per_row(4)).astype(BF16)

        def expert(wg, wu, w):
            act = _silu(jnp.dot(h, wg[0, 0], preferred_element_type=F32))
            act = act * jnp.dot(h, wu[0, 0], preferred_element_type=F32) * w
            return act.astype(BF16)

        y = jnp.dot(expert(wg_lo, wu_lo, w_lo), wd_lo[0, 0], preferred_element_type=F32)
        y = y + jnp.dot(expert(wg_hi, wu_hi, w_hi), wd_hi[0, 0], preferred_element_type=F32)
        ybuf[slot] = x + per_row(5) * y

    start_rows(i, slot, scatter, oc_ref, ol_ref)

    @pl.when(i == n - 1)
    def _():
        wait_rows(i, slot, scatter, ol_ref)

        @pl.when(i >= 1)
        def _():
            wait_rows(i - 1, 1 - slot, scatter, ol_ref)


def _moe(x1c, x1l, route_c, route_l, mods, g2, wg, wu, wd, layer, dec_seq):
    t_ctx, t_lat = x1c.shape[0], x1l.shape[0]
    tm = MOE_TILE
    flat = lambda route, k: route[:, k, :].reshape(-1)
    both = lambda k: jnp.concatenate([flat(route_c, k), flat(route_l, k)])
    mod_row = np.concatenate([np.zeros(t_ctx), 1 + np.arange(t_lat) // dec_seq]).astype(np.float32)
    n_mod_rows = 1 + t_lat // dec_seq
    tok, e_lo, e_hi, n_valid, n_ctx, meta = _moe_plan(
        both(0).astype(jnp.int32), both(1), both(2), jnp.asarray(mod_row), t_ctx, tm)
    n_tiles = n_valid.shape[0]
    w_up = lambda sel: pl.BlockSpec((1, 1, D_MODEL, D_EXPERT),
                                    lambda i, tok, lo, hi, nv, nc: (layer, (lo, hi)[sel][i], 0, 0))
    w_dn = lambda sel: pl.BlockSpec((1, 1, D_EXPERT, D_MODEL),
                                    lambda i, tok, lo, hi, nv, nc: (layer, (lo, hi)[sel][i], 0, 0))
    hbm = pl.BlockSpec(memory_space=pl.ANY)
    return pl.pallas_call(
        functools.partial(_moe_kernel, tm=tm, t_ctx=t_ctx, n_mod_rows=n_mod_rows),
        out_shape=[jax.ShapeDtypeStruct((t_ctx, D_MODEL), F32), jax.ShapeDtypeStruct((t_lat, D_MODEL), F32)],
        grid_spec=pltpu.PrefetchScalarGridSpec(
            num_scalar_prefetch=5,
            grid=(n_tiles,),
            in_specs=[
                hbm, hbm,
                pl.BlockSpec((1, tm, META_COLS), lambda i, *_: (i, 0, 0)),
                _const_spec((MOD_ROWS, 6, D_MODEL)),
                _const_spec((1, D_MODEL)),
                w_up(0), w_up(0), w_dn(0), w_up(1), w_up(1), w_dn(1),
            ],
            out_specs=[hbm, hbm],
            scratch_shapes=[
                pltpu.VMEM((2, tm, D_MODEL), F32), pltpu.VMEM((2, tm, D_MODEL), F32),
                pltpu.SemaphoreType.DMA((2,)), pltpu.SemaphoreType.DMA((2,)),
            ],
        ),
        compiler_params=_cparams("arbitrary"),
        name="moe_pairs",
    )(tok, e_lo, e_hi, n_valid, n_ctx, x1c, x1l, meta, mods, g2.reshape(1, D_MODEL), wg, wu, wd, wg, wu, wd)


def _mixers(x, p, mods, row0, row_step, h0, k_ctx=None, v_ctx=None):
    latent = k_ctx is not None
    bsz, seq, _ = x.shape
    tm = min(seq, 512)
    z, q, k, v, lg, lx = _inproj(x, mods, row0, row_step, p["norm1_g"], p["w_in"], p["q_norm_g"],
                                 p["k_norm_g"], latent, tm)
    conv = _conv_module(z, p["conv_dw"], p["conv_b"], p["conv_ln_g"], p["conv_ln_b"])
    attn = _attention(q, k, v, p["attn_sink"], k_ctx, v_ctx)
    lru, fin = _recurrent_mixer(lx, lg, h0, p["lru_conv_w"], p["lru_conv_b"], p["lru_wa"], p["lru_ba"],
                                p["lru_wx"], p["lru_bx"], p["lru_lam"])
    x1, route = _outproj(conv, attn, lru, x, mods, row0, row_step, p["norm2_g"], p["w_out"],
                         p["w_router_t"], p["b_router"], tm)
    return x1, route, k, v, fin


def kernel(x_prompt, x_sample, c, cache_k, cache_v, state_lru, c_ctx, w_mod, b_mod, norm1_g, norm2_g, w_in,
           conv_dw, conv_b, conv_ln_g, conv_ln_b, q_norm_g, k_norm_g, attn_sink, lru_conv_w, lru_conv_b,
           lru_wa, lru_ba, lru_wx, lru_bx, lru_lam, w_out, w_router, b_router, w_gate_e, w_up_e, w_down_e):
    bsz, seq, _ = x_prompt.shape
    dec_bsz, dec_seq, _ = x_sample.shape
    past = cache_k.shape[2]

    cvec = jnp.zeros((MOD_ROWS, D_MODEL), F32).at[0].set(c_ctx).at[1:1 + dec_bsz].set(c)
    mods_all = _modulation(cvec, w_mod, b_mod).reshape(DEPTH, MOD_ROWS, 6, D_MODEL)

    wr_hi = w_router.T.astype(BF16)
    wr_lo = (w_router.T - wr_hi.astype(F32)).astype(BF16)
    w_router_t = jnp.concatenate([wr_hi, wr_lo], axis=0)
    layers = []
    for l in range(DEPTH):
        layers.append({
            "norm1_g": norm1_g[l], "norm2_g": norm2_g[l], "w_in": w_in[l].astype(BF16),
            "conv_dw": conv_dw[l], "conv_b": conv_b[l], "conv_ln_g": conv_ln_g[l], "conv_ln_b": conv_ln_b[l],
            "q_norm_g": q_norm_g[l], "k_norm_g": k_norm_g[l], "attn_sink": attn_sink[l],
            "lru_conv_w": lru_conv_w[l], "lru_conv_b": lru_conv_b[l], "lru_wa": lru_wa[l], "lru_ba": lru_ba[l],
            "lru_wx": lru_wx[l], "lru_bx": lru_bx[l], "lru_lam": lru_lam[l],
            "w_out": w_out[l].astype(BF16), "w_router_t": w_router_t, "b_router": b_router,
        })
    wg_bf, wu_bf, wd_bf = w_gate_e.astype(BF16), w_up_e.astype(BF16), w_down_e.astype(BF16)

    y, z = x_prompt, x_sample
    ks, vs, hs = [], [], []
    h0_ctx = jnp.zeros((bsz, 2, D_LRU), F32)
    for l in range(DEPTH):
        p, mods = layers[l], mods_all[l]
        y1, route_c, k_l, v_l, h_l = _mixers(y, p, mods, 0, 0, h0_ctx)
        ks.append(k_l.reshape(bsz, seq, N_KV_HEADS, HEAD_DIM))
        vs.append(v_l.reshape(bsz, seq, N_KV_HEADS, HEAD_DIM))
        hs.append(h_l)
        z1, route_l, _, _, _ = _mixers(z, p, mods, 1, 1, state_lru[:, l],
                                       cache_k[:, l].reshape(dec_bsz, past, D_KV),
                                       cache_v[:, l].reshape(dec_bsz, past, D_KV))
        y, z = _moe(y1.reshape(bsz * seq, D_MODEL), z1.reshape(dec_bsz * dec_seq, D_MODEL), route_c, route_l,
                    mods, p["norm2_g"], wg_bf, wu_bf, wd_bf, l, dec_seq)
        y, z = y.reshape(bsz, seq, D_MODEL), z.reshape(dec_bsz, dec_seq, D_MODEL)
    new_cache_k = jnp.stack(ks, axis=1)
    new_cache_v = jnp.stack(vs, axis=1)
    new_state_lru = jnp.stack(hs, axis=1)
    return y, z, new_cache_k, new_cache_v, new_state_lru
```

```python
import functools

import numpy as np
import jax
import jax.numpy as jnp
from jax import lax
from jax.experimental import pallas as pl
from jax.experimental.pallas import tpu as pltpu

D_MODEL = 2048
DEPTH = 2
GRID_W = 64
D_CONV = 512
N_HEADS = 8
N_KV_HEADS = 2
HEAD_DIM = 128
GROUP = N_HEADS // N_KV_HEADS
D_ATTN = N_HEADS * HEAD_DIM
D_KV = N_KV_HEADS * HEAD_DIM
D_LRU = 512
CONV_K = 31
LRU_CONV_K = 4
LRU_BLOCKS = 8
LRU_BLK = D_LRU // LRU_BLOCKS
LRU_C = 8.0
WINDOW = 128
ROPE_BASE = 10000.0
ATTN_SCALE = HEAD_DIM ** -0.5
N_EXPERTS = 16
N_EXPERT_GROUPS = 4
EXPERTS_PER_GROUP = N_EXPERTS // N_EXPERT_GROUPS
D_EXPERT = 512
EPS = 1e-6
NEG = -1e30
D_IN = 2 * D_CONV + D_ATTN + 2 * D_KV + 2 * D_LRU
N_MOD = 6 * D_MODEL
SUB = 8
MOD_ROWS = SUB

V7X_VMEM_BYTES = 64 * 1024 * 1024
VMEM_LIMIT = V7X_VMEM_BYTES - 8 * 1024 * 1024

F32 = jnp.float32
BF16 = jnp.bfloat16
HIGHEST = lax.Precision.HIGHEST


def _cparams(*sem):
    return pltpu.CompilerParams(dimension_semantics=sem, vmem_limit_bytes=VMEM_LIMIT)


def _sigmoid(x):
    return 0.5 * jnp.tanh(0.5 * x) + 0.5


def _silu(x):
    return x * _sigmoid(x)


def _const_spec(shape):
    return pl.BlockSpec(shape, lambda *_: (0,) * len(shape), pipeline_mode=pl.Buffered(1))


def _split_bf16(x):
    hi = x.astype(BF16)
    return hi, (x - hi.astype(F32)).astype(BF16)


def _mod_kernel(c_ref, w_ref, b_ref, o_ref):
    s = _silu(c_ref[...])
    s_hi = s.astype(BF16).astype(F32)
    s_both = jnp.concatenate([s_hi, s - s_hi], axis=0).astype(BF16)
    w_hi, w_lo = _split_bf16(w_ref[0])
    by_hi = jnp.dot(s_both, w_hi, preferred_element_type=F32)
    by_lo = jnp.dot(s_both, w_lo, preferred_element_type=F32)
    o_ref[0] = by_hi[0:MOD_ROWS] + by_hi[MOD_ROWS:] + by_lo[0:MOD_ROWS] + b_ref[0]


def _modulation(cvec, w_mod, b_mod):
    tn = 1536
    return pl.pallas_call(
        _mod_kernel,
        out_shape=jax.ShapeDtypeStruct((DEPTH, MOD_ROWS, N_MOD), F32),
        grid=(DEPTH, N_MOD // tn),
        in_specs=[
            _const_spec((MOD_ROWS, D_MODEL)),
            pl.BlockSpec((1, D_MODEL, tn), lambda l, j: (l, 0, j)),
            pl.BlockSpec((1, 1, tn), lambda l, j: (l, 0, j)),
        ],
        out_specs=pl.BlockSpec((1, MOD_ROWS, tn), lambda l, j: (l, 0, j)),
        compiler_params=_cparams("parallel", "parallel"),
        name="modulation",
    )(cvec, w_mod, b_mod.reshape(DEPTH, 1, N_MOD))


def _head_norm(x, g):
    return x * lax.rsqrt(jnp.mean(x * x, axis=-1, keepdims=True) + EPS) * g


def _rope(x, cos, sin_signed):
    lane = lax.broadcasted_iota(jnp.int32, x.shape, 1)
    partner = jnp.where((lane % 64) < 32, pltpu.roll(x, 96, 1), pltpu.roll(x, 32, 1))
    return x * cos + partner * sin_signed


def _inproj_kernel(*refs, rope, n_cast):
    refs = list(refs)
    cast_dst = [refs.pop() for _ in range(n_cast)][::-1]
    z_ref, q_ref, k_ref, v_ref, lg_ref, lx_ref = refs[-6:]
    del refs[-6:]
    cast_src = [refs.pop() for _ in range(n_cast)][::-1]
    if rope:
        x_ref, mod_ref, g1_ref, w_ref, qg_ref, kg_ref, cos_ref, sin_ref = refs
    else:
        x_ref, mod_ref, g1_ref, w_ref, qg_ref, kg_ref = refs
    for src, dst in zip(cast_src, cast_dst):
        dst[0] = src[0, 0].astype(BF16)
    x = x_ref[0]
    mod = mod_ref[0]
    shift, scale = mod[0:1], mod[1:2]
    y = x * lax.rsqrt(jnp.mean(x * x, axis=-1, keepdims=True) + EPS) * g1_ref[...]
    h = (y * (1.0 + scale) + shift).astype(BF16)

    def proj(c0, c1):
        return jnp.dot(h, w_ref[:, c0:c1], preferred_element_type=F32)

    glu = proj(0, 2 * D_CONV)
    z_ref[0] = glu[:, :D_CONV] * _sigmoid(glu[:, D_CONV:])

    c0 = 2 * D_CONV
    q = proj(c0, c0 + D_ATTN)
    qg = qg_ref[...]
    for hd in range(N_HEADS):
        qh = _head_norm(q[:, hd * HEAD_DIM:(hd + 1) * HEAD_DIM], qg)
        if rope:
            qh = _rope(qh, cos_ref[...], sin_ref[...])
        q_ref[0, :, hd * HEAD_DIM:(hd + 1) * HEAD_DIM] = (qh * ATTN_SCALE).astype(BF16)

    c0 += D_ATTN
    kv = proj(c0, c0 + 2 * D_KV)
    kg = kg_ref[...]
    for hd in range(N_KV_HEADS):
        kh = _head_norm(kv[:, hd * HEAD_DIM:(hd + 1) * HEAD_DIM], kg)
        if rope:
            kh = _rope(kh, cos_ref[...], sin_ref[...])
        k_ref[0, :, hd * HEAD_DIM:(hd + 1) * HEAD_DIM] = kh
    v_ref[0] = kv[:, D_KV:]

    c0 += 2 * D_KV
    l2 = proj(c0, c0 + 2 * D_LRU)
    lg_ref[0] = l2[:, :D_LRU]
    lx_ref[0] = l2[:, D_LRU:]


def _rope_tables(seq):
    n_freq = HEAD_DIM // 4
    inv = (ROPE_BASE ** (-np.arange(n_freq, dtype=np.float32) / n_freq)).astype(np.float32)
    t = np.arange(seq)
    ang_r = (t // GRID_W).astype(np.float32)[:, None] * inv[None, :]
    ang_c = (t % GRID_W).astype(np.float32)[:, None] * inv[None, :]
    cos = np.concatenate([np.cos(ang_r)] * 2 + [np.cos(ang_c)] * 2, axis=-1)
    sin = np.concatenate([-np.sin(ang_r), np.sin(ang_r), -np.sin(ang_c), np.sin(ang_c)], axis=-1)
    return jnp.asarray(cos, F32), jnp.asarray(sin, F32)


def _cast_specs(w, layer, n_inner, n_steps):
    _, n_e, rows, cols = w.shape
    parts = n_steps // n_e
    assert parts * n_e == n_steps and rows % parts == 0
    piece = lambda b, i: divmod(b * n_inner + i, parts)
    src = pl.BlockSpec((1, 1, rows // parts, cols), lambda b, i: (layer, *piece(b, i), 0))
    dst = pl.BlockSpec((1, rows // parts, cols), lambda b, i: (*piece(b, i), 0))
    return src, dst, jax.ShapeDtypeStruct((n_e, rows, cols), BF16)


def _inproj(x, mods, row0, row_step, g1, w_in_bf, qg, kg, rope, tm, layer, cast_weights):
    bsz, seq, _ = x.shape
    tok = lambda width: pl.BlockSpec((1, tm, width), lambda b, i: (b, i, 0))
    in_specs = [
        tok(D_MODEL),
        pl.BlockSpec((1, 6, D_MODEL), lambda b, i: (row0 + row_step * b, 0, 0)),
        _const_spec((1, D_MODEL)),
        _const_spec((D_MODEL, D_IN)),
        _const_spec((1, HEAD_DIM)),
        _const_spec((1, HEAD_DIM)),
    ]
    args = [x, mods, g1.reshape(1, D_MODEL), w_in_bf, qg.reshape(1, HEAD_DIM), kg.reshape(1, HEAD_DIM)]
    if rope:
        cos, sin = _rope_tables(seq)
        in_specs += [pl.BlockSpec((tm, HEAD_DIM), lambda b, i: (i, 0))] * 2
        args += [cos, sin]
    widths = (D_CONV, D_ATTN, D_KV, D_KV, D_LRU, D_LRU)
    dtypes = (F32, BF16, F32, F32, F32, F32)
    out_shape = [jax.ShapeDtypeStruct((bsz, seq, w), dt) for w, dt in zip(widths, dtypes)]
    out_specs = [tok(w) for w in widths]
    n_inner = seq // tm
    for w in cast_weights:
        src, dst, shape = _cast_specs(w, layer, n_inner, bsz * n_inner)
        in_specs.append(src)
        args.append(w)
        out_specs.append(dst)
        out_shape.append(shape)
    outs = pl.pallas_call(
        functools.partial(_inproj_kernel, rope=rope, n_cast=len(cast_weights)),
        out_shape=out_shape,
        grid=(bsz, n_inner),
        in_specs=in_specs,
        out_specs=out_specs,
        compiler_params=_cparams("parallel", "parallel"),
        name="inproj_rope" if rope else "inproj",
    )(*args)
    return outs[:6], outs[6:]


CONV_PAD = (CONV_K - 1) // 2
CONV_HALO = 16
CONV_CHUNK = 64


def _conv_kernel(z_ref, w_ref, b_ref, g_ref, beta_ref, o_ref, zp_ref, win_ref, *, seq):
    zeros = jnp.zeros((CONV_HALO, D_CONV), F32)
    zp_ref[0:CONV_HALO, :] = zeros
    zp_ref[CONV_HALO + seq:2 * CONV_HALO + seq, :] = zeros
    zp_ref[CONV_HALO:CONV_HALO + seq, :] = z_ref[0]

    def body(c, carry):
        r0 = pl.multiple_of(c * CONV_CHUNK, CONV_CHUNK)
        win_ref[...] = zp_ref[pl.ds(r0, CONV_CHUNK + 2 * CONV_HALO), :]
        acc = jnp.broadcast_to(b_ref[...], (CONV_CHUNK, D_CONV))
        for k in range(CONV_K):
            off = CONV_HALO - CONV_PAD + k
            acc = acc + jnp.tile(w_ref[k], (CONV_CHUNK // SUB, 1)) * win_ref[off:off + CONV_CHUNK, :]
        mu = jnp.mean(acc, axis=-1, keepdims=True)
        xc = acc - mu
        var = jnp.mean(xc * xc, axis=-1, keepdims=True)
        y = xc * lax.rsqrt(var + EPS) * g_ref[...] + beta_ref[...]
        o_ref[0, pl.ds(r0, CONV_CHUNK), :] = _silu(y).astype(BF16)
        return carry

    lax.fori_loop(0, seq // CONV_CHUNK, body, 0)


def _conv_module(z, w, b, g, beta):
    bsz, seq, _ = z.shape
    row = lambda a: a.reshape(1, D_CONV)
    return pl.pallas_call(
        functools.partial(_conv_kernel, seq=seq),
        out_shape=jax.ShapeDtypeStruct((bsz, seq, D_CONV), BF16),
        grid=(bsz,),
        in_specs=[
            pl.BlockSpec((1, seq, D_CONV), lambda i: (i, 0, 0)),
            _const_spec((CONV_K, SUB, D_CONV)),
            _const_spec((1, D_CONV)), _const_spec((1, D_CONV)), _const_spec((1, D_CONV)),
        ],
        out_specs=pl.BlockSpec((1, seq, D_CONV), lambda i: (i, 0, 0)),
        scratch_shapes=[pltpu.VMEM((seq + 2 * CONV_HALO, D_CONV), F32),
                        pltpu.VMEM((CONV_CHUNK + 2 * CONV_HALO, D_CONV), F32)],
        compiler_params=_cparams("parallel"),
        name="conv_module",
    )(z, jnp.broadcast_to(w[:, None, :], (CONV_K, SUB, D_CONV)), row(b), row(g), row(beta))


ATTN_ROWS = 64


def _attn_kernel(*refs, latent, seq, tq):
    if latent:
        sink_ref, q_ref, k_ref, v_ref, kc_ref, vc_ref, o_ref, s_ref, p_ref, bias_ref = refs
    else:
        sink_ref, q_ref, k_ref, v_ref, o_ref, s_ref, p_ref = refs
    q = q_ref[0]
    n_loc = 3 * tq if latent else 0
    if latent:
        n = pl.program_id(1)
        nblk = seq // tq
        blocks = (jnp.maximum(n - 1, 0), n, jnp.minimum(n + 1, nblk - 1))

        def window(ref):
            parts = [ref[0, pl.ds(pl.multiple_of(i * tq, tq), tq), :] for i in blocks]
            return jnp.concatenate(parts, axis=0)

        k_all = jnp.concatenate([window(k_ref), kc_ref[0]], axis=0).astype(BF16)
        v_all = jnp.concatenate([window(v_ref), vc_ref[0]], axis=0).astype(BF16)
        qpos = n * tq + lax.broadcasted_iota(jnp.int32, (tq, 3 * tq), 0)
        kpos = (n - 1) * tq + lax.broadcasted_iota(jnp.int32, (tq, 3 * tq), 1)
        ok = (jnp.abs(qpos - kpos) <= WINDOW) & (kpos >= 0) & (kpos < seq)
        bias_ref[...] = jnp.where(ok, 0.0, NEG).astype(F32)
    else:
        k_all = k_ref[0].astype(BF16)
        v_all = v_ref[0].astype(BF16)

    for j in range(N_KV_HEADS):
        heads = [j * GROUP + g for g in range(GROUP)]
        qs = jnp.concatenate([q[:, h * HEAD_DIM:(h + 1) * HEAD_DIM] for h in heads], axis=0)
        kj = k_all[:, j * HEAD_DIM:(j + 1) * HEAD_DIM]
        vj = v_all[:, j * HEAD_DIM:(j + 1) * HEAD_DIM]
        s_ref[...] = lax.dot_general(qs, kj, (((1,), (1,)), ((), ())), preferred_element_type=F32)
        inv = []
        for rb in range(GROUP * tq // ATTN_ROWS):
            rows = slice(rb * ATTN_ROWS, (rb + 1) * ATTN_ROWS)
            sk = sink_ref[heads[rb * ATTN_ROWS // tq]]
            parts = [s_ref[rows, n_loc:]]
            if latent:
                q0 = rb * ATTN_ROWS % tq
                parts.insert(0, s_ref[rows, :n_loc] + bias_ref[q0:q0 + ATTN_ROWS, :])
            m = sk
            for s in parts:
                m = jnp.maximum(m, jnp.max(s, axis=-1, keepdims=True))
            den = jnp.exp(sk - m)
            c0 = 0
            for s in parts:
                p = jnp.exp(s - m)
                den = den + jnp.sum(p, axis=-1, keepdims=True)
                p_ref[rows, c0:c0 + s.shape[1]] = p.astype(BF16)
                c0 += s.shape[1]
            inv.append(1.0 / den)
        o = jnp.dot(p_ref[...], vj, preferred_element_type=F32) * jnp.concatenate(inv, axis=0)
        for g, h in enumerate(heads):
            o_ref[0, :, h * HEAD_DIM:(h + 1) * HEAD_DIM] = o[g * tq:(g + 1) * tq].astype(BF16)


def _attention(q, k, v, sink, k_ctx=None, v_ctx=None):
    bsz, seq, _ = q.shape
    latent = k_ctx is not None
    tq = WINDOW if latent else seq
    seq_spec = pl.BlockSpec((1, seq, D_KV), lambda b, i: (b, 0, 0))
    in_specs = [
        pl.BlockSpec(memory_space=pltpu.SMEM),
        pl.BlockSpec((1, tq, D_ATTN), lambda b, i: (b, i, 0)),
        seq_spec, seq_spec,
    ]
    args = [sink, q, k, v]
    n_keys = seq
    if latent:
        past = k_ctx.shape[1]
        ctx_spec = pl.BlockSpec((1, past, D_KV), lambda b, i: (b, 0, 0))
        in_specs += [ctx_spec, ctx_spec]
        args += [k_ctx, v_ctx]
        n_keys = 3 * tq + past
    scratch = [pltpu.VMEM((GROUP * tq, n_keys), F32), pltpu.VMEM((GROUP * tq, n_keys), BF16)]
    if latent:
        scratch.append(pltpu.VMEM((tq, 3 * tq), F32))
    return pl.pallas_call(
        functools.partial(_attn_kernel, latent=latent, seq=seq, tq=tq),
        out_shape=jax.ShapeDtypeStruct((bsz, seq, D_ATTN), BF16),
        grid=(bsz, seq // tq),
        in_specs=in_specs,
        out_specs=pl.BlockSpec((1, tq, D_ATTN), lambda b, i: (b, i, 0)),
        scratch_shapes=scratch,
        compiler_params=_cparams("parallel", "parallel"),
        name="attn_latent" if latent else "attn_context",
    )(*args)


LRU_HALO = 8
LRU_CHUNK = 128
LRU_HALF = D_LRU // 2


def _softplus(x):
    return jnp.maximum(x, 0.0) + jnp.log(1.0 + jnp.exp(-jnp.abs(x)))


def _gelu_tanh(x):
    return 0.5 * x * (1.0 + jnp.tanh(0.7978845608028654 * (x + 0.044715 * (x * x * x))))


def _scan_tile(a, b, carry, reverse):
    row = lax.broadcasted_iota(jnp.int32, a.shape, 0)
    for d in (1, 2, 4):
        if reverse:
            valid = row < SUB - d
            shift = SUB - d
        else:
            valid = row >= d
            shift = d
        a_prev = jnp.where(valid, pltpu.roll(a, shift, 0), 1.0)
        b_prev = jnp.where(valid, pltpu.roll(b, shift, 0), 0.0)
        b = a * b_prev + b
        a = a * a_prev
    h = a * carry + b
    last = h[0:1, :] if reverse else h[SUB - 1:SUB, :]
    return h, last


def _lru_kernel(lx_ref, lg_ref, h0_ref, cw_ref, cb_ref, wbd_ref, gbias_ref, lam_ref,
                o_ref, fin_ref, xp_ref, af_ref, bf_ref, ab_ref, bb_ref, *, seq):
    zeros = jnp.zeros((LRU_HALO, D_LRU), F32)
    xp_ref[0:LRU_HALO, :] = zeros
    xp_ref[LRU_HALO + seq:2 * LRU_HALO + seq, :] = zeros
    xp_ref[LRU_HALO:LRU_HALO + seq, :] = lx_ref[0]
    a_refs = (af_ref, ab_ref)
    b_refs = (bf_ref, bb_ref)
    sp = _softplus(-lam_ref[...])

    def gates(c, carry):
        r0 = pl.multiple_of(c * LRU_CHUNK, LRU_CHUNK)
        xc = jnp.broadcast_to(cb_ref[...], (LRU_CHUNK, D_LRU))
        win = xp_ref[pl.ds(r0, LRU_CHUNK + 2 * LRU_HALO), :]
        for k in range(LRU_CONV_K):
            off = LRU_HALO - 2 + k
            xc = xc + cw_ref[k:k + 1, :] * win[off:off + LRU_CHUNK, :]
        for s in range(2):
            cols = slice(s * LRU_HALF, (s + 1) * LRU_HALF)
            xs = xc[:, cols]
            g = jnp.dot(xs.astype(BF16), wbd_ref[s], preferred_element_type=F32)
            g = g + gbias_ref[s:s + 1, :]
            for d in range(2):
                base = d * 2 * LRU_HALF
                r = _sigmoid(g[:, base:base + LRU_HALF])
                i = _sigmoid(g[:, base + LRU_HALF:base + 2 * LRU_HALF])
                log_a = -LRU_C * r * sp[d:d + 1, cols]
                a = jnp.exp(log_a)
                t = jnp.tanh(log_a)
                b = jnp.sqrt(-2.0 * t / (1.0 - t)) * (i * xs)
                a_refs[d][pl.ds(r0, LRU_CHUNK), cols] = a
                b_refs[d][pl.ds(r0, LRU_CHUNK), cols] = b
        return carry

    lax.fori_loop(0, seq // LRU_CHUNK, gates, 0)

    ntile = seq // SUB

    def scan(t, carry):
        cf, cb = carry
        rf = pl.multiple_of(t * SUB, SUB)
        rb = pl.multiple_of((ntile - 1 - t) * SUB, SUB)
        hf, cf = _scan_tile(af_ref[pl.ds(rf, SUB), :], bf_ref[pl.ds(rf, SUB), :], cf, False)
        hb, cb = _scan_tile(ab_ref[pl.ds(rb, SUB), :], bb_ref[pl.ds(rb, SUB), :], cb, True)
        bf_ref[pl.ds(rf, SUB), :] = hf
        bb_ref[pl.ds(rb, SUB), :] = hb
        return cf, cb

    h0 = h0_ref[0]
    cf, cb = lax.fori_loop(0, ntile, scan, (h0[0:1, :], h0[1:2, :]))
    fin_ref[0, 0:1, :] = cf
    fin_ref[0, 1:2, :] = cb

    def finish(c, carry):
        r0 = pl.multiple_of(c * LRU_CHUNK, LRU_CHUNK)
        rows = pl.ds(r0, LRU_CHUNK)
        o_ref[0, rows, :] = ((bf_ref[rows, :] + bb_ref[rows, :]) * _gelu_tanh(lg_ref[0, rows, :])).astype(BF16)
        return carry

    lax.fori_loop(0, seq // LRU_CHUNK, finish, 0)


def _lru_gate_weights(wa, ba, wx, bx):
    per_half = LRU_HALF // LRU_BLK
    eye = jnp.eye(per_half, dtype=F32)

    def dense(w):
        blocks = w.reshape(2, per_half, LRU_BLK, LRU_BLK)
        return jnp.einsum("snkj,nm->snkmj", blocks, eye).reshape(2, LRU_HALF, LRU_HALF)

    wbd = jnp.concatenate([dense(wa[0]), dense(wx[0]), dense(wa[1]), dense(wx[1])], axis=2)
    gbias = jnp.stack([ba[0], bx[0], ba[1], bx[1]]).reshape(4, 2, LRU_HALF).transpose(1, 0, 2)
    return wbd.astype(BF16), gbias.reshape(2, 4 * LRU_HALF)


def _recurrent_mixer(lx, lg, h0, cw, cb, wbd, gbias, lam):
    bsz, seq, _ = lx.shape
    seq_spec = pl.BlockSpec((1, seq, D_LRU), lambda i: (i, 0, 0))
    state_spec = pl.BlockSpec((1, 2, D_LRU), lambda i: (i, 0, 0))
    return pl.pallas_call(
        functools.partial(_lru_kernel, seq=seq),
        out_shape=[jax.ShapeDtypeStruct((bsz, seq, D_LRU), BF16),
                   jax.ShapeDtypeStruct((bsz, 2, D_LRU), F32)],
        grid=(bsz,),
        in_specs=[
            seq_spec, seq_spec, state_spec,
            _const_spec((LRU_CONV_K, D_LRU)), _const_spec((1, D_LRU)),
            _const_spec((2, LRU_HALF, 4 * LRU_HALF)), _const_spec((2, 4 * LRU_HALF)),
            _const_spec((2, D_LRU)),
        ],
        out_specs=[seq_spec, state_spec],
        scratch_shapes=[pltpu.VMEM((seq + 2 * LRU_HALO, D_LRU), F32)] + [pltpu.VMEM((seq, D_LRU), F32)] * 4,
        compiler_params=_cparams("parallel"),
        name="rglru",
    )(lx, lg, h0, cw, cb.reshape(1, D_LRU), wbd, gbias, lam)


N_PAIRS = EXPERTS_PER_GROUP * (EXPERTS_PER_GROUP - 1) // 2
N_BUCKETS = N_EXPERT_GROUPS * N_PAIRS
PAIR_LO = (0, 0, 0, 1, 1, 2)
PAIR_HI = (1, 2, 3, 2, 3, 3)
ROUTE_ROWS = SUB


def _route(scores, biased):
    rows = [biased[e:e + 1, :] for e in range(N_EXPERTS)]
    group_score = []
    for g in range(N_EXPERT_GROUPS):
        a, b, c, d = rows[4 * g:4 * g + 4]
        hi1, lo1 = jnp.maximum(a, b), jnp.minimum(a, b)
        hi2, lo2 = jnp.maximum(c, d), jnp.minimum(c, d)
        top = jnp.maximum(hi1, hi2)
        second = jnp.maximum(jnp.minimum(hi1, hi2), jnp.maximum(lo1, lo2))
        group_score.append(top + second)
    best = group_score[0]
    g_sel = jnp.zeros_like(best, dtype=jnp.int32)
    for g in range(1, N_EXPERT_GROUPS):
        better = group_score[g] > best
        g_sel = jnp.where(better, g, g_sel)
        best = jnp.where(better, group_score[g], best)
    sel = []
    for e in range(N_EXPERTS):
        g = e // EXPERTS_PER_GROUP
        rank = jnp.zeros_like(g_sel)
        for o in range(g * EXPERTS_PER_GROUP, (g + 1) * EXPERTS_PER_GROUP):
            if o == e:
                continue
            ahead = (rows[o] >= rows[e]) if o < e else (rows[o] > rows[e])
            rank = rank + ahead.astype(jnp.int32)
        sel.append(jnp.where(jnp.where(g_sel == g, rank, 2) < 2, 1, 0))
    zero = jnp.zeros_like(best)
    lo_w, hi_w = zero, zero
    lo_idx = jnp.zeros_like(g_sel)
    hi_idx = jnp.zeros_like(g_sel)
    for g in range(N_EXPERT_GROUPS):
        seen = jnp.zeros_like(g_sel)
        for j in range(EXPERTS_PER_GROUP):
            e = g * EXPERTS_PER_GROUP + j
            order = jnp.where(sel[e] == 1, seen, 2)
            lo_w = jnp.where(order == 0, scores[e:e + 1, :], lo_w)
            hi_w = jnp.where(order == 1, scores[e:e + 1, :], hi_w)
            lo_idx = jnp.where(order == 0, j, lo_idx)
            hi_idx = jnp.where(order == 1, j, hi_idx)
            seen = seen + sel[e]
    pair_base = jnp.where(lo_idx == 0, 0, jnp.where(lo_idx == 1, 3, 5))
    bucket = g_sel * N_PAIRS + pair_base + hi_idx - lo_idx - 1
    total = lo_w + hi_w
    pad = jnp.zeros((ROUTE_ROWS - 3, best.shape[1]), F32)
    return jnp.concatenate([bucket.astype(F32), lo_w / total, hi_w / total, pad], axis=0)


def _modulated_norm(x, g, shift, scale):
    y = x * lax.rsqrt(jnp.mean(x * x, axis=-1, keepdims=True) + EPS) * g
    return y * (1.0 + scale) + shift


def _outproj_kernel(conv_ref, attn_ref, lru_ref, x_ref, mod_ref, g2_ref, w_ref, wr_ref, br_ref,
                    x1_ref, route_ref, cat_ref):
    cat_ref[:, 0:D_CONV] = conv_ref[0]
    cat_ref[:, D_CONV:D_CONV + D_ATTN] = attn_ref[0]
    cat_ref[:, D_CONV + D_ATTN:] = lru_ref[0]
    mix = jnp.dot(cat_ref[...], w_ref[...], preferred_element_type=F32)
    mod = mod_ref[0]
    x1 = x_ref[0] + mod[2:3] * mix
    x1_ref[0] = x1
    h2 = _modulated_norm(x1, g2_ref[...], mod[3:4], mod[4:5])
    h_hi, h_lo = _split_bf16(h2)
    nt = (((1,), (1,)), ((), ()))
    by_hi = lax.dot_general(wr_ref[...], h_hi, nt, preferred_element_type=F32)
    by_lo = lax.dot_general(wr_ref[0:N_EXPERTS, :], h_lo, nt, preferred_element_type=F32)
    logits = by_hi[0:N_EXPERTS] + by_hi[N_EXPERTS:] + by_lo
    scores = _sigmoid(logits)
    route_ref[0] = _route(scores, scores + br_ref[...])


def _outproj(conv, attn, lru, x, mods, row0, row_step, g2, w_out_bf, w_router_t, b_router, tm):
    bsz, seq, _ = x.shape
    tok = lambda width: pl.BlockSpec((1, tm, width), lambda b, i: (b, i, 0))
    return pl.pallas_call(
        _outproj_kernel,
        out_shape=[jax.ShapeDtypeStruct((bsz, seq, D_MODEL), F32),
                   jax.ShapeDtypeStruct((bsz, ROUTE_ROWS, seq), F32)],
        grid=(bsz, seq // tm),
        in_specs=[
            tok(D_CONV), tok(D_ATTN), tok(D_LRU), tok(D_MODEL),
            pl.BlockSpec((1, 6, D_MODEL), lambda b, i: (row0 + row_step * b, 0, 0)),
            _const_spec((1, D_MODEL)),
            _const_spec((D_MODEL, D_MODEL)),
            _const_spec((2 * N_EXPERTS, D_MODEL)),
            _const_spec((N_EXPERTS, 1)),
        ],
        out_specs=[tok(D_MODEL),
                   pl.BlockSpec((1, ROUTE_ROWS, tm), lambda b, i: (b, 0, i))],
        scratch_shapes=[pltpu.VMEM((tm, D_MODEL), BF16)],
        compiler_params=_cparams("parallel", "parallel"),
        name="outproj_router",
    )(conv, attn, lru, x, mods, g2.reshape(1, D_MODEL), w_out_bf, w_router_t, b_router.reshape(N_EXPERTS, 1))


MOE_TILE = 256
MOE_UNROLL = 8
MOE_PARTS = 2
META_COLS = 4


def _moe_plan(bucket, per_token, t_ctx, tm):
    t_all = bucket.shape[0]
    n_tiles = (t_all + N_BUCKETS * (tm - 1) + tm - 1) // tm
    ids = jnp.arange(N_BUCKETS, dtype=jnp.int32)
    order = jnp.argsort(bucket, stable=True).astype(jnp.int32)
    counts = jnp.sum((bucket[None, :] == ids[:, None]).astype(jnp.int32), axis=1)
    tiles_per = (counts + tm - 1) // tm
    tile_end = jnp.cumsum(tiles_per)
    tile_start = tile_end - tiles_per
    sorted_start = jnp.cumsum(counts) - counts
    n_used = tile_end[-1]
    j = jnp.arange(n_tiles, dtype=jnp.int32)
    jj = jnp.minimum(j, n_used - 1)
    b_of = jnp.sum((tile_end[None, :] <= jj[:, None]).astype(jnp.int32), axis=1)
    local = jj - tile_start[b_of]
    n_valid = jnp.where(j < n_used, jnp.clip(counts[b_of] - local * tm, 0, tm), 0)
    rows = (sorted_start[b_of] + local * tm)[:, None] + jnp.arange(tm, dtype=jnp.int32)[None, :]
    tok = order[jnp.clip(rows, 0, t_all - 1)]
    valid = jnp.arange(tm, dtype=jnp.int32)[None, :] < n_valid[:, None]
    n_ctx = jnp.sum((valid & (tok < t_ctx)).astype(jnp.int32), axis=1)
    group, pair = b_of // N_PAIRS, b_of % N_PAIRS
    e_lo = group * EXPERTS_PER_GROUP + jnp.asarray(PAIR_LO, jnp.int32)[pair]
    e_hi = group * EXPERTS_PER_GROUP + jnp.asarray(PAIR_HI, jnp.int32)[pair]
    i32 = lambda a: a.astype(jnp.int32)
    return i32(tok.reshape(-1)), i32(e_lo), i32(e_hi), i32(n_valid), i32(n_ctx), per_token[tok]


def _moe_kernel(tok_ref, elo_ref, ehi_ref, nv_ref, nc_ref,
                xc_ref, xl_ref, meta_ref, mods_ref, g2_ref,
                wg_lo, wu_lo, wd_lo, wg_hi, wu_hi, wd_hi,
                oc_ref, ol_ref, xbuf, ybuf, gsem, ssem, *, tm, t_ctx, n_mod_rows):
    i = pl.program_id(0)
    n = pl.num_programs(0)
    slot = lax.rem(i, 2)

    def gather(hbm, t, r, s, size):
        return pltpu.make_async_copy(hbm.at[pl.ds(t, size), :], xbuf.at[s, pl.ds(r, size), :], gsem.at[s])

    def scatter(hbm, t, r, s, size):
        return pltpu.make_async_copy(ybuf.at[s, pl.ds(r, size), :], hbm.at[pl.ds(t, size), :], ssem.at[s])

    def start_rows(j, s, copy, hbm_ctx, hbm_lat):
        base = j * tm

        def ctx_row(r):
            copy(hbm_ctx, tok_ref[base + r], r, s, 1).start()

        def lat_row(r):
            copy(hbm_lat, tok_ref[base + r] - t_ctx, r, s, 1).start()

        def rows(lo, hi, one_row):
            groups = lax.shift_right_logical(hi - lo, MOE_UNROLL.bit_length() - 1)

            def group(g, c):
                for u in range(MOE_UNROLL):
                    one_row(lo + g * MOE_UNROLL + u)
                return c

            def single(r, c):
                one_row(r)
                return c

            lax.fori_loop(0, groups, group, 0)
            lax.fori_loop(lo + groups * MOE_UNROLL, hi, single, 0)

        rows(0, nc_ref[j], ctx_row)
        rows(nc_ref[j], nv_ref[j], lat_row)

    def wait_rows(j, s, copy, hbm):
        cnt = nv_ref[j]
        bulk = pl.multiple_of(lax.shift_left(lax.shift_right_logical(cnt, 3), 3), SUB)

        @pl.when(bulk > 0)
        def _():
            copy(hbm, 0, 0, s, bulk).wait()

        def one(r, c):
            copy(hbm, 0, 0, s, 1).wait()
            return c

        lax.fori_loop(bulk, cnt, one, 0)

    @pl.when(i == 0)
    def _():
        xbuf[...] = jnp.zeros_like(xbuf)
        start_rows(0, 0, gather, xc_ref, xl_ref)

    @pl.when(i + 1 < n)
    def _():
        start_rows(i + 1, 1 - slot, gather, xc_ref, xl_ref)

    wait_rows(i, slot, gather, xl_ref)

    @pl.when(i >= 2)
    def _():
        wait_rows(i - 2, slot, scatter, ol_ref)

    def tile_part(rows):
        x = xbuf[slot, rows, :]
        meta = meta_ref[0, rows, :]
        w_lo, w_hi, mrow = meta[:, 0:1], meta[:, 1:2], meta[:, 2:3]
        n_rows = x.shape[0]

        def per_row(*ks):
            ids = jnp.broadcast_to(mrow, (n_rows, 128))
            masks = [ids == float(r) for r in range(1, n_mod_rows)]
            cols = [[] for _ in ks]
            for c in range(D_MODEL // 128):
                lanes = slice(c * 128, (c + 1) * 128)
                for q, k in enumerate(ks):
                    v = jnp.broadcast_to(mods_ref[0, k:k + 1, lanes], (n_rows, 128))
                    for r, mask in enumerate(masks, 1):
                        v = jnp.where(mask, mods_ref[r, k:k + 1, lanes], v)
                    cols[q].append(v)
            return [jnp.concatenate(col, axis=1) for col in cols]

        shift, scale = per_row(3, 4)
        h = _modulated_norm(x, g2_ref[...], shift, scale).astype(BF16)

        def expert(wg, wu, w):
            act = _silu(jnp.dot(h, wg[0], preferred_element_type=F32))
            act = act * jnp.dot(h, wu[0], preferred_element_type=F32) * w
            return act.astype(BF16)

        y = jnp.dot(expert(wg_lo, wu_lo, w_lo), wd_lo[0], preferred_element_type=F32)
        y = y + jnp.dot(expert(wg_hi, wu_hi, w_hi), wd_hi[0], preferred_element_type=F32)
        (gate,) = per_row(5)
        ybuf[slot, rows, :] = x + gate * y

    @pl.when(nv_ref[i] > 0)
    def _():
        part = tm // MOE_PARTS
        for p in range(MOE_PARTS):
            tile_part(slice(p * part, (p + 1) * part))

    start_rows(i, slot, scatter, oc_ref, ol_ref)

    @pl.when(i == n - 1)
    def _():
        wait_rows(i, slot, scatter, ol_ref)

        @pl.when(i >= 1)
        def _():
            wait_rows(i - 1, 1 - slot, scatter, ol_ref)


def _moe(x1c, x1l, route_c, route_l, mods, g2, wg, wu, wd, dec_seq):
    t_ctx, t_lat = x1c.shape[0], x1l.shape[0]
    tm = MOE_TILE
    route = jnp.concatenate([jnp.swapaxes(route_c, 0, 1).reshape(ROUTE_ROWS, t_ctx),
                             jnp.swapaxes(route_l, 0, 1).reshape(ROUTE_ROWS, t_lat)], axis=1)
    mod_row = np.concatenate([np.zeros(t_ctx), 1 + np.arange(t_lat) // dec_seq]).astype(np.float32)
    n_mod_rows = 1 + t_lat // dec_seq
    per_token = jnp.stack([route[1], route[2], jnp.asarray(mod_row), jnp.zeros((t_ctx + t_lat,), F32)], axis=-1)
    tok, e_lo, e_hi, n_valid, n_ctx, meta = _moe_plan(route[0].astype(jnp.int32), per_token, t_ctx, tm)
    n_tiles = n_valid.shape[0]
    w_up = lambda sel: pl.BlockSpec((1, D_MODEL, D_EXPERT),
                                    lambda i, tok, lo, hi, nv, nc: ((lo, hi)[sel][i], 0, 0))
    w_dn = lambda sel: pl.BlockSpec((1, D_EXPERT, D_MODEL),
                                    lambda i, tok, lo, hi, nv, nc: ((lo, hi)[sel][i], 0, 0))
    hbm = pl.BlockSpec(memory_space=pl.ANY)
    return pl.pallas_call(
        functools.partial(_moe_kernel, tm=tm, t_ctx=t_ctx, n_mod_rows=n_mod_rows),
        out_shape=[jax.ShapeDtypeStruct((t_ctx, D_MODEL), F32), jax.ShapeDtypeStruct((t_lat, D_MODEL), F32)],
        grid_spec=pltpu.PrefetchScalarGridSpec(
            num_scalar_prefetch=5,
            grid=(n_tiles,),
            in_specs=[
                hbm, hbm,
                pl.BlockSpec((1, tm, META_COLS), lambda i, *_: (i, 0, 0)),
                _const_spec((MOD_ROWS, 6, D_MODEL)),
                _const_spec((1, D_MODEL)),
                w_up(0), w_up(0), w_dn(0), w_up(1), w_up(1), w_dn(1),
            ],
            out_specs=[hbm, hbm],
            scratch_shapes=[
                pltpu.VMEM((2, tm, D_MODEL), F32), pltpu.VMEM((2, tm, D_MODEL), F32),
                pltpu.SemaphoreType.DMA((2,)), pltpu.SemaphoreType.DMA((2,)),
            ],
        ),
        compiler_params=_cparams("arbitrary"),
        name="moe_pairs",
    )(tok, e_lo, e_hi, n_valid, n_ctx, x1c, x1l, meta, mods, g2.reshape(1, D_MODEL), wg, wu, wd, wg, wu, wd)


INPROJ_TILE = 256


def _mixers(x, p, mods, row0, row_step, h0, layer, cast_weights, k_ctx=None, v_ctx=None):
    latent = k_ctx is not None
    bsz, seq, _ = x.shape
    (z, q, k, v, lg, lx), casts = _inproj(x, mods, row0, row_step, p["norm1_g"], p["w_in"], p["q_norm_g"],
                                          p["k_norm_g"], latent, INPROJ_TILE, layer, cast_weights)
    conv = _conv_module(z, p["conv_dw"], p["conv_b"], p["conv_ln_g"], p["conv_ln_b"])
    attn = _attention(q, k, v, p["attn_sink"], k_ctx, v_ctx)
    lru, fin = _recurrent_mixer(lx, lg, h0, p["lru_conv_w"], p["lru_conv_b"], p["lru_wbd"], p["lru_gbias"],
                                p["lru_lam"])
    x1, route = _outproj(conv, attn, lru, x, mods, row0, row_step, p["norm2_g"], p["w_out"],
                         p["w_router_t"], p["b_router"], min(seq, 512))
    return x1, route, k, v, fin, casts


def kernel(x_prompt, x_sample, c, cache_k, cache_v, state_lru, c_ctx, w_mod, b_mod, norm1_g, norm2_g, w_in,
           conv_dw, conv_b, conv_ln_g, conv_ln_b, q_norm_g, k_norm_g, attn_sink, lru_conv_w, lru_conv_b,
           lru_wa, lru_ba, lru_wx, lru_bx, lru_lam, w_out, w_router, b_router, w_gate_e, w_up_e, w_down_e):
    bsz, seq, _ = x_prompt.shape
    dec_bsz, dec_seq, _ = x_sample.shape
    past = cache_k.shape[2]

    cvec = jnp.zeros((MOD_ROWS, D_MODEL), F32).at[0].set(c_ctx).at[1:1 + dec_bsz].set(c)
    mods_all = _modulation(cvec, w_mod, b_mod).reshape(DEPTH, MOD_ROWS, 6, D_MODEL)

    wr_hi = w_router.T.astype(BF16)
    wr_lo = (w_router.T - wr_hi.astype(F32)).astype(BF16)
    w_router_t = jnp.concatenate([wr_hi, wr_lo], axis=0)
    layers = []
    for l in range(DEPTH):
        wbd, gbias = _lru_gate_weights(lru_wa[l], lru_ba[l], lru_wx[l], lru_bx[l])
        layers.append({
            "norm1_g": norm1_g[l], "norm2_g": norm2_g[l], "w_in": w_in[l].astype(BF16),
            "conv_dw": conv_dw[l], "conv_b": conv_b[l], "conv_ln_g": conv_ln_g[l], "conv_ln_b": conv_ln_b[l],
            "q_norm_g": q_norm_g[l], "k_norm_g": k_norm_g[l], "attn_sink": attn_sink[l],
            "lru_conv_w": lru_conv_w[l], "lru_conv_b": lru_conv_b[l], "lru_wbd": wbd, "lru_gbias": gbias,
            "lru_lam": lru_lam[l],
            "w_out": w_out[l].astype(BF16), "w_router_t": w_router_t, "b_router": b_router,
        })

    y, z = x_prompt, x_sample
    ks, vs, hs = [], [], []
    h0_ctx = jnp.zeros((bsz, 2, D_LRU), F32)
    for l in range(DEPTH):
        p, mods = layers[l], mods_all[l]
        y1, route_c, k_l, v_l, h_l, (wg_bf,) = _mixers(y, p, mods, 0, 0, h0_ctx, l, [w_gate_e])
        ks.append(k_l.reshape(bsz, seq, N_KV_HEADS, HEAD_DIM))
        vs.append(v_l.reshape(bsz, seq, N_KV_HEADS, HEAD_DIM))
        hs.append(h_l)
        z1, route_l, _, _, _, (wu_bf, wd_bf) = _mixers(z, p, mods, 1, 1, state_lru[:, l], l, [w_up_e, w_down_e],
                                                       cache_k[:, l].reshape(dec_bsz, past, D_KV),
                                                       cache_v[:, l].reshape(dec_bsz, past, D_KV))
        y, z = _moe(y1.reshape(bsz * seq, D_MODEL), z1.reshape(dec_bsz * dec_seq, D_MODEL), route_c, route_l,
                    mods, p["norm2_g"], wg_bf, wu_bf, wd_bf, dec_seq)
        y, z = y.reshape(bsz, seq, D_MODEL), z.reshape(dec_bsz, dec_seq, D_MODEL)
    new_cache_k = jnp.stack(ks, axis=1)
    new_cache_v = jnp.stack(vs, axis=1)
    new_state_lru = jnp.stack(hs, axis=1)
    return y, z, new_cache_k, new_cache_v, new_state_lru
```

```python
import functools

import numpy as np
import jax
import jax.numpy as jnp
from jax import lax
from jax.experimental import pallas as pl
from jax.experimental.pallas import tpu as pltpu

D_MODEL = 2048
DEPTH = 2
GRID_W = 64
D_CONV = 512
N_HEADS = 8
N_KV_HEADS = 2
HEAD_DIM = 128
GROUP = N_HEADS // N_KV_HEADS
D_ATTN = N_HEADS * HEAD_DIM
D_KV = N_KV_HEADS * HEAD_DIM
D_LRU = 512
CONV_K = 31
LRU_CONV_K = 4
LRU_BLOCKS = 8
LRU_BLK = D_LRU // LRU_BLOCKS
LRU_C = 8.0
WINDOW = 128
ROPE_BASE = 10000.0
ATTN_SCALE = HEAD_DIM ** -0.5
N_EXPERTS = 16
N_EXPERT_GROUPS = 4
EXPERTS_PER_GROUP = N_EXPERTS // N_EXPERT_GROUPS
D_EXPERT = 512
EPS = 1e-6
NEG = -1e30
D_IN = 2 * D_CONV + D_ATTN + 2 * D_KV + 2 * D_LRU
N_MOD = 6 * D_MODEL
SUB = 8
MOD_ROWS = SUB

V7X_VMEM_BYTES = 64 * 1024 * 1024
VMEM_LIMIT = V7X_VMEM_BYTES - 8 * 1024 * 1024

F32 = jnp.float32
BF16 = jnp.bfloat16
HIGHEST = lax.Precision.HIGHEST


def _cparams(*sem):
    return pltpu.CompilerParams(dimension_semantics=sem, vmem_limit_bytes=VMEM_LIMIT)


def _sigmoid(x):
    return 0.5 * jnp.tanh(0.5 * x) + 0.5


def _silu(x):
    return x * _sigmoid(x)


def _const_spec(shape):
    return pl.BlockSpec(shape, lambda *_: (0,) * len(shape), pipeline_mode=pl.Buffered(1))


def _split_bf16(x):
    hi = x.astype(BF16)
    return hi, (x - hi.astype(F32)).astype(BF16)


def _mod_kernel(c_ref, w_ref, b_ref, o_ref):
    s = _silu(c_ref[...])
    s_hi = s.astype(BF16).astype(F32)
    s_both = jnp.concatenate([s_hi, s - s_hi], axis=0).astype(BF16)
    w_hi, w_lo = _split_bf16(w_ref[0])
    by_hi = jnp.dot(s_both, w_hi, preferred_element_type=F32)
    by_lo = jnp.dot(s_both, w_lo, preferred_element_type=F32)
    o_ref[0] = by_hi[0:MOD_ROWS] + by_hi[MOD_ROWS:] + by_lo[0:MOD_ROWS] + b_ref[0]


def _modulation(cvec, w_mod, b_mod):
    tn = 1536
    return pl.pallas_call(
        _mod_kernel,
        out_shape=jax.ShapeDtypeStruct((DEPTH, MOD_ROWS, N_MOD), F32),
        grid=(DEPTH, N_MOD // tn),
        in_specs=[
            _const_spec((MOD_ROWS, D_MODEL)),
            pl.BlockSpec((1, D_MODEL, tn), lambda l, j: (l, 0, j)),
            pl.BlockSpec((1, 1, tn), lambda l, j: (l, 0, j)),
        ],
        out_specs=pl.BlockSpec((1, MOD_ROWS, tn), lambda l, j: (l, 0, j)),
        compiler_params=_cparams("parallel", "parallel"),
        name="modulation",
    )(cvec, w_mod, b_mod.reshape(DEPTH, 1, N_MOD))


def _head_norm(x, g):
    return x * lax.rsqrt(jnp.mean(x * x, axis=-1, keepdims=True) + EPS) * g


def _rope(x, cos, sin_signed):
    lane = lax.broadcasted_iota(jnp.int32, x.shape, 1)
    partner = jnp.where((lane % 64) < 32, pltpu.roll(x, 96, 1), pltpu.roll(x, 32, 1))
    return x * cos + partner * sin_signed


def _inproj_kernel(*refs, rope, n_cast):
    refs = list(refs)
    cast_dst = [refs.pop() for _ in range(n_cast)][::-1]
    z_ref, q_ref, k_ref, v_ref, lg_ref, lx_ref = refs[-6:]
    del refs[-6:]
    cast_src = [refs.pop() for _ in range(n_cast)][::-1]
    if rope:
        x_ref, mod_ref, g1_ref, w_ref, qg_ref, kg_ref, cos_ref, sin_ref = refs
    else:
        x_ref, mod_ref, g1_ref, w_ref, qg_ref, kg_ref = refs
    for src, dst in zip(cast_src, cast_dst):
        dst[0] = src[0, 0].astype(BF16)
    x = x_ref[0]
    mod = mod_ref[0]
    shift, scale = mod[0:1], mod[1:2]
    y = x * lax.rsqrt(jnp.mean(x * x, axis=-1, keepdims=True) + EPS) * g1_ref[...]
    h = (y * (1.0 + scale) + shift).astype(BF16)

    def proj(c0, c1):
        return jnp.dot(h, w_ref[:, c0:c1], preferred_element_type=F32)

    glu = proj(0, 2 * D_CONV)
    z_ref[0] = glu[:, :D_CONV] * _sigmoid(glu[:, D_CONV:])

    c0 = 2 * D_CONV
    q = proj(c0, c0 + D_ATTN)
    qg = qg_ref[...]
    for hd in range(N_HEADS):
        qh = _head_norm(q[:, hd * HEAD_DIM:(hd + 1) * HEAD_DIM], qg)
        if rope:
            qh = _rope(qh, cos_ref[...], sin_ref[...])
        q_ref[0, :, hd * HEAD_DIM:(hd + 1) * HEAD_DIM] = (qh * ATTN_SCALE).astype(BF16)

    c0 += D_ATTN
    kv = proj(c0, c0 + 2 * D_KV)
    kg = kg_ref[...]
    for hd in range(N_KV_HEADS):
        kh = _head_norm(kv[:, hd * HEAD_DIM:(hd + 1) * HEAD_DIM], kg)
        if rope:
            kh = _rope(kh, cos_ref[...], sin_ref[...])
        k_ref[0, :, hd * HEAD_DIM:(hd + 1) * HEAD_DIM] = kh.astype(k_ref.dtype)
    v_ref[0] = kv[:, D_KV:].astype(v_ref.dtype)

    c0 += 2 * D_KV
    l2 = proj(c0, c0 + 2 * D_LRU)
    lg_ref[0] = l2[:, :D_LRU]
    lx_ref[0] = l2[:, D_LRU:]


def _rope_tables(seq):
    n_freq = HEAD_DIM // 4
    inv = (ROPE_BASE ** (-np.arange(n_freq, dtype=np.float32) / n_freq)).astype(np.float32)
    t = np.arange(seq)
    ang_r = (t // GRID_W).astype(np.float32)[:, None] * inv[None, :]
    ang_c = (t % GRID_W).astype(np.float32)[:, None] * inv[None, :]
    cos = np.concatenate([np.cos(ang_r)] * 2 + [np.cos(ang_c)] * 2, axis=-1)
    sin = np.concatenate([-np.sin(ang_r), np.sin(ang_r), -np.sin(ang_c), np.sin(ang_c)], axis=-1)
    return jnp.asarray(cos, F32), jnp.asarray(sin, F32)


def _cast_specs(w, layer, n_inner, n_steps):
    _, n_e, rows, cols = w.shape
    parts = n_steps // n_e
    assert parts * n_e == n_steps and rows % parts == 0
    piece = lambda b, i: divmod(b * n_inner + i, parts)
    src = pl.BlockSpec((1, 1, rows // parts, cols), lambda b, i: (layer, *piece(b, i), 0))
    dst = pl.BlockSpec((1, rows // parts, cols), lambda b, i: (*piece(b, i), 0))
    return src, dst, jax.ShapeDtypeStruct((n_e, rows, cols), BF16)


def _inproj(x, mods, row0, row_step, g1, w_in_bf, qg, kg, rope, tm, layer, cast_weights):
    bsz, seq, _ = x.shape
    tok = lambda width: pl.BlockSpec((1, tm, width), lambda b, i: (b, i, 0))
    in_specs = [
        tok(D_MODEL),
        pl.BlockSpec((1, 6, D_MODEL), lambda b, i: (row0 + row_step * b, 0, 0)),
        _const_spec((1, D_MODEL)),
        _const_spec((D_MODEL, D_IN)),
        _const_spec((1, HEAD_DIM)),
        _const_spec((1, HEAD_DIM)),
    ]
    args = [x, mods, g1.reshape(1, D_MODEL), w_in_bf, qg.reshape(1, HEAD_DIM), kg.reshape(1, HEAD_DIM)]
    if rope:
        cos, sin = _rope_tables(seq)
        in_specs += [pl.BlockSpec((tm, HEAD_DIM), lambda b, i: (i, 0))] * 2
        args += [cos, sin]
    widths = (D_CONV, D_ATTN, D_KV, D_KV, D_LRU, D_LRU)
    kv_dtype = BF16 if rope else F32
    dtypes = (F32, BF16, kv_dtype, kv_dtype, F32, F32)
    out_shape = [jax.ShapeDtypeStruct((bsz, seq, w), dt) for w, dt in zip(widths, dtypes)]
    out_specs = [tok(w) for w in widths]
    n_inner = seq // tm
    for w in cast_weights:
        src, dst, shape = _cast_specs(w, layer, n_inner, bsz * n_inner)
        in_specs.append(src)
        args.append(w)
        out_specs.append(dst)
        out_shape.append(shape)
    outs = pl.pallas_call(
        functools.partial(_inproj_kernel, rope=rope, n_cast=len(cast_weights)),
        out_shape=out_shape,
        grid=(bsz, n_inner),
        in_specs=in_specs,
        out_specs=out_specs,
        compiler_params=_cparams("parallel", "parallel"),
        name="inproj_rope" if rope else "inproj",
    )(*args)
    return outs[:6], outs[6:]


CONV_PAD = (CONV_K - 1) // 2
CONV_HALO = 16
CONV_CHUNK = 64
CONV_COLS = 256


def _conv_kernel(z_ref, w_ref, b_ref, g_ref, beta_ref, o_ref, zp_ref, win_ref, acc_ref, *, seq):
    zeros = jnp.zeros((CONV_HALO, D_CONV), F32)
    zp_ref[0:CONV_HALO, :] = zeros
    zp_ref[CONV_HALO + seq:2 * CONV_HALO + seq, :] = zeros
    zp_ref[CONV_HALO:CONV_HALO + seq, :] = z_ref[0]
    n_parts = D_CONV // CONV_COLS

    def body(c, carry):
        r0 = pl.multiple_of(c * CONV_CHUNK, CONV_CHUNK)
        for p in range(n_parts):
            win_ref[p] = zp_ref[pl.ds(r0, CONV_CHUNK + 2 * CONV_HALO), p * CONV_COLS:(p + 1) * CONV_COLS]

        def taps(p, inner):
            part = jnp.broadcast_to(b_ref[p], (CONV_CHUNK, CONV_COLS))
            for k in range(CONV_K):
                off = CONV_HALO - CONV_PAD + k
                part = part + jnp.tile(w_ref[p, k], (CONV_CHUNK // SUB, 1)) * win_ref[p, off:off + CONV_CHUNK, :]
            acc_ref[p] = part
            return inner

        lax.fori_loop(0, n_parts, taps, 0)
        acc = jnp.concatenate([acc_ref[p] for p in range(n_parts)], axis=1)
        mu = jnp.mean(acc, axis=-1, keepdims=True)
        xc = acc - mu
        var = jnp.mean(xc * xc, axis=-1, keepdims=True)
        y = xc * lax.rsqrt(var + EPS) * g_ref[...] + beta_ref[...]
        o_ref[0, pl.ds(r0, CONV_CHUNK), :] = _silu(y).astype(BF16)
        return carry

    lax.fori_loop(0, seq // CONV_CHUNK, body, 0)


def _conv_module(z, w, b, g, beta):
    bsz, seq, _ = z.shape
    row = lambda a: a.reshape(1, D_CONV)
    n_parts = D_CONV // CONV_COLS
    w_parts = jnp.broadcast_to(w.reshape(CONV_K, 1, n_parts, CONV_COLS), (CONV_K, SUB, n_parts, CONV_COLS))
    w_parts = w_parts.transpose(2, 0, 1, 3)
    return pl.pallas_call(
        functools.partial(_conv_kernel, seq=seq),
        out_shape=jax.ShapeDtypeStruct((bsz, seq, D_CONV), BF16),
        grid=(bsz,),
        in_specs=[
            pl.BlockSpec((1, seq, D_CONV), lambda i: (i, 0, 0)),
            _const_spec((n_parts, CONV_K, SUB, CONV_COLS)),
            _const_spec((n_parts, 1, CONV_COLS)), _const_spec((1, D_CONV)), _const_spec((1, D_CONV)),
        ],
        out_specs=pl.BlockSpec((1, seq, D_CONV), lambda i: (i, 0, 0)),
        scratch_shapes=[pltpu.VMEM((seq + 2 * CONV_HALO, D_CONV), F32),
                        pltpu.VMEM((n_parts, CONV_CHUNK + 2 * CONV_HALO, CONV_COLS), F32),
                        pltpu.VMEM((n_parts, CONV_CHUNK, CONV_COLS), F32)],
        compiler_params=_cparams("parallel"),
        name="conv_module",
    )(z, w_parts, b.reshape(n_parts, 1, CONV_COLS), row(g), row(beta))


ATTN_ROWS = 64


def _attn_kernel(*refs, latent, seq, tq):
    if latent:
        sink_ref, q_ref, k_ref, v_ref, kc_ref, vc_ref, o_ref, s_ref, p_ref, bias_ref = refs
    else:
        sink_ref, q_ref, k_ref, v_ref, o_ref, s_ref, p_ref = refs
    q = q_ref[0]
    n_loc = 3 * tq if latent else 0
    if latent:
        n = pl.program_id(1)
        nblk = seq // tq
        blocks = (jnp.maximum(n - 1, 0), n, jnp.minimum(n + 1, nblk - 1))

        def window(ref):
            parts = [ref[0, pl.ds(pl.multiple_of(i * tq, tq), tq), :] for i in blocks]
            return jnp.concatenate(parts, axis=0)

        k_all = jnp.concatenate([window(k_ref), kc_ref[0]], axis=0).astype(BF16)
        v_all = jnp.concatenate([window(v_ref), vc_ref[0]], axis=0).astype(BF16)
        qpos = n * tq + lax.broadcasted_iota(jnp.int32, (tq, 3 * tq), 0)
        kpos = (n - 1) * tq + lax.broadcasted_iota(jnp.int32, (tq, 3 * tq), 1)
        ok = (jnp.abs(qpos - kpos) <= WINDOW) & (kpos >= 0) & (kpos < seq)
        bias_ref[...] = jnp.where(ok, 0.0, NEG).astype(F32)
    else:
        k_all = k_ref[0].astype(BF16)
        v_all = v_ref[0].astype(BF16)

    for j in range(N_KV_HEADS):
        heads = [j * GROUP + g for g in range(GROUP)]
        qs = jnp.concatenate([q[:, h * HEAD_DIM:(h + 1) * HEAD_DIM] for h in heads], axis=0)
        kj = k_all[:, j * HEAD_DIM:(j + 1) * HEAD_DIM]
        vj = v_all[:, j * HEAD_DIM:(j + 1) * HEAD_DIM]
        s_ref[...] = lax.dot_general(qs, kj, (((1,), (1,)), ((), ())), preferred_element_type=F32)
        inv = []
        for rb in range(GROUP * tq // ATTN_ROWS):
            rows = slice(rb * ATTN_ROWS, (rb + 1) * ATTN_ROWS)
            sk = sink_ref[heads[rb * ATTN_ROWS // tq]]
            parts = [s_ref[rows, n_loc:]]
            if latent:
                q0 = rb * ATTN_ROWS % tq
                parts.insert(0, s_ref[rows, :n_loc] + bias_ref[q0:q0 + ATTN_ROWS, :])
            m = sk
            for s in parts:
                m = jnp.maximum(m, jnp.max(s, axis=-1, keepdims=True))
            den = jnp.exp(sk - m)
            c0 = 0
            for s in parts:
                p = jnp.exp(s - m)
                den = den + jnp.sum(p, axis=-1, keepdims=True)
                p_ref[rows, c0:c0 + s.shape[1]] = p.astype(BF16)
                c0 += s.shape[1]
            inv.append(1.0 / den)
        o = jnp.dot(p_ref[...], vj, preferred_element_type=F32) * jnp.concatenate(inv, axis=0)
        for g, h in enumerate(heads):
            o_ref[0, :, h * HEAD_DIM:(h + 1) * HEAD_DIM] = o[g * tq:(g + 1) * tq].astype(BF16)


def _attention(q, k, v, sink, k_ctx=None, v_ctx=None):
    bsz, seq, _ = q.shape
    latent = k_ctx is not None
    tq = WINDOW if latent else seq
    seq_spec = pl.BlockSpec((1, seq, D_KV), lambda b, i: (b, 0, 0))
    in_specs = [
        pl.BlockSpec(memory_space=pltpu.SMEM),
        pl.BlockSpec((1, tq, D_ATTN), lambda b, i: (b, i, 0)),
        seq_spec, seq_spec,
    ]
    args = [sink, q, k, v]
    n_keys = seq
    if latent:
        past = k_ctx.shape[1]
        ctx_spec = pl.BlockSpec((1, past, D_KV), lambda b, i: (b, 0, 0))
        in_specs += [ctx_spec, ctx_spec]
        args += [k_ctx, v_ctx]
        n_keys = 3 * tq + past
    scratch = [pltpu.VMEM((GROUP * tq, n_keys), F32), pltpu.VMEM((GROUP * tq, n_keys), BF16)]
    if latent:
        scratch.append(pltpu.VMEM((tq, 3 * tq), F32))
    return pl.pallas_call(
        functools.partial(_attn_kernel, latent=latent, seq=seq, tq=tq),
        out_shape=jax.ShapeDtypeStruct((bsz, seq, D_ATTN), BF16),
        grid=(bsz, seq // tq),
        in_specs=in_specs,
        out_specs=pl.BlockSpec((1, tq, D_ATTN), lambda b, i: (b, i, 0)),
        scratch_shapes=scratch,
        compiler_params=_cparams("parallel", "parallel"),
        name="attn_latent" if latent else "attn_context",
    )(*args)


LRU_HALO = 8
LRU_CHUNK = 128
LRU_HALF = D_LRU // 2


def _softplus(x):
    return jnp.maximum(x, 0.0) + jnp.log(1.0 + jnp.exp(-jnp.abs(x)))


def _gelu_tanh(x):
    return 0.5 * x * (1.0 + jnp.tanh(0.7978845608028654 * (x + 0.044715 * (x * x * x))))


def _scan_tile(a, b, carry, reverse):
    row = lax.broadcasted_iota(jnp.int32, a.shape, 0)
    for d in (1, 2, 4):
        if reverse:
            valid = row < SUB - d
            shift = SUB - d
        else:
            valid = row >= d
            shift = d
        a_prev = jnp.where(valid, pltpu.roll(a, shift, 0), 1.0)
        b_prev = jnp.where(valid, pltpu.roll(b, shift, 0), 0.0)
        b = a * b_prev + b
        a = a * a_prev
    h = a * carry + b
    last = h[0:1, :] if reverse else h[SUB - 1:SUB, :]
    return h, last


def _lru_kernel(lx_ref, lg_ref, h0_ref, cw_ref, cb_ref, wbd_ref, gbias_ref, lam_ref,
                o_ref, fin_ref, xp_ref, af_ref, bf_ref, ab_ref, bb_ref, *, seq):
    zeros = jnp.zeros((LRU_HALO, D_LRU), F32)
    xp_ref[0:LRU_HALO, :] = zeros
    xp_ref[LRU_HALO + seq:2 * LRU_HALO + seq, :] = zeros
    xp_ref[LRU_HALO:LRU_HALO + seq, :] = lx_ref[0]
    a_refs = (af_ref, ab_ref)
    b_refs = (bf_ref, bb_ref)
    sp = _softplus(-lam_ref[...])

    def gates(c, carry):
        r0 = pl.multiple_of(c * LRU_CHUNK, LRU_CHUNK)
        xc = jnp.broadcast_to(cb_ref[...], (LRU_CHUNK, D_LRU))
        win = xp_ref[pl.ds(r0, LRU_CHUNK + 2 * LRU_HALO), :]
        for k in range(LRU_CONV_K):
            off = LRU_HALO - 2 + k
            xc = xc + cw_ref[k:k + 1, :] * win[off:off + LRU_CHUNK, :]
        for s in range(2):
            cols = slice(s * LRU_HALF, (s + 1) * LRU_HALF)
            xs = xc[:, cols]
            g = jnp.dot(xs.astype(BF16), wbd_ref[s], preferred_element_type=F32)
            g = g + gbias_ref[s:s + 1, :]
            for d in range(2):
                base = d * 2 * LRU_HALF
                r = _sigmoid(g[:, base:base + LRU_HALF])
                i = _sigmoid(g[:, base + LRU_HALF:base + 2 * LRU_HALF])
                log_a = -LRU_C * r * sp[d:d + 1, cols]
                a = jnp.exp(log_a)
                t = jnp.tanh(log_a)
                b = jnp.sqrt(-2.0 * t / (1.0 - t)) * (i * xs)
                a_refs[d][pl.ds(r0, LRU_CHUNK), cols] = a
                b_refs[d][pl.ds(r0, LRU_CHUNK), cols] = b
        return carry

    lax.fori_loop(0, seq // LRU_CHUNK, gates, 0)

    ntile = seq // SUB

    def scan(t, carry):
        cf, cb = carry
        rf = pl.multiple_of(t * SUB, SUB)
        rb = pl.multiple_of((ntile - 1 - t) * SUB, SUB)
        hf, cf = _scan_tile(af_ref[pl.ds(rf, SUB), :], bf_ref[pl.ds(rf, SUB), :], cf, False)
        hb, cb = _scan_tile(ab_ref[pl.ds(rb, SUB), :], bb_ref[pl.ds(rb, SUB), :], cb, True)
        bf_ref[pl.ds(rf, SUB), :] = hf
        bb_ref[pl.ds(rb, SUB), :] = hb
        return cf, cb

    h0 = h0_ref[0]
    cf, cb = lax.fori_loop(0, ntile, scan, (h0[0:1, :], h0[1:2, :]))
    fin_ref[0, 0:1, :] = cf
    fin_ref[0, 1:2, :] = cb

    def finish(c, carry):
        r0 = pl.multiple_of(c * LRU_CHUNK, LRU_CHUNK)
        rows = pl.ds(r0, LRU_CHUNK)
        o_ref[0, rows, :] = ((bf_ref[rows, :] + bb_ref[rows, :]) * _gelu_tanh(lg_ref[0, rows, :])).astype(BF16)
        return carry

    lax.fori_loop(0, seq // LRU_CHUNK, finish, 0)


def _lru_gate_weights(wa, ba, wx, bx):
    per_half = LRU_HALF // LRU_BLK
    eye = jnp.eye(per_half, dtype=F32)

    def dense(w):
        blocks = w.reshape(2, per_half, LRU_BLK, LRU_BLK)
        return jnp.einsum("snkj,nm->snkmj", blocks, eye).reshape(2, LRU_HALF, LRU_HALF)

    wbd = jnp.concatenate([dense(wa[0]), dense(wx[0]), dense(wa[1]), dense(wx[1])], axis=2)
    gbias = jnp.stack([ba[0], bx[0], ba[1], bx[1]]).reshape(4, 2, LRU_HALF).transpose(1, 0, 2)
    return wbd.astype(BF16), gbias.reshape(2, 4 * LRU_HALF)


def _recurrent_mixer(lx, lg, h0, cw, cb, wbd, gbias, lam):
    bsz, seq, _ = lx.shape
    seq_spec = pl.BlockSpec((1, seq, D_LRU), lambda i: (i, 0, 0))
    state_spec = pl.BlockSpec((1, 2, D_LRU), lambda i: (i, 0, 0))
    return pl.pallas_call(
        functools.partial(_lru_kernel, seq=seq),
        out_shape=[jax.ShapeDtypeStruct((bsz, seq, D_LRU), BF16),
                   jax.ShapeDtypeStruct((bsz, 2, D_LRU), F32)],
        grid=(bsz,),
        in_specs=[
            seq_spec, seq_spec, state_spec,
            _const_spec((LRU_CONV_K, D_LRU)), _const_spec((1, D_LRU)),
            _const_spec((2, LRU_HALF, 4 * LRU_HALF)), _const_spec((2, 4 * LRU_HALF)),
            _const_spec((2, D_LRU)),
        ],
        out_specs=[seq_spec, state_spec],
        scratch_shapes=[pltpu.VMEM((seq + 2 * LRU_HALO, D_LRU), F32)] + [pltpu.VMEM((seq, D_LRU), F32)] * 4,
        compiler_params=_cparams("parallel"),
        name="rglru",
    )(lx, lg, h0, cw, cb.reshape(1, D_LRU), wbd, gbias, lam)


N_PAIRS = EXPERTS_PER_GROUP * (EXPERTS_PER_GROUP - 1) // 2
N_BUCKETS = N_EXPERT_GROUPS * N_PAIRS
PAIR_LO = (0, 0, 0, 1, 1, 2)
PAIR_HI = (1, 2, 3, 2, 3, 3)
ROUTE_ROWS = SUB


def _route(scores, biased):
    rows = [biased[e:e + 1, :] for e in range(N_EXPERTS)]
    group_score = []
    for g in range(N_EXPERT_GROUPS):
        a, b, c, d = rows[4 * g:4 * g + 4]
        hi1, lo1 = jnp.maximum(a, b), jnp.minimum(a, b)
        hi2, lo2 = jnp.maximum(c, d), jnp.minimum(c, d)
        top = jnp.maximum(hi1, hi2)
        second = jnp.maximum(jnp.minimum(hi1, hi2), jnp.maximum(lo1, lo2))
        group_score.append(top + second)
    best = group_score[0]
    g_sel = jnp.zeros_like(best, dtype=jnp.int32)
    for g in range(1, N_EXPERT_GROUPS):
        better = group_score[g] > best
        g_sel = jnp.where(better, g, g_sel)
        best = jnp.where(better, group_score[g], best)
    sel = []
    for e in range(N_EXPERTS):
        g = e // EXPERTS_PER_GROUP
        rank = jnp.zeros_like(g_sel)
        for o in range(g * EXPERTS_PER_GROUP, (g + 1) * EXPERTS_PER_GROUP):
            if o == e:
                continue
            ahead = (rows[o] >= rows[e]) if o < e else (rows[o] > rows[e])
            rank = rank + ahead.astype(jnp.int32)
        sel.append(jnp.where(jnp.where(g_sel == g, rank, 2) < 2, 1, 0))
    zero = jnp.zeros_like(best)
    lo_w, hi_w = zero, zero
    lo_idx = jnp.zeros_like(g_sel)
    hi_idx = jnp.zeros_like(g_sel)
    for g in range(N_EXPERT_GROUPS):
        seen = jnp.zeros_like(g_sel)
        for j in range(EXPERTS_PER_GROUP):
            e = g * EXPERTS_PER_GROUP + j
            order = jnp.where(sel[e] == 1, seen, 2)
            lo_w = jnp.where(order == 0, scores[e:e + 1, :], lo_w)
            hi_w = jnp.where(order == 1, scores[e:e + 1, :], hi_w)
            lo_idx = jnp.where(order == 0, j, lo_idx)
            hi_idx = jnp.where(order == 1, j, hi_idx)
            seen = seen + sel[e]
    pair_base = jnp.where(lo_idx == 0, 0, jnp.where(lo_idx == 1, 3, 5))
    bucket = g_sel * N_PAIRS + pair_base + hi_idx - lo_idx - 1
    total = lo_w + hi_w
    pad = jnp.zeros((ROUTE_ROWS - 3, best.shape[1]), F32)
    return jnp.concatenate([bucket.astype(F32), lo_w / total, hi_w / total, pad], axis=0)


def _modulated_norm(x, g, shift, scale):
    y = x * lax.rsqrt(jnp.mean(x * x, axis=-1, keepdims=True) + EPS) * g
    return y * (1.0 + scale) + shift


META_LANES = 128
META_W_LO, META_W_HI, META_MOD_ROW = 1, 2, 3


def _outproj_kernel(conv_ref, attn_ref, lru_ref, x_ref, mod_ref, g2_ref, w_ref, wr_ref, br_ref,
                    x1_ref, route_ref, cat_ref, *, row0, row_step):
    cat_ref[:, 0:D_CONV] = conv_ref[0]
    cat_ref[:, D_CONV:D_CONV + D_ATTN] = attn_ref[0]
    cat_ref[:, D_CONV + D_ATTN:] = lru_ref[0]
    mix = jnp.dot(cat_ref[...], w_ref[...], preferred_element_type=F32)
    mod = mod_ref[0]
    x1 = x_ref[0] + mod[2:3] * mix
    x1_ref[0, :, 0:D_MODEL] = x1
    h2 = _modulated_norm(x1, g2_ref[...], mod[3:4], mod[4:5])
    h_hi, h_lo = _split_bf16(h2)
    nt = (((1,), (1,)), ((), ()))
    by_hi = lax.dot_general(wr_ref[...], h_hi, nt, preferred_element_type=F32)
    by_lo = lax.dot_general(wr_ref[0:N_EXPERTS, :], h_lo, nt, preferred_element_type=F32)
    logits = by_hi[0:N_EXPERTS] + by_hi[N_EXPERTS:] + by_lo
    scores = _sigmoid(logits)
    route = _route(scores, scores + br_ref[...])
    route_ref[0] = route
    tm = route.shape[1]
    mod_row = jnp.full((1, tm), row0, jnp.int32) + row_step * pl.program_id(0)
    rows = jnp.concatenate([route[0:META_MOD_ROW], mod_row.astype(F32),
                            jnp.zeros((META_LANES - META_MOD_ROW - 1, tm), F32)], axis=0)
    x1_ref[0, :, D_MODEL:] = rows.T


def _outproj(conv, attn, lru, x, mods, row0, row_step, g2, w_out_bf, w_router_t, b_router, tm):
    bsz, seq, _ = x.shape
    tok = lambda width: pl.BlockSpec((1, tm, width), lambda b, i: (b, i, 0))
    return pl.pallas_call(
        functools.partial(_outproj_kernel, row0=row0, row_step=row_step),
        out_shape=[jax.ShapeDtypeStruct((bsz, seq, D_MODEL + META_LANES), F32),
                   jax.ShapeDtypeStruct((bsz, ROUTE_ROWS, seq), F32)],
        grid=(bsz, seq // tm),
        in_specs=[
            tok(D_CONV), tok(D_ATTN), tok(D_LRU), tok(D_MODEL),
            pl.BlockSpec((1, 6, D_MODEL), lambda b, i: (row0 + row_step * b, 0, 0)),
            _const_spec((1, D_MODEL)),
            _const_spec((D_MODEL, D_MODEL)),
            _const_spec((2 * N_EXPERTS, D_MODEL)),
            _const_spec((N_EXPERTS, 1)),
        ],
        out_specs=[tok(D_MODEL + META_LANES),
                   pl.BlockSpec((1, ROUTE_ROWS, tm), lambda b, i: (b, 0, i))],
        scratch_shapes=[pltpu.VMEM((tm, D_MODEL), BF16)],
        compiler_params=_cparams("parallel", "parallel"),
        name="outproj_router",
    )(conv, attn, lru, x, mods, g2.reshape(1, D_MODEL), w_out_bf, w_router_t, b_router.reshape(N_EXPERTS, 1))


MOE_TILE = 256
MOE_UNROLL = 8
MOE_PARTS = 2


def _moe_plan(bucket, t_ctx, tm):
    t_all = bucket.shape[0]
    n_tiles = (t_all + N_BUCKETS * (tm - 1) + tm - 1) // tm
    ids = jnp.arange(N_BUCKETS, dtype=jnp.int32)
    order = jnp.argsort(bucket, stable=True).astype(jnp.int32)
    member = (bucket[None, :] == ids[:, None]).astype(jnp.int32)
    counts = jnp.sum(member, axis=1)
    counts_ctx = jnp.sum(member[:, :t_ctx], axis=1)
    tiles_per = (counts + tm - 1) // tm
    tile_end = jnp.cumsum(tiles_per)
    tile_start = tile_end - tiles_per
    sorted_start = jnp.cumsum(counts) - counts
    n_used = tile_end[-1]
    j = jnp.arange(n_tiles, dtype=jnp.int32)
    jj = jnp.minimum(j, n_used - 1)
    b_of = jnp.sum((tile_end[None, :] <= jj[:, None]).astype(jnp.int32), axis=1)
    local = jj - tile_start[b_of]
    n_valid = jnp.where(j < n_used, jnp.clip(counts[b_of] - local * tm, 0, tm), 0)
    n_ctx = jnp.clip(counts_ctx[b_of] - local * tm, 0, n_valid)
    first = jnp.where(j < n_used, sorted_start[b_of] + local * tm, 0)
    group, pair = b_of // N_PAIRS, b_of % N_PAIRS
    e_lo = group * EXPERTS_PER_GROUP + jnp.asarray(PAIR_LO, jnp.int32)[pair]
    e_hi = group * EXPERTS_PER_GROUP + jnp.asarray(PAIR_HI, jnp.int32)[pair]
    i32 = lambda a: a.astype(jnp.int32)
    return order, i32(first), i32(e_lo), i32(e_hi), i32(n_valid), i32(n_ctx)


def _moe_kernel(order_ref, first_ref, elo_ref, ehi_ref, nv_ref, nc_ref,
                xc_ref, xl_ref, mods_ref, g2_ref,
                wg_lo, wu_lo, wd_lo, wg_hi, wu_hi, wd_hi,
                oc_ref, ol_ref, xbuf, ybuf, gsem, ssem, *, tm, t_ctx, n_mod_rows):
    i = pl.program_id(0)
    n = pl.num_programs(0)
    slot = lax.rem(i, 2)

    def gather(hbm, t, r, s, size):
        return pltpu.make_async_copy(hbm.at[pl.ds(t, size), :], xbuf.at[s, pl.ds(r, size), :], gsem.at[s])

    def scatter(hbm, t, r, s, size):
        return pltpu.make_async_copy(ybuf.at[s, pl.ds(r, size), :], hbm.at[pl.ds(t, size), :], ssem.at[s])

    def start_rows(j, s, copy, hbm_ctx, hbm_lat):
        base = first_ref[j]

        def ctx_row(r):
            copy(hbm_ctx, order_ref[base + r], r, s, 1).start()

        def lat_row(r):
            copy(hbm_lat, order_ref[base + r] - t_ctx, r, s, 1).start()

        def rows(lo, hi, one_row):
            groups = lax.shift_right_logical(hi - lo, MOE_UNROLL.bit_length() - 1)

            def group(g, c):
                for u in range(MOE_UNROLL):
                    one_row(lo + g * MOE_UNROLL + u)
                return c

            def single(r, c):
                one_row(r)
                return c

            lax.fori_loop(0, groups, group, 0)
            lax.fori_loop(lo + groups * MOE_UNROLL, hi, single, 0)

        rows(0, nc_ref[j], ctx_row)
        rows(nc_ref[j], nv_ref[j], lat_row)

    def wait_rows(j, s, copy, hbm):
        cnt = nv_ref[j]
        bulk = pl.multiple_of(lax.shift_left(lax.shift_right_logical(cnt, 3), 3), SUB)

        @pl.when(bulk > 0)
        def _():
            copy(hbm, 0, 0, s, bulk).wait()

        def one(r, c):
            copy(hbm, 0, 0, s, 1).wait()
            return c

        lax.fori_loop(bulk, cnt, one, 0)

    @pl.when(i == 0)
    def _():
        xbuf[...] = jnp.zeros_like(xbuf)
        start_rows(0, 0, gather, xc_ref, xl_ref)

    @pl.when(i + 1 < n)
    def _():
        start_rows(i + 1, 1 - slot, gather, xc_ref, xl_ref)

    wait_rows(i, slot, gather, xl_ref)

    @pl.when(i >= 2)
    def _():
        wait_rows(i - 2, slot, scatter, ol_ref)

    def tile_part(rows):
        x = xbuf[slot, rows, 0:D_MODEL]
        meta = xbuf[slot, rows, D_MODEL:]
        w_lo = meta[:, META_W_LO:META_W_LO + 1]
        w_hi = meta[:, META_W_HI:META_W_HI + 1]
        mrow = meta[:, META_MOD_ROW:META_MOD_ROW + 1]
        n_rows = x.shape[0]

        def per_row(*ks):
            ids = jnp.broadcast_to(mrow, (n_rows, 128))
            masks = [ids == float(r) for r in range(1, n_mod_rows)]
            cols = [[] for _ in ks]
            for c in range(D_MODEL // 128):
                lanes = slice(c * 128, (c + 1) * 128)
                for q, k in enumerate(ks):
                    v = jnp.broadcast_to(mods_ref[0, k:k + 1, lanes], (n_rows, 128))
                    for r, mask in enumerate(masks, 1):
                        v = jnp.where(mask, mods_ref[r, k:k + 1, lanes], v)
                    cols[q].append(v)
            return [jnp.concatenate(col, axis=1) for col in cols]

        shift, scale = per_row(3, 4)
        h = _modulated_norm(x, g2_ref[...], shift, scale).astype(BF16)

        def expert(wg, wu, w):
            act = _silu(jnp.dot(h, wg[0], preferred_element_type=F32))
            act = act * jnp.dot(h, wu[0], preferred_element_type=F32) * w
            return act.astype(BF16)

        y = jnp.dot(expert(wg_lo, wu_lo, w_lo), wd_lo[0], preferred_element_type=F32)
        y = y + jnp.dot(expert(wg_hi, wu_hi, w_hi), wd_hi[0], preferred_element_type=F32)
        (gate,) = per_row(5)
        ybuf[slot, rows, :] = x + gate * y

    @pl.when(nv_ref[i] > 0)
    def _():
        part = tm // MOE_PARTS
        for p in range(MOE_PARTS):
            tile_part(slice(p * part, (p + 1) * part))

    start_rows(i, slot, scatter, oc_ref, ol_ref)

    @pl.when(i == n - 1)
    def _():
        wait_rows(i, slot, scatter, ol_ref)

        @pl.when(i >= 1)
        def _():
            wait_rows(i - 1, 1 - slot, scatter, ol_ref)


def _moe(x1c, x1l, route_c, route_l, mods, g2, wg, wu, wd, n_mod_rows):
    t_ctx, t_lat = x1c.shape[0], x1l.shape[0]
    tm = MOE_TILE
    bucket = jnp.concatenate([route_c[:, 0, :].reshape(t_ctx), route_l[:, 0, :].reshape(t_lat)])
    order, first, e_lo, e_hi, n_valid, n_ctx = _moe_plan(bucket.astype(jnp.int32), t_ctx, tm)
    n_tiles = n_valid.shape[0]
    w_up = lambda sel: pl.BlockSpec((1, D_MODEL, D_EXPERT),
                                    lambda i, order, first, lo, hi, nv, nc: ((lo, hi)[sel][i], 0, 0))
    w_dn = lambda sel: pl.BlockSpec((1, D_EXPERT, D_MODEL),
                                    lambda i, order, first, lo, hi, nv, nc: ((lo, hi)[sel][i], 0, 0))
    hbm = pl.BlockSpec(memory_space=pl.ANY)
    return pl.pallas_call(
        functools.partial(_moe_kernel, tm=tm, t_ctx=t_ctx, n_mod_rows=n_mod_rows),
        out_shape=[jax.ShapeDtypeStruct((t_ctx, D_MODEL), F32), jax.ShapeDtypeStruct((t_lat, D_MODEL), F32)],
        grid_spec=pltpu.PrefetchScalarGridSpec(
            num_scalar_prefetch=6,
            grid=(n_tiles,),
            in_specs=[
                hbm, hbm,
                _const_spec((MOD_ROWS, 6, D_MODEL)),
                _const_spec((1, D_MODEL)),
                w_up(0), w_up(0), w_dn(0), w_up(1), w_up(1), w_dn(1),
            ],
            out_specs=[hbm, hbm],
            scratch_shapes=[
                pltpu.VMEM((2, tm, D_MODEL + META_LANES), F32), pltpu.VMEM((2, tm, D_MODEL), F32),
                pltpu.SemaphoreType.DMA((2,)), pltpu.SemaphoreType.DMA((2,)),
            ],
        ),
        compiler_params=_cparams("arbitrary"),
        name="moe_pairs",
    )(order, first, e_lo, e_hi, n_valid, n_ctx, x1c, x1l, mods, g2.reshape(1, D_MODEL), wg, wu, wd, wg, wu, wd)


INPROJ_TILE = 256


def _mixers(x, p, mods, row0, row_step, h0, layer, cast_weights, k_ctx=None, v_ctx=None):
    latent = k_ctx is not None
    bsz, seq, _ = x.shape
    (z, q, k, v, lg, lx), casts = _inproj(x, mods, row0, row_step, p["norm1_g"], p["w_in"], p["q_norm_g"],
                                          p["k_norm_g"], latent, INPROJ_TILE, layer, cast_weights)
    conv = _conv_module(z, p["conv_dw"], p["conv_b"], p["conv_ln_g"], p["conv_ln_b"])
    attn = _attention(q, k, v, p["attn_sink"], k_ctx, v_ctx)
    lru, fin = _recurrent_mixer(lx, lg, h0, p["lru_conv_w"], p["lru_conv_b"], p["lru_wbd"], p["lru_gbias"],
                                p["lru_lam"])
    x1, route = _outproj(conv, attn, lru, x, mods, row0, row_step, p["norm2_g"], p["w_out"],
                         p["w_router_t"], p["b_router"], min(seq, 512))
    return x1, route, k, v, fin, casts


def kernel(x_prompt, x_sample, c, cache_k, cache_v, state_lru, c_ctx, w_mod, b_mod, norm1_g, norm2_g, w_in,
           conv_dw, conv_b, conv_ln_g, conv_ln_b, q_norm_g, k_norm_g, attn_sink, lru_conv_w, lru_conv_b,
           lru_wa, lru_ba, lru_wx, lru_bx, lru_lam, w_out, w_router, b_router, w_gate_e, w_up_e, w_down_e):
    bsz, seq, _ = x_prompt.shape
    dec_bsz, dec_seq, _ = x_sample.shape
    past = cache_k.shape[2]

    cvec = jnp.zeros((MOD_ROWS, D_MODEL), F32).at[0].set(c_ctx).at[1:1 + dec_bsz].set(c)
    mods_all = _modulation(cvec, w_mod, b_mod).reshape(DEPTH, MOD_ROWS, 6, D_MODEL)

    wr_hi = w_router.T.astype(BF16)
    wr_lo = (w_router.T - wr_hi.astype(F32)).astype(BF16)
    w_router_t = jnp.concatenate([wr_hi, wr_lo], axis=0)
    layers = []
    for l in range(DEPTH):
        wbd, gbias = _lru_gate_weights(lru_wa[l], lru_ba[l], lru_wx[l], lru_bx[l])
        layers.append({
            "norm1_g": norm1_g[l], "norm2_g": norm2_g[l], "w_in": w_in[l].astype(BF16),
            "conv_dw": conv_dw[l], "conv_b": conv_b[l], "conv_ln_g": conv_ln_g[l], "conv_ln_b": conv_ln_b[l],
            "q_norm_g": q_norm_g[l], "k_norm_g": k_norm_g[l], "attn_sink": attn_sink[l],
            "lru_conv_w": lru_conv_w[l], "lru_conv_b": lru_conv_b[l], "lru_wbd": wbd, "lru_gbias": gbias,
            "lru_lam": lru_lam[l],
            "w_out": w_out[l].astype(BF16), "w_router_t": w_router_t, "b_router": b_router,
        })

    y, z = x_prompt, x_sample
    ks, vs, hs = [], [], []
    h0_ctx = jnp.zeros((bsz, 2, D_LRU), F32)
    for l in range(DEPTH):
        p, mods = layers[l], mods_all[l]
        y1, route_c, k_l, v_l, h_l, (wg_bf,) = _mixers(y, p, mods, 0, 0, h0_ctx, l, [w_gate_e])
        ks.append(k_l.reshape(bsz, seq, N_KV_HEADS, HEAD_DIM))
        vs.append(v_l.reshape(bsz, seq, N_KV_HEADS, HEAD_DIM))
        hs.append(h_l)
        z1, route_l, _, _, _, (wu_bf, wd_bf) = _mixers(z, p, mods, 1, 1, state_lru[:, l], l, [w_up_e, w_down_e],
                                                       cache_k[:, l].reshape(dec_bsz, past, D_KV).astype(BF16),
                                                       cache_v[:, l].reshape(dec_bsz, past, D_KV).astype(BF16))
        y, z = _moe(y1.reshape(bsz * seq, -1), z1.reshape(dec_bsz * dec_seq, -1), route_c, route_l,
                    mods, p["norm2_g"], wg_bf, wu_bf, wd_bf, 1 + dec_bsz)
        y, z = y.reshape(bsz, seq, D_MODEL), z.reshape(dec_bsz, dec_seq, D_MODEL)
    new_cache_k = jnp.stack(ks, axis=1)
    new_cache_v = jnp.stack(vs, axis=1)
    new_state_lru = jnp.stack(hs, axis=1)
    return y, z, new_cache_k, new_cache_v, new_state_lru
```

```python
import functools

import numpy as np
import jax
import jax.numpy as jnp
from jax import lax
from jax.experimental import pallas as pl
from jax.experimental.pallas import tpu as pltpu

D_MODEL = 2048
DEPTH = 2
GRID_W = 64
D_CONV = 512
N_HEADS = 8
N_KV_HEADS = 2
HEAD_DIM = 128
GROUP = N_HEADS // N_KV_HEADS
D_ATTN = N_HEADS * HEAD_DIM
D_KV = N_KV_HEADS * HEAD_DIM
D_LRU = 512
CONV_K = 31
LRU_CONV_K = 4
LRU_BLOCKS = 8
LRU_BLK = D_LRU // LRU_BLOCKS
LRU_C = 8.0
WINDOW = 128
ROPE_BASE = 10000.0
ATTN_SCALE = HEAD_DIM ** -0.5
N_EXPERTS = 16
N_EXPERT_GROUPS = 4
EXPERTS_PER_GROUP = N_EXPERTS // N_EXPERT_GROUPS
D_EXPERT = 512
EPS = 1e-6
NEG = -1e30
D_IN = 2 * D_CONV + D_ATTN + 2 * D_KV + 2 * D_LRU
N_MOD = 6 * D_MODEL
SUB = 8
MOD_ROWS = SUB

V7X_VMEM_BYTES = 64 * 1024 * 1024
VMEM_LIMIT = V7X_VMEM_BYTES - 8 * 1024 * 1024

F32 = jnp.float32
BF16 = jnp.bfloat16
HIGHEST = lax.Precision.HIGHEST


def _cparams(*sem):
    return pltpu.CompilerParams(dimension_semantics=sem, vmem_limit_bytes=VMEM_LIMIT)


def _sigmoid(x):
    return 0.5 * jnp.tanh(0.5 * x) + 0.5


def _silu(x):
    return x * _sigmoid(x)


def _const_spec(shape):
    return pl.BlockSpec(shape, lambda *_: (0,) * len(shape), pipeline_mode=pl.Buffered(1))


def _layer_spec(shape, layer):
    return pl.BlockSpec((1,) + shape, lambda *_: (layer,) + (0,) * len(shape), pipeline_mode=pl.Buffered(1))


def _split_bf16(x):
    hi = x.astype(BF16)
    return hi, (x - hi.astype(F32)).astype(BF16)


def _mod_kernel(c_ref, w_ref, b_ref, o_ref):
    s = _silu(c_ref[...])
    s_hi = s.astype(BF16).astype(F32)
    s_both = jnp.concatenate([s_hi, s - s_hi], axis=0).astype(BF16)
    w_hi, w_lo = _split_bf16(w_ref[0])
    by_hi = jnp.dot(s_both, w_hi, preferred_element_type=F32)
    by_lo = jnp.dot(s_both, w_lo, preferred_element_type=F32)
    o_ref[0] = by_hi[0:MOD_ROWS] + by_hi[MOD_ROWS:] + by_lo[0:MOD_ROWS] + b_ref[0]


def _modulation(cvec, w_mod, b_mod):
    tn = 1536
    return pl.pallas_call(
        _mod_kernel,
        out_shape=jax.ShapeDtypeStruct((DEPTH, MOD_ROWS, N_MOD), F32),
        grid=(DEPTH, N_MOD // tn),
        in_specs=[
            _const_spec((MOD_ROWS, D_MODEL)),
            pl.BlockSpec((1, D_MODEL, tn), lambda l, j: (l, 0, j)),
            pl.BlockSpec((1, 1, tn), lambda l, j: (l, 0, j)),
        ],
        out_specs=pl.BlockSpec((1, MOD_ROWS, tn), lambda l, j: (l, 0, j)),
        compiler_params=_cparams("parallel", "parallel"),
        name="modulation",
    )(cvec, w_mod, b_mod.reshape(DEPTH, 1, N_MOD))


def _head_norm(x, g):
    return x * lax.rsqrt(jnp.mean(x * x, axis=-1, keepdims=True) + EPS) * g


def _rope(x, cos, sin_signed):
    lane = lax.broadcasted_iota(jnp.int32, x.shape, 1)
    partner = jnp.where((lane % 64) < 32, pltpu.roll(x, 96, 1), pltpu.roll(x, 32, 1))
    return x * cos + partner * sin_signed


def _inproj_kernel(*refs, rope, n_cast):
    refs = list(refs)
    cast_dst = [refs.pop() for _ in range(n_cast)][::-1]
    z_ref, q_ref, k_ref, v_ref, lg_ref, lx_ref = refs[-6:]
    del refs[-6:]
    cast_src = [refs.pop() for _ in range(n_cast)][::-1]
    if rope:
        x_ref, mod_ref, g1_ref, w_ref, qg_ref, kg_ref, cos_ref, sin_ref = refs
    else:
        x_ref, mod_ref, g1_ref, w_ref, qg_ref, kg_ref = refs
    for src, dst in zip(cast_src, cast_dst):
        dst[0] = src[0, 0].astype(BF16)
    x = x_ref[0]
    mod = mod_ref[0]
    shift, scale = mod[0:1], mod[1:2]
    y = x * lax.rsqrt(jnp.mean(x * x, axis=-1, keepdims=True) + EPS) * g1_ref[...]
    h = (y * (1.0 + scale) + shift).astype(BF16)

    def proj(c0, c1):
        return jnp.dot(h, w_ref[0, :, c0:c1], preferred_element_type=F32)

    glu = proj(0, 2 * D_CONV)
    z_ref[0] = glu[:, :D_CONV] * _sigmoid(glu[:, D_CONV:])

    c0 = 2 * D_CONV
    q = proj(c0, c0 + D_ATTN)
    qg = qg_ref[...]
    for hd in range(N_HEADS):
        qh = _head_norm(q[:, hd * HEAD_DIM:(hd + 1) * HEAD_DIM], qg)
        if rope:
            qh = _rope(qh, cos_ref[...], sin_ref[...])
        q_ref[0, :, hd * HEAD_DIM:(hd + 1) * HEAD_DIM] = (qh * ATTN_SCALE).astype(BF16)

    c0 += D_ATTN
    kv = proj(c0, c0 + 2 * D_KV)
    kg = kg_ref[...]
    for hd in range(N_KV_HEADS):
        kh = _head_norm(kv[:, hd * HEAD_DIM:(hd + 1) * HEAD_DIM], kg)
        if rope:
            kh = _rope(kh, cos_ref[...], sin_ref[...])
        k_ref[0, :, hd * HEAD_DIM:(hd + 1) * HEAD_DIM] = kh.astype(k_ref.dtype)
    v_ref[0] = kv[:, D_KV:].astype(v_ref.dtype)

    c0 += 2 * D_KV
    l2 = proj(c0, c0 + 2 * D_LRU)
    lg_ref[0] = l2[:, :D_LRU]
    lx_ref[0] = l2[:, D_LRU:]


def _rope_tables(seq):
    n_freq = HEAD_DIM // 4
    inv = (ROPE_BASE ** (-np.arange(n_freq, dtype=np.float32) / n_freq)).astype(np.float32)
    t = np.arange(seq)
    ang_r = (t // GRID_W).astype(np.float32)[:, None] * inv[None, :]
    ang_c = (t % GRID_W).astype(np.float32)[:, None] * inv[None, :]
    cos = np.concatenate([np.cos(ang_r)] * 2 + [np.cos(ang_c)] * 2, axis=-1)
    sin = np.concatenate([-np.sin(ang_r), np.sin(ang_r), -np.sin(ang_c), np.sin(ang_c)], axis=-1)
    return jnp.asarray(cos, F32), jnp.asarray(sin, F32)


def _cast_specs(w, layer, n_inner, n_steps):
    _, n_e, rows, cols = w.shape
    parts = n_steps // n_e
    assert parts * n_e == n_steps and rows % parts == 0
    piece = lambda b, i: divmod(b * n_inner + i, parts)
    src = pl.BlockSpec((1, 1, rows // parts, cols), lambda b, i: (layer, *piece(b, i), 0))
    dst = pl.BlockSpec((1, rows // parts, cols), lambda b, i: (*piece(b, i), 0))
    return src, dst, jax.ShapeDtypeStruct((n_e, rows, cols), BF16)


def _inproj(x, mods, row0, row_step, g1, w_in_bf, qg, kg, rope, tm, layer, cast_weights):
    bsz, seq, _ = x.shape
    tok = lambda width: pl.BlockSpec((1, tm, width), lambda b, i: (b, i, 0))
    in_specs = [
        tok(D_MODEL),
        pl.BlockSpec((1, 6, D_MODEL), lambda b, i: (row0 + row_step * b, 0, 0)),
        _const_spec((1, D_MODEL)),
        _layer_spec((D_MODEL, D_IN), layer),
        _const_spec((1, HEAD_DIM)),
        _const_spec((1, HEAD_DIM)),
    ]
    args = [x, mods, g1.reshape(1, D_MODEL), w_in_bf, qg.reshape(1, HEAD_DIM), kg.reshape(1, HEAD_DIM)]
    if rope:
        cos, sin = _rope_tables(seq)
        in_specs += [pl.BlockSpec((tm, HEAD_DIM), lambda b, i: (i, 0))] * 2
        args += [cos, sin]
    widths = (D_CONV, D_ATTN, D_KV, D_KV, D_LRU, D_LRU)
    kv_dtype = BF16 if rope else F32
    dtypes = (F32, BF16, kv_dtype, kv_dtype, F32, F32)
    out_shape = [jax.ShapeDtypeStruct((bsz, seq, w), dt) for w, dt in zip(widths, dtypes)]
    out_specs = [tok(w) for w in widths]
    n_inner = seq // tm
    for w in cast_weights:
        src, dst, shape = _cast_specs(w, layer, n_inner, bsz * n_inner)
        in_specs.append(src)
        args.append(w)
        out_specs.append(dst)
        out_shape.append(shape)
    outs = pl.pallas_call(
        functools.partial(_inproj_kernel, rope=rope, n_cast=len(cast_weights)),
        out_shape=out_shape,
        grid=(bsz, n_inner),
        in_specs=in_specs,
        out_specs=out_specs,
        compiler_params=_cparams("parallel", "parallel"),
        name="inproj_rope" if rope else "inproj",
    )(*args)
    return outs[:6], outs[6:]


CONV_PAD = (CONV_K - 1) // 2
CONV_HALO = 16
CONV_CHUNK = 64
CONV_COLS = 256


def _conv_kernel(z_ref, w_ref, b_ref, g_ref, beta_ref, o_ref, zp_ref, win_ref, acc_ref, *, seq):
    zeros = jnp.zeros((CONV_HALO, D_CONV), F32)
    zp_ref[0:CONV_HALO, :] = zeros
    zp_ref[CONV_HALO + seq:2 * CONV_HALO + seq, :] = zeros
    zp_ref[CONV_HALO:CONV_HALO + seq, :] = z_ref[0]
    n_parts = D_CONV // CONV_COLS

    def body(c, carry):
        r0 = pl.multiple_of(c * CONV_CHUNK, CONV_CHUNK)
        for p in range(n_parts):
            win_ref[p] = zp_ref[pl.ds(r0, CONV_CHUNK + 2 * CONV_HALO), p * CONV_COLS:(p + 1) * CONV_COLS]

        def taps(p, inner):
            part = jnp.broadcast_to(b_ref[p], (CONV_CHUNK, CONV_COLS))
            for k in range(CONV_K):
                off = CONV_HALO - CONV_PAD + k
                part = part + jnp.tile(w_ref[p, k], (CONV_CHUNK // SUB, 1)) * win_ref[p, off:off + CONV_CHUNK, :]
            acc_ref[p] = part
            return inner

        lax.fori_loop(0, n_parts, taps, 0)
        acc = jnp.concatenate([acc_ref[p] for p in range(n_parts)], axis=1)
        mu = jnp.mean(acc, axis=-1, keepdims=True)
        xc = acc - mu
        var = jnp.mean(xc * xc, axis=-1, keepdims=True)
        y = xc * lax.rsqrt(var + EPS) * g_ref[...] + beta_ref[...]
        o_ref[0, pl.ds(r0, CONV_CHUNK), :] = _silu(y).astype(BF16)
        return carry

    lax.fori_loop(0, seq // CONV_CHUNK, body, 0)


def _conv_module(z, w, b, g, beta):
    bsz, seq, _ = z.shape
    row = lambda a: a.reshape(1, D_CONV)
    n_parts = D_CONV // CONV_COLS
    w_parts = jnp.broadcast_to(w.reshape(CONV_K, 1, n_parts, CONV_COLS), (CONV_K, SUB, n_parts, CONV_COLS))
    w_parts = w_parts.transpose(2, 0, 1, 3)
    return pl.pallas_call(
        functools.partial(_conv_kernel, seq=seq),
        out_shape=jax.ShapeDtypeStruct((bsz, seq, D_CONV), BF16),
        grid=(bsz,),
        in_specs=[
            pl.BlockSpec((1, seq, D_CONV), lambda i: (i, 0, 0)),
            _const_spec((n_parts, CONV_K, SUB, CONV_COLS)),
            _const_spec((n_parts, 1, CONV_COLS)), _const_spec((1, D_CONV)), _const_spec((1, D_CONV)),
        ],
        out_specs=pl.BlockSpec((1, seq, D_CONV), lambda i: (i, 0, 0)),
        scratch_shapes=[pltpu.VMEM((seq + 2 * CONV_HALO, D_CONV), F32),
                        pltpu.VMEM((n_parts, CONV_CHUNK + 2 * CONV_HALO, CONV_COLS), F32),
                        pltpu.VMEM((n_parts, CONV_CHUNK, CONV_COLS), F32)],
        compiler_params=_cparams("parallel"),
        name="conv_module",
    )(z, w_parts, b.reshape(n_parts, 1, CONV_COLS), row(g), row(beta))


ATTN_ROWS = 64


def _attn_kernel(*refs, latent, seq, tq):
    if latent:
        sink_ref, q_ref, k_ref, v_ref, kc_ref, vc_ref, o_ref, s_ref, p_ref, bias_ref = refs
    else:
        sink_ref, q_ref, k_ref, v_ref, o_ref, s_ref, p_ref = refs
    q = q_ref[0]
    n_loc = 3 * tq if latent else 0
    if latent:
        n = pl.program_id(1)
        nblk = seq // tq
        blocks = (jnp.maximum(n - 1, 0), n, jnp.minimum(n + 1, nblk - 1))

        def window(ref):
            parts = [ref[0, pl.ds(pl.multiple_of(i * tq, tq), tq), :] for i in blocks]
            return jnp.concatenate(parts, axis=0)

        k_all = jnp.concatenate([window(k_ref), kc_ref[0]], axis=0).astype(BF16)
        v_all = jnp.concatenate([window(v_ref), vc_ref[0]], axis=0).astype(BF16)
        qpos = n * tq + lax.broadcasted_iota(jnp.int32, (tq, 3 * tq), 0)
        kpos = (n - 1) * tq + lax.broadcasted_iota(jnp.int32, (tq, 3 * tq), 1)
        ok = (jnp.abs(qpos - kpos) <= WINDOW) & (kpos >= 0) & (kpos < seq)
        bias_ref[...] = jnp.where(ok, 0.0, NEG).astype(F32)
    else:
        k_all = k_ref[0].astype(BF16)
        v_all = v_ref[0].astype(BF16)

    for j in range(N_KV_HEADS):
        heads = [j * GROUP + g for g in range(GROUP)]
        qs = jnp.concatenate([q[:, h * HEAD_DIM:(h + 1) * HEAD_DIM] for h in heads], axis=0)
        kj = k_all[:, j * HEAD_DIM:(j + 1) * HEAD_DIM]
        vj = v_all[:, j * HEAD_DIM:(j + 1) * HEAD_DIM]
        s_ref[...] = lax.dot_general(qs, kj, (((1,), (1,)), ((), ())), preferred_element_type=F32)
        inv = []
        for rb in range(GROUP * tq // ATTN_ROWS):
            rows = slice(rb * ATTN_ROWS, (rb + 1) * ATTN_ROWS)
            sk = sink_ref[heads[rb * ATTN_ROWS // tq]]
            parts = [s_ref[rows, n_loc:]]
            if latent:
                q0 = rb * ATTN_ROWS % tq
                parts.insert(0, s_ref[rows, :n_loc] + bias_ref[q0:q0 + ATTN_ROWS, :])
            m = sk
            for s in parts:
                m = jnp.maximum(m, jnp.max(s, axis=-1, keepdims=True))
            den = jnp.exp(sk - m)
            c0 = 0
            for s in parts:
                p = jnp.exp(s - m)
                den = den + jnp.sum(p, axis=-1, keepdims=True)
                p_ref[rows, c0:c0 + s.shape[1]] = p.astype(BF16)
                c0 += s.shape[1]
            inv.append(1.0 / den)
        o = jnp.dot(p_ref[...], vj, preferred_element_type=F32) * jnp.concatenate(inv, axis=0)
        for g, h in enumerate(heads):
            o_ref[0, :, h * HEAD_DIM:(h + 1) * HEAD_DIM] = o[g * tq:(g + 1) * tq].astype(BF16)


def _attention(q, k, v, sink, k_ctx=None, v_ctx=None):
    bsz, seq, _ = q.shape
    latent = k_ctx is not None
    tq = WINDOW if latent else seq
    seq_spec = pl.BlockSpec((1, seq, D_KV), lambda b, i: (b, 0, 0))
    in_specs = [
        pl.BlockSpec(memory_space=pltpu.SMEM),
        pl.BlockSpec((1, tq, D_ATTN), lambda b, i: (b, i, 0)),
        seq_spec, seq_spec,
    ]
    args = [sink, q, k, v]
    n_keys = seq
    if latent:
        past = k_ctx.shape[1]
        ctx_spec = pl.BlockSpec((1, past, D_KV), lambda b, i: (b, 0, 0))
        in_specs += [ctx_spec, ctx_spec]
        args += [k_ctx, v_ctx]
        n_keys = 3 * tq + past
    scratch = [pltpu.VMEM((GROUP * tq, n_keys), F32), pltpu.VMEM((GROUP * tq, n_keys), BF16)]
    if latent:
        scratch.append(pltpu.VMEM((tq, 3 * tq), F32))
    return pl.pallas_call(
        functools.partial(_attn_kernel, latent=latent, seq=seq, tq=tq),
        out_shape=jax.ShapeDtypeStruct((bsz, seq, D_ATTN), BF16),
        grid=(bsz, seq // tq),
        in_specs=in_specs,
        out_specs=pl.BlockSpec((1, tq, D_ATTN), lambda b, i: (b, i, 0)),
        scratch_shapes=scratch,
        compiler_params=_cparams("parallel", "parallel"),
        name="attn_latent" if latent else "attn_context",
    )(*args)


LRU_HALO = 8
LRU_CHUNK = 128
LRU_HALF = D_LRU // 2


def _softplus(x):
    return jnp.maximum(x, 0.0) + jnp.log(1.0 + jnp.exp(-jnp.abs(x)))


def _gelu_tanh(x):
    return 0.5 * x * (1.0 + jnp.tanh(0.7978845608028654 * (x + 0.044715 * (x * x * x))))


def _scan_tile(a, b, carry, reverse):
    row = lax.broadcasted_iota(jnp.int32, a.shape, 0)
    for d in (1, 2, 4):
        if reverse:
            valid = row < SUB - d
            shift = SUB - d
        else:
            valid = row >= d
            shift = d
        a_prev = jnp.where(valid, pltpu.roll(a, shift, 0), 1.0)
        b_prev = jnp.where(valid, pltpu.roll(b, shift, 0), 0.0)
        b = a * b_prev + b
        a = a * a_prev
    h = a * carry + b
    last = h[0:1, :] if reverse else h[SUB - 1:SUB, :]
    return h, last


def _lru_kernel(lx_ref, lg_ref, h0_ref, cw_ref, cb_ref, wbd_ref, gbias_ref, lam_ref,
                o_ref, fin_ref, xp_ref, win_ref, af_ref, bf_ref, ab_ref, bb_ref, *, seq):
    zeros = jnp.zeros((LRU_HALO, D_LRU), F32)
    xp_ref[0:LRU_HALO, :] = zeros
    xp_ref[LRU_HALO + seq:2 * LRU_HALO + seq, :] = zeros
    xp_ref[LRU_HALO:LRU_HALO + seq, :] = lx_ref[0]
    a_refs = (af_ref, ab_ref)
    b_refs = (bf_ref, bb_ref)
    half_c = (-0.5 * LRU_C) * _softplus(-lam_ref[...])

    def gates(c, carry):
        r0 = pl.multiple_of(c * LRU_CHUNK, LRU_CHUNK)
        win_ref[...] = xp_ref[pl.ds(r0, LRU_CHUNK + 2 * LRU_HALO), :]
        xc = jnp.broadcast_to(cb_ref[...], (LRU_CHUNK, D_LRU))
        for k in range(LRU_CONV_K):
            off = LRU_HALO - 2 + k
            xc = xc + cw_ref[k:k + 1, :] * win_ref[off:off + LRU_CHUNK, :]
        for s in range(2):
            cols = slice(s * LRU_HALF, (s + 1) * LRU_HALF)
            xs = xc[:, cols]
            g = jnp.dot(xs.astype(BF16), wbd_ref[s], preferred_element_type=F32)
            g = g + gbias_ref[s:s + 1, :]
            for d in range(2):
                base = d * 2 * LRU_HALF
                hc = half_c[d:d + 1, cols]
                log_a = hc * jnp.tanh(g[:, base:base + LRU_HALF]) + hc
                i = 0.5 * jnp.tanh(g[:, base + LRU_HALF:base + 2 * LRU_HALF]) + 0.5
                a = jnp.exp(log_a)
                t = jnp.tanh(log_a)
                b = jnp.sqrt(-2.0 * t / (1.0 - t)) * (i * xs)
                a_refs[d][pl.ds(r0, LRU_CHUNK), cols] = a
                b_refs[d][pl.ds(r0, LRU_CHUNK), cols] = b
        return carry

    lax.fori_loop(0, seq // LRU_CHUNK, gates, 0)

    ntile = seq // SUB

    def scan(t, carry):
        cf, cb = carry
        rf = pl.multiple_of(t * SUB, SUB)
        rb = pl.multiple_of((ntile - 1 - t) * SUB, SUB)
        hf, cf = _scan_tile(af_ref[pl.ds(rf, SUB), :], bf_ref[pl.ds(rf, SUB), :], cf, False)
        hb, cb = _scan_tile(ab_ref[pl.ds(rb, SUB), :], bb_ref[pl.ds(rb, SUB), :], cb, True)
        bf_ref[pl.ds(rf, SUB), :] = hf
        bb_ref[pl.ds(rb, SUB), :] = hb
        return cf, cb

    h0 = h0_ref[0]
    cf, cb = lax.fori_loop(0, ntile, scan, (h0[0:1, :], h0[1:2, :]))
    fin_ref[0, 0:1, :] = cf
    fin_ref[0, 1:2, :] = cb

    def finish(c, carry):
        r0 = pl.multiple_of(c * LRU_CHUNK, LRU_CHUNK)
        rows = pl.ds(r0, LRU_CHUNK)
        o_ref[0, rows, :] = ((bf_ref[rows, :] + bb_ref[rows, :]) * _gelu_tanh(lg_ref[0, rows, :])).astype(BF16)
        return carry

    lax.fori_loop(0, seq // LRU_CHUNK, finish, 0)


def _lru_gate_weights(wa, ba, wx, bx):
    per_half = LRU_HALF // LRU_BLK
    eye = jnp.eye(per_half, dtype=F32)

    def dense(w):
        blocks = w.reshape(2, per_half, LRU_BLK, LRU_BLK)
        return jnp.einsum("snkj,nm->snkmj", blocks, eye).reshape(2, LRU_HALF, LRU_HALF)

    wbd = jnp.concatenate([dense(wa[0]), dense(wx[0]), dense(wa[1]), dense(wx[1])], axis=2)
    gbias = jnp.stack([ba[0], bx[0], ba[1], bx[1]]).reshape(4, 2, LRU_HALF).transpose(1, 0, 2)
    return (0.5 * wbd).astype(BF16), 0.5 * gbias.reshape(2, 4 * LRU_HALF)


def _recurrent_mixer(lx, lg, h0, cw, cb, wbd, gbias, lam):
    bsz, seq, _ = lx.shape
    seq_spec = pl.BlockSpec((1, seq, D_LRU), lambda i: (i, 0, 0))
    state_spec = pl.BlockSpec((1, 2, D_LRU), lambda i: (i, 0, 0))
    return pl.pallas_call(
        functools.partial(_lru_kernel, seq=seq),
        out_shape=[jax.ShapeDtypeStruct((bsz, seq, D_LRU), BF16),
                   jax.ShapeDtypeStruct((bsz, 2, D_LRU), F32)],
        grid=(bsz,),
        in_specs=[
            seq_spec, seq_spec, state_spec,
            _const_spec((LRU_CONV_K, D_LRU)), _const_spec((1, D_LRU)),
            _const_spec((2, LRU_HALF, 4 * LRU_HALF)), _const_spec((2, 4 * LRU_HALF)),
            _const_spec((2, D_LRU)),
        ],
        out_specs=[seq_spec, state_spec],
        scratch_shapes=[pltpu.VMEM((seq + 2 * LRU_HALO, D_LRU), F32),
                        pltpu.VMEM((LRU_CHUNK + 2 * LRU_HALO, D_LRU), F32)] + [pltpu.VMEM((seq, D_LRU), F32)] * 4,
        compiler_params=_cparams("parallel"),
        name="rglru",
    )(lx, lg, h0, cw, cb.reshape(1, D_LRU), wbd, gbias, lam)


N_PAIRS = EXPERTS_PER_GROUP * (EXPERTS_PER_GROUP - 1) // 2
N_BUCKETS = N_EXPERT_GROUPS * N_PAIRS
PAIR_LO = (0, 0, 0, 1, 1, 2)
PAIR_HI = (1, 2, 3, 2, 3, 3)
ROUTE_ROWS = SUB


def _route(scores, biased):
    rows = [biased[e:e + 1, :] for e in range(N_EXPERTS)]
    group_score = []
    for g in range(N_EXPERT_GROUPS):
        a, b, c, d = rows[4 * g:4 * g + 4]
        hi1, lo1 = jnp.maximum(a, b), jnp.minimum(a, b)
        hi2, lo2 = jnp.maximum(c, d), jnp.minimum(c, d)
        top = jnp.maximum(hi1, hi2)
        second = jnp.maximum(jnp.minimum(hi1, hi2), jnp.maximum(lo1, lo2))
        group_score.append(top + second)
    best = group_score[0]
    g_sel = jnp.zeros_like(best, dtype=jnp.int32)
    for g in range(1, N_EXPERT_GROUPS):
        better = group_score[g] > best
        g_sel = jnp.where(better, g, g_sel)
        best = jnp.where(better, group_score[g], best)
    sel = []
    for e in range(N_EXPERTS):
        g = e // EXPERTS_PER_GROUP
        rank = jnp.zeros_like(g_sel)
        for o in range(g * EXPERTS_PER_GROUP, (g + 1) * EXPERTS_PER_GROUP):
            if o == e:
                continue
            ahead = (rows[o] >= rows[e]) if o < e else (rows[o] > rows[e])
            rank = rank + ahead.astype(jnp.int32)
        sel.append(jnp.where(jnp.where(g_sel == g, rank, 2) < 2, 1, 0))
    zero = jnp.zeros_like(best)
    lo_w, hi_w = zero, zero
    lo_idx = jnp.zeros_like(g_sel)
    hi_idx = jnp.zeros_like(g_sel)
    for g in range(N_EXPERT_GROUPS):
        seen = jnp.zeros_like(g_sel)
        for j in range(EXPERTS_PER_GROUP):
            e = g * EXPERTS_PER_GROUP + j
            order = jnp.where(sel[e] == 1, seen, 2)
            lo_w = jnp.where(order == 0, scores[e:e + 1, :], lo_w)
            hi_w = jnp.where(order == 1, scores[e:e + 1, :], hi_w)
            lo_idx = jnp.where(order == 0, j, lo_idx)
            hi_idx = jnp.where(order == 1, j, hi_idx)
            seen = seen + sel[e]
    pair_base = jnp.where(lo_idx == 0, 0, jnp.where(lo_idx == 1, 3, 5))
    bucket = g_sel * N_PAIRS + pair_base + hi_idx - lo_idx - 1
    total = lo_w + hi_w
    pad = jnp.zeros((ROUTE_ROWS - 3, best.shape[1]), F32)
    return jnp.concatenate([bucket.astype(F32), lo_w / total, hi_w / total, pad], axis=0)


def _modulated_norm(x, g, shift, scale):
    y = x * lax.rsqrt(jnp.mean(x * x, axis=-1, keepdims=True) + EPS) * g
    return y * (1.0 + scale) + shift


META_LANES = 128
META_W_LO, META_W_HI, META_MOD_ROW = 1, 2, 3


def _outproj_kernel(*refs, row0, row_step, n_cast):
    conv_ref, attn_ref, lru_ref, x_ref, mod_ref, g2_ref, w_ref, wr_ref, br_ref = refs[:9]
    cast_src = refs[9:9 + n_cast]
    x1_ref, route_ref = refs[9 + n_cast:11 + n_cast]
    cast_dst = refs[11 + n_cast:11 + 2 * n_cast]
    cat_ref = refs[-1]
    for src, dst in zip(cast_src, cast_dst):
        dst[0] = src[0, 0].astype(BF16)
    cat_ref[:, 0:D_CONV] = conv_ref[0]
    cat_ref[:, D_CONV:D_CONV + D_ATTN] = attn_ref[0]
    cat_ref[:, D_CONV + D_ATTN:] = lru_ref[0]
    mix = jnp.dot(cat_ref[...], w_ref[0], preferred_element_type=F32)
    mod = mod_ref[0]
    x1 = x_ref[0] + mod[2:3] * mix
    x1_ref[0, :, 0:D_MODEL] = x1
    h2 = _modulated_norm(x1, g2_ref[...], mod[3:4], mod[4:5])
    h_hi, h_lo = _split_bf16(h2)
    nt = (((1,), (1,)), ((), ()))
    by_hi = lax.dot_general(wr_ref[...], h_hi, nt, preferred_element_type=F32)
    by_lo = lax.dot_general(wr_ref[0:N_EXPERTS, :], h_lo, nt, preferred_element_type=F32)
    logits = by_hi[0:N_EXPERTS] + by_hi[N_EXPERTS:] + by_lo
    scores = _sigmoid(logits)
    route = _route(scores, scores + br_ref[...])
    route_ref[0] = route
    tm = route.shape[1]
    mod_row = jnp.full((1, tm), row0, jnp.int32) + row_step * pl.program_id(0)
    rows = jnp.concatenate([route[0:META_MOD_ROW], mod_row.astype(F32),
                            jnp.zeros((META_LANES - META_MOD_ROW - 1, tm), F32)], axis=0)
    x1_ref[0, :, D_MODEL:] = rows.T


def _outproj(conv, attn, lru, x, mods, row0, row_step, g2, w_out_bf, w_router_t, b_router, tm, layer,
             cast_weights):
    bsz, seq, _ = x.shape
    tok = lambda width: pl.BlockSpec((1, tm, width), lambda b, i: (b, i, 0))
    in_specs = [
        tok(D_CONV), tok(D_ATTN), tok(D_LRU), tok(D_MODEL),
        pl.BlockSpec((1, 6, D_MODEL), lambda b, i: (row0 + row_step * b, 0, 0)),
        _const_spec((1, D_MODEL)),
        _layer_spec((D_MODEL, D_MODEL), layer),
        _const_spec((2 * N_EXPERTS, D_MODEL)),
        _const_spec((N_EXPERTS, 1)),
    ]
    args = [conv, attn, lru, x, mods, g2.reshape(1, D_MODEL), w_out_bf, w_router_t, b_router.reshape(N_EXPERTS, 1)]
    out_shape = [jax.ShapeDtypeStruct((bsz, seq, D_MODEL + META_LANES), F32),
                 jax.ShapeDtypeStruct((bsz, ROUTE_ROWS, seq), F32)]
    out_specs = [tok(D_MODEL + META_LANES), pl.BlockSpec((1, ROUTE_ROWS, tm), lambda b, i: (b, 0, i))]
    n_inner = seq // tm
    for w in cast_weights:
        src, dst, shape = _cast_specs(w, layer, n_inner, bsz * n_inner)
        in_specs.append(src)
        args.append(w)
        out_specs.append(dst)
        out_shape.append(shape)
    outs = pl.pallas_call(
        functools.partial(_outproj_kernel, row0=row0, row_step=row_step, n_cast=len(cast_weights)),
        out_shape=out_shape,
        grid=(bsz, n_inner),
        in_specs=in_specs,
        out_specs=out_specs,
        scratch_shapes=[pltpu.VMEM((tm, D_MODEL), BF16)],
        compiler_params=_cparams("parallel", "parallel"),
        name="outproj_router",
    )(*args)
    return outs[0], outs[1], outs[2:]


MOE_TILE = 256
MOE_UNROLL = 8
MOE_PARTS = 2


def _moe_plan(bucket, t_ctx, tm):
    t_all = bucket.shape[0]
    n_tiles = (t_all + N_BUCKETS * (tm - 1) + tm - 1) // tm
    ids = jnp.arange(N_BUCKETS, dtype=jnp.int32)
    order = jnp.argsort(bucket, stable=True).astype(jnp.int32)
    member = (bucket[None, :] == ids[:, None]).astype(jnp.int32)
    counts = jnp.sum(member, axis=1)
    counts_ctx = jnp.sum(member[:, :t_ctx], axis=1)
    tiles_per = (counts + tm - 1) // tm
    tile_end = jnp.cumsum(tiles_per)
    tile_start = tile_end - tiles_per
    sorted_start = jnp.cumsum(counts) - counts
    n_used = tile_end[-1]
    j = jnp.arange(n_tiles, dtype=jnp.int32)
    jj = jnp.minimum(j, n_used - 1)
    b_of = jnp.sum((tile_end[None, :] <= jj[:, None]).astype(jnp.int32), axis=1)
    local = jj - tile_start[b_of]
    n_valid = jnp.where(j < n_used, jnp.clip(counts[b_of] - local * tm, 0, tm), 0)
    n_ctx = jnp.clip(counts_ctx[b_of] - local * tm, 0, n_valid)
    first = jnp.where(j < n_used, sorted_start[b_of] + local * tm, 0)
    group, pair = b_of // N_PAIRS, b_of % N_PAIRS
    e_lo = group * EXPERTS_PER_GROUP + jnp.asarray(PAIR_LO, jnp.int32)[pair]
    e_hi = group * EXPERTS_PER_GROUP + jnp.asarray(PAIR_HI, jnp.int32)[pair]
    i32 = lambda a: a.astype(jnp.int32)
    return order, i32(first), i32(e_lo), i32(e_hi), i32(n_valid), i32(n_ctx)


def _moe_kernel(order_ref, first_ref, elo_ref, ehi_ref, nv_ref, nc_ref,
                xc_ref, xl_ref, mods_ref, g2_ref,
                wg_lo, wu_lo, wd_lo, wg_hi, wu_hi, wd_hi,
                oc_ref, ol_ref, xbuf, ybuf, gsem, ssem, *, tm, t_ctx, n_mod_rows):
    i = pl.program_id(0)
    n = pl.num_programs(0)
    slot = lax.rem(i, 2)

    def gather(hbm, t, r, s, size):
        return pltpu.make_async_copy(hbm.at[pl.ds(t, size), :], xbuf.at[s, pl.ds(r, size), :], gsem.at[s])

    def scatter(hbm, t, r, s, size):
        return pltpu.make_async_copy(ybuf.at[s, pl.ds(r, size), :], hbm.at[pl.ds(t, size), :], ssem.at[s])

    def start_rows(j, s, copy, hbm_ctx, hbm_lat):
        base = first_ref[j]

        def ctx_row(r):
            copy(hbm_ctx, order_ref[base + r], r, s, 1).start()

        def lat_row(r):
            copy(hbm_lat, order_ref[base + r] - t_ctx, r, s, 1).start()

        def rows(lo, hi, one_row):
            groups = lax.shift_right_logical(hi - lo, MOE_UNROLL.bit_length() - 1)

            def group(g, c):
                for u in range(MOE_UNROLL):
                    one_row(lo + g * MOE_UNROLL + u)
                return c

            def single(r, c):
                one_row(r)
                return c

            lax.fori_loop(0, groups, group, 0)
            lax.fori_loop(lo + groups * MOE_UNROLL, hi, single, 0)

        rows(0, nc_ref[j], ctx_row)
        rows(nc_ref[j], nv_ref[j], lat_row)

    def wait_rows(j, s, copy, hbm):
        cnt = nv_ref[j]
        bulk = pl.multiple_of(lax.shift_left(lax.shift_right_logical(cnt, 3), 3), SUB)

        @pl.when(bulk > 0)
        def _():
            copy(hbm, 0, 0, s, bulk).wait()

        def one(r, c):
            copy(hbm, 0, 0, s, 1).wait()
            return c

        lax.fori_loop(bulk, cnt, one, 0)

    @pl.when(i == 0)
    def _():
        xbuf[...] = jnp.zeros_like(xbuf)
        start_rows(0, 0, gather, xc_ref, xl_ref)

    @pl.when(i + 1 < n)
    def _():
        start_rows(i + 1, 1 - slot, gather, xc_ref, xl_ref)

    wait_rows(i, slot, gather, xl_ref)

    @pl.when(i >= 2)
    def _():
        wait_rows(i - 2, slot, scatter, ol_ref)

    def tile_part(rows):
        x = xbuf[slot, rows, 0:D_MODEL]
        meta = xbuf[slot, rows, D_MODEL:]
        w_lo = meta[:, META_W_LO:META_W_LO + 1]
        w_hi = meta[:, META_W_HI:META_W_HI + 1]
        mrow = meta[:, META_MOD_ROW:META_MOD_ROW + 1]
        n_rows = x.shape[0]

        def per_row(*ks):
            ids = jnp.broadcast_to(mrow, (n_rows, 128))
            masks = [ids == float(r) for r in range(1, n_mod_rows)]
            cols = [[] for _ in ks]
            for c in range(D_MODEL // 128):
                lanes = slice(c * 128, (c + 1) * 128)
                for q, k in enumerate(ks):
                    v = jnp.broadcast_to(mods_ref[0, k:k + 1, lanes], (n_rows, 128))
                    for r, mask in enumerate(masks, 1):
                        v = jnp.where(mask, mods_ref[r, k:k + 1, lanes], v)
                    cols[q].append(v)
            return [jnp.concatenate(col, axis=1) for col in cols]

        shift, scale = per_row(3, 4)
        h = _modulated_norm(x, g2_ref[...], shift, scale).astype(BF16)

        def expert(wg, wu, w):
            act = _silu(jnp.dot(h, wg[0], preferred_element_type=F32))
            act = act * jnp.dot(h, wu[0], preferred_element_type=F32) * w
            return act.astype(BF16)

        y = jnp.dot(expert(wg_lo, wu_lo, w_lo), wd_lo[0], preferred_element_type=F32)
        y = y + jnp.dot(expert(wg_hi, wu_hi, w_hi), wd_hi[0], preferred_element_type=F32)
        (gate,) = per_row(5)
        ybuf[slot, rows, :] = x + gate * y

    @pl.when(nv_ref[i] > 0)
    def _():
        part = tm // MOE_PARTS
        for p in range(MOE_PARTS):
            tile_part(slice(p * part, (p + 1) * part))

    start_rows(i, slot, scatter, oc_ref, ol_ref)

    @pl.when(i == n - 1)
    def _():
        wait_rows(i, slot, scatter, ol_ref)

        @pl.when(i >= 1)
        def _():
            wait_rows(i - 1, 1 - slot, scatter, ol_ref)


def _moe(x1c, x1l, route_c, route_l, mods, g2, wg, wu, wd, n_mod_rows):
    t_ctx, t_lat = x1c.shape[0], x1l.shape[0]
    tm = MOE_TILE
    bucket = jnp.concatenate([route_c[:, 0, :].reshape(t_ctx), route_l[:, 0, :].reshape(t_lat)])
    order, first, e_lo, e_hi, n_valid, n_ctx = _moe_plan(bucket.astype(jnp.int32), t_ctx, tm)
    n_tiles = n_valid.shape[0]
    w_up = lambda sel: pl.BlockSpec((1, D_MODEL, D_EXPERT),
                                    lambda i, order, first, lo, hi, nv, nc: ((lo, hi)[sel][i], 0, 0))
    w_dn = lambda sel: pl.BlockSpec((1, D_EXPERT, D_MODEL),
                                    lambda i, order, first, lo, hi, nv, nc: ((lo, hi)[sel][i], 0, 0))
    hbm = pl.BlockSpec(memory_space=pl.ANY)
    return pl.pallas_call(
        functools.partial(_moe_kernel, tm=tm, t_ctx=t_ctx, n_mod_rows=n_mod_rows),
        out_shape=[jax.ShapeDtypeStruct((t_ctx, D_MODEL), F32), jax.ShapeDtypeStruct((t_lat, D_MODEL), F32)],
        grid_spec=pltpu.PrefetchScalarGridSpec(
            num_scalar_prefetch=6,
            grid=(n_tiles,),
            in_specs=[
                hbm, hbm,
                _const_spec((MOD_ROWS, 6, D_MODEL)),
                _const_spec((1, D_MODEL)),
                w_up(0), w_up(0), w_dn(0), w_up(1), w_up(1), w_dn(1),
            ],
            out_specs=[hbm, hbm],
            scratch_shapes=[
                pltpu.VMEM((2, tm, D_MODEL + META_LANES), F32), pltpu.VMEM((2, tm, D_MODEL), F32),
                pltpu.SemaphoreType.DMA((2,)), pltpu.SemaphoreType.DMA((2,)),
            ],
        ),
        compiler_params=_cparams("arbitrary"),
        name="moe_pairs",
    )(order, first, e_lo, e_hi, n_valid, n_ctx, x1c, x1l, mods, g2.reshape(1, D_MODEL), wg, wu, wd, wg, wu, wd)


def _mixers(x, p, mods, row0, row_step, h0, layer, casts_in, casts_out, k_ctx=None, v_ctx=None):
    latent = k_ctx is not None
    bsz, seq, _ = x.shape
    tm = min(seq, 512)
    (z, q, k, v, lg, lx), cast_a = _inproj(x, mods, row0, row_step, p["norm1_g"], p["w_in"], p["q_norm_g"],
                                           p["k_norm_g"], latent, tm, layer, casts_in)
    conv = _conv_module(z, p["conv_dw"], p["conv_b"], p["conv_ln_g"], p["conv_ln_b"])
    attn = _attention(q, k, v, p["attn_sink"], k_ctx, v_ctx)
    lru, fin = _recurrent_mixer(lx, lg, h0, p["lru_conv_w"], p["lru_conv_b"], p["lru_wbd"], p["lru_gbias"],
                                p["lru_lam"])
    x1, route, cast_b = _outproj(conv, attn, lru, x, mods, row0, row_step, p["norm2_g"], p["w_out"],
                                 p["w_router_t"], p["b_router"], tm, layer, casts_out)
    return x1, route, k, v, fin, list(cast_a) + list(cast_b)


def kernel(x_prompt, x_sample, c, cache_k, cache_v, state_lru, c_ctx, w_mod, b_mod, norm1_g, norm2_g, w_in,
           conv_dw, conv_b, conv_ln_g, conv_ln_b, q_norm_g, k_norm_g, attn_sink, lru_conv_w, lru_conv_b,
           lru_wa, lru_ba, lru_wx, lru_bx, lru_lam, w_out, w_router, b_router, w_gate_e, w_up_e, w_down_e):
    bsz, seq, _ = x_prompt.shape
    dec_bsz, dec_seq, _ = x_sample.shape
    past = cache_k.shape[2]

    cvec = jnp.zeros((MOD_ROWS, D_MODEL), F32).at[0].set(c_ctx).at[1:1 + dec_bsz].set(c)
    mods_all = _modulation(cvec, w_mod, b_mod).reshape(DEPTH, MOD_ROWS, 6, D_MODEL)

    wr_hi = w_router.T.astype(BF16)
    wr_lo = (w_router.T - wr_hi.astype(F32)).astype(BF16)
    w_router_t = jnp.concatenate([wr_hi, wr_lo], axis=0)
    w_in_bf, w_out_bf = w_in.astype(BF16), w_out.astype(BF16)
    layers = []
    for l in range(DEPTH):
        wbd, gbias = _lru_gate_weights(lru_wa[l], lru_ba[l], lru_wx[l], lru_bx[l])
        layers.append({
            "norm1_g": norm1_g[l], "norm2_g": norm2_g[l], "w_in": w_in_bf,
            "conv_dw": conv_dw[l], "conv_b": conv_b[l], "conv_ln_g": conv_ln_g[l], "conv_ln_b": conv_ln_b[l],
            "q_norm_g": q_norm_g[l], "k_norm_g": k_norm_g[l], "attn_sink": attn_sink[l],
            "lru_conv_w": lru_conv_w[l], "lru_conv_b": lru_conv_b[l], "lru_wbd": wbd, "lru_gbias": gbias,
            "lru_lam": lru_lam[l],
            "w_out": w_out_bf, "w_router_t": w_router_t, "b_router": b_router,
        })

    y, z = x_prompt, x_sample
    ks, vs, hs = [], [], []
    h0_ctx = jnp.zeros((bsz, 2, D_LRU), F32)
    for l in range(DEPTH):
        p, mods = layers[l], mods_all[l]
        y1, route_c, k_l, v_l, h_l, (wg_bf, wd_bf) = _mixers(y, p, mods, 0, 0, h0_ctx, l, [w_gate_e], [w_down_e])
        ks.append(k_l.reshape(bsz, seq, N_KV_HEADS, HEAD_DIM))
        vs.append(v_l.reshape(bsz, seq, N_KV_HEADS, HEAD_DIM))
        hs.append(h_l)
        z1, route_l, _, _, _, (wu_bf,) = _mixers(z, p, mods, 1, 1, state_lru[:, l], l, [w_up_e], [],
                                                 cache_k[:, l].reshape(dec_bsz, past, D_KV).astype(BF16),
                                                 cache_v[:, l].reshape(dec_bsz, past, D_KV).astype(BF16))
        y, z = _moe(y1.reshape(bsz * seq, -1), z1.reshape(dec_bsz * dec_seq, -1), route_c, route_l,
                    mods, p["norm2_g"], wg_bf, wu_bf, wd_bf, 1 + dec_bsz)
        y, z = y.reshape(bsz, seq, D_MODEL), z.reshape(dec_bsz, dec_seq, D_MODEL)
    new_cache_k = jnp.stack(ks, axis=1)
    new_cache_v = jnp.stack(vs, axis=1)
    new_state_lru = jnp.stack(hs, axis=1)
    return y, z, new_cache_k, new_cache_v, new_state_lru
```

```python
import functools

import numpy as np
import jax
import jax.numpy as jnp
from jax import lax
from jax.experimental import pallas as pl
from jax.experimental.pallas import tpu as pltpu

D_MODEL = 2048
DEPTH = 2
GRID_W = 64
D_CONV = 512
N_HEADS = 8
N_KV_HEADS = 2
HEAD_DIM = 128
GROUP = N_HEADS // N_KV_HEADS
D_ATTN = N_HEADS * HEAD_DIM
D_KV = N_KV_HEADS * HEAD_DIM
D_LRU = 512
CONV_K = 31
LRU_CONV_K = 4
LRU_BLOCKS = 8
LRU_BLK = D_LRU // LRU_BLOCKS
LRU_C = 8.0
WINDOW = 128
ROPE_BASE = 10000.0
ATTN_SCALE = HEAD_DIM ** -0.5
N_EXPERTS = 16
N_EXPERT_GROUPS = 4
EXPERTS_PER_GROUP = N_EXPERTS // N_EXPERT_GROUPS
D_EXPERT = 512
EPS = 1e-6
NEG = -1e30
D_IN = 2 * D_CONV + D_ATTN + 2 * D_KV + 2 * D_LRU
N_MOD = 6 * D_MODEL
SUB = 8
MOD_ROWS = SUB

V7X_VMEM_BYTES = 64 * 1024 * 1024
VMEM_LIMIT = V7X_VMEM_BYTES - 8 * 1024 * 1024

F32 = jnp.float32
BF16 = jnp.bfloat16
HIGHEST = lax.Precision.HIGHEST


def _cparams(*sem):
    return pltpu.CompilerParams(dimension_semantics=sem, vmem_limit_bytes=VMEM_LIMIT)


def _sigmoid(x):
    return 0.5 * jnp.tanh(0.5 * x) + 0.5


def _silu(x):
    return x * _sigmoid(x)


def _const_spec(shape):
    return pl.BlockSpec(shape, lambda *_: (0,) * len(shape), pipeline_mode=pl.Buffered(1))


def _layer_spec(shape, layer):
    return pl.BlockSpec((1,) + shape, lambda *_: (layer,) + (0,) * len(shape), pipeline_mode=pl.Buffered(1))


def _split_bf16(x):
    hi = x.astype(BF16)
    return hi, (x - hi.astype(F32)).astype(BF16)


def _mod_kernel(c_ref, w_ref, b_ref, o_ref):
    s = _silu(c_ref[...])
    s_hi = s.astype(BF16).astype(F32)
    s_both = jnp.concatenate([s_hi, s - s_hi], axis=0).astype(BF16)
    w_hi, w_lo = _split_bf16(w_ref[0])
    by_hi = jnp.dot(s_both, w_hi, preferred_element_type=F32)
    by_lo = jnp.dot(s_both, w_lo, preferred_element_type=F32)
    o_ref[0] = by_hi[0:MOD_ROWS] + by_hi[MOD_ROWS:] + by_lo[0:MOD_ROWS] + b_ref[0]


def _modulation(cvec, w_mod, b_mod):
    tn = 1536
    return pl.pallas_call(
        _mod_kernel,
        out_shape=jax.ShapeDtypeStruct((DEPTH, MOD_ROWS, N_MOD), F32),
        grid=(DEPTH, N_MOD // tn),
        in_specs=[
            _const_spec((MOD_ROWS, D_MODEL)),
            pl.BlockSpec((1, D_MODEL, tn), lambda l, j: (l, 0, j)),
            pl.BlockSpec((1, 1, tn), lambda l, j: (l, 0, j)),
        ],
        out_specs=pl.BlockSpec((1, MOD_ROWS, tn), lambda l, j: (l, 0, j)),
        compiler_params=_cparams("parallel", "parallel"),
        name="modulation",
    )(cvec, w_mod, b_mod.reshape(DEPTH, 1, N_MOD))


def _head_norm(x, g):
    return x * lax.rsqrt(jnp.mean(x * x, axis=-1, keepdims=True) + EPS) * g


def _rope(x, cos, sin_signed):
    lane = lax.broadcasted_iota(jnp.int32, x.shape, 1)
    partner = jnp.where((lane % 64) < 32, pltpu.roll(x, 96, 1), pltpu.roll(x, 32, 1))
    return x * cos + partner * sin_signed


def _inproj_kernel(*refs, rope, n_cast):
    refs = list(refs)
    cast_dst = [refs.pop() for _ in range(n_cast)][::-1]
    z_ref, q_ref, k_ref, v_ref, lg_ref, lx_ref = refs[-6:]
    del refs[-6:]
    cast_src = [refs.pop() for _ in range(n_cast)][::-1]
    if rope:
        x_ref, mod_ref, g1_ref, w_ref, qg_ref, kg_ref, cos_ref, sin_ref = refs
    else:
        x_ref, mod_ref, g1_ref, w_ref, qg_ref, kg_ref = refs
    for src, dst in zip(cast_src, cast_dst):
        dst[0] = src[0, 0].astype(BF16)
    x = x_ref[0]
    mod = mod_ref[0]
    shift, scale = mod[0:1], mod[1:2]
    y = x * lax.rsqrt(jnp.mean(x * x, axis=-1, keepdims=True) + EPS) * g1_ref[...]
    h = (y * (1.0 + scale) + shift).astype(BF16)

    def proj(c0, c1):
        return jnp.dot(h, w_ref[0, :, c0:c1], preferred_element_type=F32)

    glu = proj(0, 2 * D_CONV)
    z_ref[0] = glu[:, :D_CONV] * _sigmoid(glu[:, D_CONV:])

    c0 = 2 * D_CONV
    q = proj(c0, c0 + D_ATTN)
    qg = qg_ref[...]
    for hd in range(N_HEADS):
        qh = _head_norm(q[:, hd * HEAD_DIM:(hd + 1) * HEAD_DIM], qg)
        if rope:
            qh = _rope(qh, cos_ref[...], sin_ref[...])
        q_ref[0, :, hd * HEAD_DIM:(hd + 1) * HEAD_DIM] = (qh * ATTN_SCALE).astype(BF16)

    c0 += D_ATTN
    kv = proj(c0, c0 + 2 * D_KV)
    kg = kg_ref[...]
    for hd in range(N_KV_HEADS):
        kh = _head_norm(kv[:, hd * HEAD_DIM:(hd + 1) * HEAD_DIM], kg)
        if rope:
            kh = _rope(kh, cos_ref[...], sin_ref[...])
        k_ref[0, :, hd * HEAD_DIM:(hd + 1) * HEAD_DIM] = kh.astype(k_ref.dtype)
    v_ref[0] = kv[:, D_KV:].astype(v_ref.dtype)

    c0 += 2 * D_KV
    l2 = proj(c0, c0 + 2 * D_LRU)
    lg_ref[0] = l2[:, :D_LRU]
    lx_ref[0] = l2[:, D_LRU:]


def _rope_tables(seq):
    n_freq = HEAD_DIM // 4
    inv = (ROPE_BASE ** (-np.arange(n_freq, dtype=np.float32) / n_freq)).astype(np.float32)
    t = np.arange(seq)
    ang_r = (t // GRID_W).astype(np.float32)[:, None] * inv[None, :]
    ang_c = (t % GRID_W).astype(np.float32)[:, None] * inv[None, :]
    cos = np.concatenate([np.cos(ang_r)] * 2 + [np.cos(ang_c)] * 2, axis=-1)
    sin = np.concatenate([-np.sin(ang_r), np.sin(ang_r), -np.sin(ang_c), np.sin(ang_c)], axis=-1)
    return jnp.asarray(cos, F32), jnp.asarray(sin, F32)


def _cast_specs(w, layer, n_inner, n_steps):
    _, n_e, rows, cols = w.shape
    parts = n_steps // n_e
    assert parts * n_e == n_steps and rows % parts == 0
    piece = lambda b, i: divmod(b * n_inner + i, parts)
    src = pl.BlockSpec((1, 1, rows // parts, cols), lambda b, i: (layer, *piece(b, i), 0))
    dst = pl.BlockSpec((1, rows // parts, cols), lambda b, i: (*piece(b, i), 0))
    return src, dst, jax.ShapeDtypeStruct((n_e, rows, cols), BF16)


def _inproj(x, mods, row0, row_step, g1, w_in_bf, qg, kg, rope, tm, layer, cast_weights):
    bsz, seq, _ = x.shape
    tok = lambda width: pl.BlockSpec((1, tm, width), lambda b, i: (b, i, 0))
    in_specs = [
        tok(D_MODEL),
        pl.BlockSpec((1, 6, D_MODEL), lambda b, i: (row0 + row_step * b, 0, 0)),
        _const_spec((1, D_MODEL)),
        _layer_spec((D_MODEL, D_IN), layer),
        _const_spec((1, HEAD_DIM)),
        _const_spec((1, HEAD_DIM)),
    ]
    args = [x, mods, g1.reshape(1, D_MODEL), w_in_bf, qg.reshape(1, HEAD_DIM), kg.reshape(1, HEAD_DIM)]
    if rope:
        cos, sin = _rope_tables(seq)
        in_specs += [pl.BlockSpec((tm, HEAD_DIM), lambda b, i: (i, 0))] * 2
        args += [cos, sin]
    widths = (D_CONV, D_ATTN, D_KV, D_KV, D_LRU, D_LRU)
    kv_dtype = BF16 if rope else F32
    dtypes = (F32, BF16, kv_dtype, kv_dtype, F32, F32)
    out_shape = [jax.ShapeDtypeStruct((bsz, seq, w), dt) for w, dt in zip(widths, dtypes)]
    out_specs = [tok(w) for w in widths]
    n_inner = seq // tm
    for w in cast_weights:
        src, dst, shape = _cast_specs(w, layer, n_inner, bsz * n_inner)
        in_specs.append(src)
        args.append(w)
        out_specs.append(dst)
        out_shape.append(shape)
    outs = pl.pallas_call(
        functools.partial(_inproj_kernel, rope=rope, n_cast=len(cast_weights)),
        out_shape=out_shape,
        grid=(bsz, n_inner),
        in_specs=in_specs,
        out_specs=out_specs,
        compiler_params=_cparams("parallel", "parallel"),
        name="inproj_rope" if rope else "inproj",
    )(*args)
    return outs[:6], outs[6:]


CONV_PAD = (CONV_K - 1) // 2
CONV_HALO = 16
CONV_CHUNK = 64
CONV_COLS = 256


def _conv_kernel(z_ref, w_ref, b_ref, g_ref, beta_ref, o_ref, zp_ref, win_ref, acc_ref, *, seq):
    zeros = jnp.zeros((CONV_HALO, D_CONV), F32)
    zp_ref[0:CONV_HALO, :] = zeros
    zp_ref[CONV_HALO + seq:2 * CONV_HALO + seq, :] = zeros
    zp_ref[CONV_HALO:CONV_HALO + seq, :] = z_ref[0]
    n_parts = D_CONV // CONV_COLS

    def body(c, carry):
        r0 = pl.multiple_of(c * CONV_CHUNK, CONV_CHUNK)
        for p in range(n_parts):
            win_ref[p] = zp_ref[pl.ds(r0, CONV_CHUNK + 2 * CONV_HALO), p * CONV_COLS:(p + 1) * CONV_COLS]

        def taps(p, inner):
            part = jnp.broadcast_to(b_ref[p], (CONV_CHUNK, CONV_COLS))
            for k in range(CONV_K):
                off = CONV_HALO - CONV_PAD + k
                part = part + jnp.tile(w_ref[p, k], (CONV_CHUNK // SUB, 1)) * win_ref[p, off:off + CONV_CHUNK, :]
            acc_ref[p] = part
            return inner

        lax.fori_loop(0, n_parts, taps, 0)
        acc = jnp.concatenate([acc_ref[p] for p in range(n_parts)], axis=1)
        mu = jnp.mean(acc, axis=-1, keepdims=True)
        xc = acc - mu
        var = jnp.mean(xc * xc, axis=-1, keepdims=True)
        y = xc * lax.rsqrt(var + EPS) * g_ref[...] + beta_ref[...]
        o_ref[0, pl.ds(r0, CONV_CHUNK), :] = _silu(y).astype(BF16)
        return carry

    lax.fori_loop(0, seq // CONV_CHUNK, body, 0)


def _conv_module(z, w, b, g, beta):
    bsz, seq, _ = z.shape
    row = lambda a: a.reshape(1, D_CONV)
    n_parts = D_CONV // CONV_COLS
    w_parts = jnp.broadcast_to(w.reshape(CONV_K, 1, n_parts, CONV_COLS), (CONV_K, SUB, n_parts, CONV_COLS))
    w_parts = w_parts.transpose(2, 0, 1, 3)
    return pl.pallas_call(
        functools.partial(_conv_kernel, seq=seq),
        out_shape=jax.ShapeDtypeStruct((bsz, seq, D_CONV), BF16),
        grid=(bsz,),
        in_specs=[
            pl.BlockSpec((1, seq, D_CONV), lambda i: (i, 0, 0)),
            _const_spec((n_parts, CONV_K, SUB, CONV_COLS)),
            _const_spec((n_parts, 1, CONV_COLS)), _const_spec((1, D_CONV)), _const_spec((1, D_CONV)),
        ],
        out_specs=pl.BlockSpec((1, seq, D_CONV), lambda i: (i, 0, 0)),
        scratch_shapes=[pltpu.VMEM((seq + 2 * CONV_HALO, D_CONV), F32),
                        pltpu.VMEM((n_parts, CONV_CHUNK + 2 * CONV_HALO, CONV_COLS), F32),
                        pltpu.VMEM((n_parts, CONV_CHUNK, CONV_COLS), F32)],
        compiler_params=_cparams("parallel"),
        name="conv_module",
    )(z, w_parts, b.reshape(n_parts, 1, CONV_COLS), row(g), row(beta))


ATTN_ROWS = 64


def _attn_kernel(*refs, latent, seq, tq):
    if latent:
        sink_ref, q_ref, k_ref, v_ref, kc_ref, vc_ref, o_ref, s_ref, p_ref, bias_ref = refs
    else:
        sink_ref, q_ref, k_ref, v_ref, o_ref, s_ref, p_ref = refs
    q = q_ref[0]
    n_loc = 3 * tq if latent else 0
    if latent:
        n = pl.program_id(1)
        nblk = seq // tq
        blocks = (jnp.maximum(n - 1, 0), n, jnp.minimum(n + 1, nblk - 1))

        def window(ref):
            parts = [ref[0, pl.ds(pl.multiple_of(i * tq, tq), tq), :] for i in blocks]
            return jnp.concatenate(parts, axis=0)

        k_all = jnp.concatenate([window(k_ref), kc_ref[0]], axis=0).astype(BF16)
        v_all = jnp.concatenate([window(v_ref), vc_ref[0]], axis=0).astype(BF16)
        qpos = n * tq + lax.broadcasted_iota(jnp.int32, (tq, 3 * tq), 0)
        kpos = (n - 1) * tq + lax.broadcasted_iota(jnp.int32, (tq, 3 * tq), 1)
        ok = (jnp.abs(qpos - kpos) <= WINDOW) & (kpos >= 0) & (kpos < seq)
        bias_ref[...] = jnp.where(ok, 0.0, NEG).astype(F32)
    else:
        k_all = k_ref[0].astype(BF16)
        v_all = v_ref[0].astype(BF16)

    for j in range(N_KV_HEADS):
        heads = [j * GROUP + g for g in range(GROUP)]
        qs = jnp.concatenate([q[:, h * HEAD_DIM:(h + 1) * HEAD_DIM] for h in heads], axis=0)
        kj = k_all[:, j * HEAD_DIM:(j + 1) * HEAD_DIM]
        vj = v_all[:, j * HEAD_DIM:(j + 1) * HEAD_DIM]
        s_ref[...] = lax.dot_general(qs, kj, (((1,), (1,)), ((), ())), preferred_element_type=F32)
        inv = []
        for rb in range(GROUP * tq // ATTN_ROWS):
            rows = slice(rb * ATTN_ROWS, (rb + 1) * ATTN_ROWS)
            sk = sink_ref[heads[rb * ATTN_ROWS // tq]]
            parts = [s_ref[rows, n_loc:]]
            if latent:
                q0 = rb * ATTN_ROWS % tq
                parts.insert(0, s_ref[rows, :n_loc] + bias_ref[q0:q0 + ATTN_ROWS, :])
            m = sk
            for s in parts:
                m = jnp.maximum(m, jnp.max(s, axis=-1, keepdims=True))
            den = jnp.exp(sk - m)
            c0 = 0
            for s in parts:
                p = jnp.exp(s - m)
                den = den + jnp.sum(p, axis=-1, keepdims=True)
                p_ref[rows, c0:c0 + s.shape[1]] = p.astype(BF16)
                c0 += s.shape[1]
            inv.append(1.0 / den)
        o = jnp.dot(p_ref[...], vj, preferred_element_type=F32) * jnp.concatenate(inv, axis=0)
        for g, h in enumerate(heads):
            o_ref[0, :, h * HEAD_DIM:(h + 1) * HEAD_DIM] = o[g * tq:(g + 1) * tq].astype(BF16)


def _attention(q, k, v, sink, k_ctx=None, v_ctx=None):
    bsz, seq, _ = q.shape
    latent = k_ctx is not None
    tq = WINDOW if latent else seq
    seq_spec = pl.BlockSpec((1, seq, D_KV), lambda b, i: (b, 0, 0))
    in_specs = [
        pl.BlockSpec(memory_space=pltpu.SMEM),
        pl.BlockSpec((1, tq, D_ATTN), lambda b, i: (b, i, 0)),
        seq_spec, seq_spec,
    ]
    args = [sink, q, k, v]
    n_keys = seq
    if latent:
        past = k_ctx.shape[1]
        ctx_spec = pl.BlockSpec((1, past, D_KV), lambda b, i: (b, 0, 0))
        in_specs += [ctx_spec, ctx_spec]
        args += [k_ctx, v_ctx]
        n_keys = 3 * tq + past
    scratch = [pltpu.VMEM((GROUP * tq, n_keys), F32), pltpu.VMEM((GROUP * tq, n_keys), BF16)]
    if latent:
        scratch.append(pltpu.VMEM((tq, 3 * tq), F32))
    return pl.pallas_call(
        functools.partial(_attn_kernel, latent=latent, seq=seq, tq=tq),
        out_shape=jax.ShapeDtypeStruct((bsz, seq, D_ATTN), BF16),
        grid=(bsz, seq // tq),
        in_specs=in_specs,
        out_specs=pl.BlockSpec((1, tq, D_ATTN), lambda b, i: (b, i, 0)),
        scratch_shapes=scratch,
        compiler_params=_cparams("parallel", "parallel"),
        name="attn_latent" if latent else "attn_context",
    )(*args)


LRU_HALO = 8
LRU_CHUNK = 128
LRU_HALF = D_LRU // 2


def _softplus(x):
    return jnp.maximum(x, 0.0) + jnp.log(1.0 + jnp.exp(-jnp.abs(x)))


def _gelu_tanh(x):
    return 0.5 * x * (1.0 + jnp.tanh(0.7978845608028654 * (x + 0.044715 * (x * x * x))))


def _scan_tile(a, b, carry, reverse):
    row = lax.broadcasted_iota(jnp.int32, a.shape, 0)
    for d in (1, 2, 4):
        if reverse:
            valid = row < SUB - d
            shift = SUB - d
        else:
            valid = row >= d
            shift = d
        a_prev = jnp.where(valid, pltpu.roll(a, shift, 0), 1.0)
        b_prev = jnp.where(valid, pltpu.roll(b, shift, 0), 0.0)
        b = a * b_prev + b
        a = a * a_prev
    h = a * carry + b
    last = h[0:1, :] if reverse else h[SUB - 1:SUB, :]
    return h, last


def _lru_kernel(lx_ref, lg_ref, h0_ref, cw_ref, cb_ref, wbd_ref, gbias_ref, lam_ref,
                o_ref, fin_ref, xp_ref, win_ref, af_ref, bf_ref, ab_ref, bb_ref, *, seq):
    zeros = jnp.zeros((LRU_HALO, D_LRU), F32)
    xp_ref[0:LRU_HALO, :] = zeros
    xp_ref[LRU_HALO + seq:2 * LRU_HALO + seq, :] = zeros
    xp_ref[LRU_HALO:LRU_HALO + seq, :] = lx_ref[0]
    a_refs = (af_ref, ab_ref)
    b_refs = (bf_ref, bb_ref)
    half_c = (-0.5 * LRU_C) * _softplus(-lam_ref[...])

    def gates(c, carry):
        r0 = pl.multiple_of(c * LRU_CHUNK, LRU_CHUNK)
        win_ref[...] = xp_ref[pl.ds(r0, LRU_CHUNK + 2 * LRU_HALO), :]
        xc = jnp.broadcast_to(cb_ref[...], (LRU_CHUNK, D_LRU))
        for k in range(LRU_CONV_K):
            off = LRU_HALO - 2 + k
            xc = xc + cw_ref[k:k + 1, :] * win_ref[off:off + LRU_CHUNK, :]
        for s in range(2):
            cols = slice(s * LRU_HALF, (s + 1) * LRU_HALF)
            xs = xc[:, cols]
            g = jnp.dot(xs.astype(BF16), wbd_ref[s], preferred_element_type=F32)
            g = g + gbias_ref[s:s + 1, :]
            for d in range(2):
                base = d * 2 * LRU_HALF
                hc = half_c[d:d + 1, cols]
                log_a = hc * jnp.tanh(g[:, base:base + LRU_HALF]) + hc
                i = 0.5 * jnp.tanh(g[:, base + LRU_HALF:base + 2 * LRU_HALF]) + 0.5
                a = jnp.exp(log_a)
                t = jnp.tanh(log_a)
                b = jnp.sqrt(-2.0 * t / (1.0 - t)) * (i * xs)
                a_refs[d][pl.ds(r0, LRU_CHUNK), cols] = a
                b_refs[d][pl.ds(r0, LRU_CHUNK), cols] = b
        return carry

    lax.fori_loop(0, seq // LRU_CHUNK, gates, 0)

    ntile = seq // SUB

    def scan(t, carry):
        cf, cb = carry
        rf = pl.multiple_of(t * SUB, SUB)
        rb = pl.multiple_of((ntile - 1 - t) * SUB, SUB)
        hf, cf = _scan_tile(af_ref[pl.ds(rf, SUB), :], bf_ref[pl.ds(rf, SUB), :], cf, False)
        hb, cb = _scan_tile(ab_ref[pl.ds(rb, SUB), :], bb_ref[pl.ds(rb, SUB), :], cb, True)
        bf_ref[pl.ds(rf, SUB), :] = hf
        bb_ref[pl.ds(rb, SUB), :] = hb
        return cf, cb

    h0 = h0_ref[0]
    cf, cb = lax.fori_loop(0, ntile, scan, (h0[0:1, :], h0[1:2, :]))
    fin_ref[0, 0:1, :] = cf
    fin_ref[0, 1:2, :] = cb

    def finish(c, carry):
        r0 = pl.multiple_of(c * LRU_CHUNK, LRU_CHUNK)
        rows = pl.ds(r0, LRU_CHUNK)
        o_ref[0, rows, :] = ((bf_ref[rows, :] + bb_ref[rows, :]) * _gelu_tanh(lg_ref[0, rows, :])).astype(BF16)
        return carry

    lax.fori_loop(0, seq // LRU_CHUNK, finish, 0)


def _lru_gate_weights(wa, ba, wx, bx):
    per_half = LRU_HALF // LRU_BLK
    eye = jnp.eye(per_half, dtype=F32)

    def dense(w):
        blocks = w.reshape(2, per_half, LRU_BLK, LRU_BLK)
        return jnp.einsum("snkj,nm->snkmj", blocks, eye).reshape(2, LRU_HALF, LRU_HALF)

    wbd = jnp.concatenate([dense(wa[0]), dense(wx[0]), dense(wa[1]), dense(wx[1])], axis=2)
    gbias = jnp.stack([ba[0], bx[0], ba[1], bx[1]]).reshape(4, 2, LRU_HALF).transpose(1, 0, 2)
    return (0.5 * wbd).astype(BF16), 0.5 * gbias.reshape(2, 4 * LRU_HALF)


def _recurrent_mixer(lx, lg, h0, cw, cb, wbd, gbias, lam):
    bsz, seq, _ = lx.shape
    seq_spec = pl.BlockSpec((1, seq, D_LRU), lambda i: (i, 0, 0))
    state_spec = pl.BlockSpec((1, 2, D_LRU), lambda i: (i, 0, 0))
    return pl.pallas_call(
        functools.partial(_lru_kernel, seq=seq),
        out_shape=[jax.ShapeDtypeStruct((bsz, seq, D_LRU), BF16),
                   jax.ShapeDtypeStruct((bsz, 2, D_LRU), F32)],
        grid=(bsz,),
        in_specs=[
            seq_spec, seq_spec, state_spec,
            _const_spec((LRU_CONV_K, D_LRU)), _const_spec((1, D_LRU)),
            _const_spec((2, LRU_HALF, 4 * LRU_HALF)), _const_spec((2, 4 * LRU_HALF)),
            _const_spec((2, D_LRU)),
        ],
        out_specs=[seq_spec, state_spec],
        scratch_shapes=[pltpu.VMEM((seq + 2 * LRU_HALO, D_LRU), F32),
                        pltpu.VMEM((LRU_CHUNK + 2 * LRU_HALO, D_LRU), F32)] + [pltpu.VMEM((seq, D_LRU), F32)] * 4,
        compiler_params=_cparams("parallel"),
        name="rglru",
    )(lx, lg, h0, cw, cb.reshape(1, D_LRU), wbd, gbias, lam)


N_PAIRS = EXPERTS_PER_GROUP * (EXPERTS_PER_GROUP - 1) // 2
N_BUCKETS = N_EXPERT_GROUPS * N_PAIRS
PAIR_LO = (0, 0, 0, 1, 1, 2)
PAIR_HI = (1, 2, 3, 2, 3, 3)
ROUTE_ROWS = SUB


def _route(scores, biased):
    rows = [biased[e:e + 1, :] for e in range(N_EXPERTS)]
    group_score = []
    for g in range(N_EXPERT_GROUPS):
        a, b, c, d = rows[4 * g:4 * g + 4]
        hi1, lo1 = jnp.maximum(a, b), jnp.minimum(a, b)
        hi2, lo2 = jnp.maximum(c, d), jnp.minimum(c, d)
        top = jnp.maximum(hi1, hi2)
        second = jnp.maximum(jnp.minimum(hi1, hi2), jnp.maximum(lo1, lo2))
        group_score.append(top + second)
    best = group_score[0]
    g_sel = jnp.zeros_like(best, dtype=jnp.int32)
    for g in range(1, N_EXPERT_GROUPS):
        better = group_score[g] > best
        g_sel = jnp.where(better, g, g_sel)
        best = jnp.where(better, group_score[g], best)
    sel = []
    for e in range(N_EXPERTS):
        g = e // EXPERTS_PER_GROUP
        rank = jnp.zeros_like(g_sel)
        for o in range(g * EXPERTS_PER_GROUP, (g + 1) * EXPERTS_PER_GROUP):
            if o == e:
                continue
            ahead = (rows[o] >= rows[e]) if o < e else (rows[o] > rows[e])
            rank = rank + ahead.astype(jnp.int32)
        sel.append(jnp.where(jnp.where(g_sel == g, rank, 2) < 2, 1, 0))
    zero = jnp.zeros_like(best)
    lo_w, hi_w = zero, zero
    lo_idx = jnp.zeros_like(g_sel)
    hi_idx = jnp.zeros_like(g_sel)
    for g in range(N_EXPERT_GROUPS):
        seen = jnp.zeros_like(g_sel)
        for j in range(EXPERTS_PER_GROUP):
            e = g * EXPERTS_PER_GROUP + j
            order = jnp.where(sel[e] == 1, seen, 2)
            lo_w = jnp.where(order == 0, scores[e:e + 1, :], lo_w)
            hi_w = jnp.where(order == 1, scores[e:e + 1, :], hi_w)
            lo_idx = jnp.where(order == 0, j, lo_idx)
            hi_idx = jnp.where(order == 1, j, hi_idx)
            seen = seen + sel[e]
    pair_base = jnp.where(lo_idx == 0, 0, jnp.where(lo_idx == 1, 3, 5))
    bucket = g_sel * N_PAIRS + pair_base + hi_idx - lo_idx - 1
    total = lo_w + hi_w
    pad = jnp.zeros((ROUTE_ROWS - 3, best.shape[1]), F32)
    return jnp.concatenate([bucket.astype(F32), lo_w / total, hi_w / total, pad], axis=0)


def _modulated_norm(x, g, shift, scale):
    y = x * lax.rsqrt(jnp.mean(x * x, axis=-1, keepdims=True) + EPS) * g
    return y * (1.0 + scale) + shift


META_LANES = 128
ROW_LANES = 2 * D_MODEL + META_LANES
META_W_LO, META_W_HI, META_MOD_ROW = 1, 2, 3
OUTPROJ_PARTS = 1


def _outproj_kernel(*refs, row0, row_step, n_cast):
    conv_ref, attn_ref, lru_ref, x_ref, mod_ref, g2_ref, w_ref, wr_ref, br_ref = refs[:9]
    cast_src = refs[9:9 + n_cast]
    x1_ref, route_ref = refs[9 + n_cast:11 + n_cast]
    cast_dst = refs[11 + n_cast:11 + 2 * n_cast]
    cat_ref = refs[-1]
    for src, dst in zip(cast_src, cast_dst):
        dst[0] = src[0, 0].astype(BF16)
    cat_ref[:, 0:D_CONV] = conv_ref[0]
    cat_ref[:, D_CONV:D_CONV + D_ATTN] = attn_ref[0]
    cat_ref[:, D_CONV + D_ATTN:] = lru_ref[0]
    mod = mod_ref[0]
    tm = cat_ref.shape[0]
    part = tm // OUTPROJ_PARTS
    for p in range(OUTPROJ_PARTS):
        rows = slice(p * part, (p + 1) * part)
        mix = jnp.dot(cat_ref[rows, :], w_ref[0], preferred_element_type=F32)
        x1 = x_ref[0, rows, :] + mod[2:3] * mix
        x1_ref[0, rows, 0:D_MODEL] = x1
        h2 = _modulated_norm(x1, g2_ref[...], mod[3:4], mod[4:5])
        x1_ref[0, rows, D_MODEL:2 * D_MODEL] = h2
        h_hi, h_lo = _split_bf16(h2)
        nt = (((1,), (1,)), ((), ()))
        by_hi = lax.dot_general(wr_ref[...], h_hi, nt, preferred_element_type=F32)
        by_lo = lax.dot_general(wr_ref[0:N_EXPERTS, :], h_lo, nt, preferred_element_type=F32)
        logits = by_hi[0:N_EXPERTS] + by_hi[N_EXPERTS:] + by_lo
        scores = _sigmoid(logits)
        route = _route(scores, scores + br_ref[...])
        route_ref[0, :, rows] = route
        mod_row = jnp.full((1, part), row0, jnp.int32) + row_step * pl.program_id(0)
        meta = jnp.concatenate([route[0:META_MOD_ROW], mod_row.astype(F32),
                                jnp.zeros((META_LANES - META_MOD_ROW - 1, part), F32)], axis=0)
        x1_ref[0, rows, 2 * D_MODEL:] = meta.T


def _outproj(conv, attn, lru, x, mods, row0, row_step, g2, w_out_bf, w_router_t, b_router, tm, layer,
             cast_weights):
    bsz, seq, _ = x.shape
    tok = lambda width: pl.BlockSpec((1, tm, width), lambda b, i: (b, i, 0))
    in_specs = [
        tok(D_CONV), tok(D_ATTN), tok(D_LRU), tok(D_MODEL),
        pl.BlockSpec((1, 6, D_MODEL), lambda b, i: (row0 + row_step * b, 0, 0)),
        _const_spec((1, D_MODEL)),
        _layer_spec((D_MODEL, D_MODEL), layer),
        _const_spec((2 * N_EXPERTS, D_MODEL)),
        _const_spec((N_EXPERTS, 1)),
    ]
    args = [conv, attn, lru, x, mods, g2.reshape(1, D_MODEL), w_out_bf, w_router_t, b_router.reshape(N_EXPERTS, 1)]
    out_shape = [jax.ShapeDtypeStruct((bsz, seq, ROW_LANES), F32),
                 jax.ShapeDtypeStruct((bsz, ROUTE_ROWS, seq), F32)]
    out_specs = [tok(ROW_LANES), pl.BlockSpec((1, ROUTE_ROWS, tm), lambda b, i: (b, 0, i))]
    n_inner = seq // tm
    for w in cast_weights:
        src, dst, shape = _cast_specs(w, layer, n_inner, bsz * n_inner)
        in_specs.append(src)
        args.append(w)
        out_specs.append(dst)
        out_shape.append(shape)
    outs = pl.pallas_call(
        functools.partial(_outproj_kernel, row0=row0, row_step=row_step, n_cast=len(cast_weights)),
        out_shape=out_shape,
        grid=(bsz, n_inner),
        in_specs=in_specs,
        out_specs=out_specs,
        scratch_shapes=[pltpu.VMEM((tm, D_MODEL), BF16)],
        compiler_params=_cparams("parallel", "parallel"),
        name="outproj_router",
    )(*args)
    return outs[0], outs[1], outs[2:]


MOE_TILE = 256
MOE_UNROLL = 8
MOE_PARTS = 1


def _moe_plan(bucket, t_ctx, tm):
    t_all = bucket.shape[0]
    n_tiles = (t_all + N_BUCKETS * (tm - 1) + tm - 1) // tm
    ids = jnp.arange(N_BUCKETS, dtype=jnp.int32)
    order = jnp.argsort(bucket, stable=True).astype(jnp.int32)
    member = (bucket[None, :] == ids[:, None]).astype(jnp.int32)
    counts = jnp.sum(member, axis=1)
    counts_ctx = jnp.sum(member[:, :t_ctx], axis=1)
    tiles_per = (counts + tm - 1) // tm
    tile_end = jnp.cumsum(tiles_per)
    tile_start = tile_end - tiles_per
    sorted_start = jnp.cumsum(counts) - counts
    n_used = tile_end[-1]
    j = jnp.arange(n_tiles, dtype=jnp.int32)
    jj = jnp.minimum(j, n_used - 1)
    b_of = jnp.sum((tile_end[None, :] <= jj[:, None]).astype(jnp.int32), axis=1)
    local = jj - tile_start[b_of]
    n_valid = jnp.where(j < n_used, jnp.clip(counts[b_of] - local * tm, 0, tm), 0)
    n_ctx = jnp.clip(counts_ctx[b_of] - local * tm, 0, n_valid)
    first = jnp.where(j < n_used, sorted_start[b_of] + local * tm, 0)
    group, pair = b_of // N_PAIRS, b_of % N_PAIRS
    e_lo = group * EXPERTS_PER_GROUP + jnp.asarray(PAIR_LO, jnp.int32)[pair]
    e_hi = group * EXPERTS_PER_GROUP + jnp.asarray(PAIR_HI, jnp.int32)[pair]
    i32 = lambda a: a.astype(jnp.int32)
    return order, i32(first), i32(e_lo), i32(e_hi), i32(n_valid), i32(n_ctx)


def _moe_kernel(order_ref, first_ref, elo_ref, ehi_ref, nv_ref, nc_ref,
                xc_ref, xl_ref, mods_ref,
                wg_lo, wu_lo, wd_lo, wg_hi, wu_hi, wd_hi,
                oc_ref, ol_ref, xbuf, ybuf, gsem, ssem, *, tm, t_ctx, n_mod_rows):
    i = pl.program_id(0)
    n = pl.num_programs(0)
    slot = lax.rem(i, 2)

    def gather(hbm, t, r, s, size):
        return pltpu.make_async_copy(hbm.at[pl.ds(t, size), :], xbuf.at[s, pl.ds(r, size), :], gsem.at[s])

    def scatter(hbm, t, r, s, size):
        return pltpu.make_async_copy(ybuf.at[s, pl.ds(r, size), :], hbm.at[pl.ds(t, size), :], ssem.at[s])

    def start_rows(j, s, copy, hbm_ctx, hbm_lat):
        base = first_ref[j]

        def ctx_row(r):
            copy(hbm_ctx, order_ref[base + r], r, s, 1).start()

        def lat_row(r):
            copy(hbm_lat, order_ref[base + r] - t_ctx, r, s, 1).start()

        def rows(lo, hi, one_row):
            groups = lax.shift_right_logical(hi - lo, MOE_UNROLL.bit_length() - 1)

            def group(g, c):
                for u in range(MOE_UNROLL):
                    one_row(lo + g * MOE_UNROLL + u)
                return c

            def single(r, c):
                one_row(r)
                return c

            lax.fori_loop(0, groups, group, 0)
            lax.fori_loop(lo + groups * MOE_UNROLL, hi, single, 0)

        rows(0, nc_ref[j], ctx_row)
        rows(nc_ref[j], nv_ref[j], lat_row)

    def wait_rows(j, s, copy, hbm):
        cnt = nv_ref[j]
        bulk = pl.multiple_of(lax.shift_left(lax.shift_right_logical(cnt, 3), 3), SUB)

        @pl.when(bulk > 0)
        def _():
            copy(hbm, 0, 0, s, bulk).wait()

        def one(r, c):
            copy(hbm, 0, 0, s, 1).wait()
            return c

        lax.fori_loop(bulk, cnt, one, 0)

    @pl.when(i == 0)
    def _():
        xbuf[...] = jnp.zeros_like(xbuf)
        start_rows(0, 0, gather, xc_ref, xl_ref)

    @pl.when(i + 1 < n)
    def _():
        start_rows(i + 1, 1 - slot, gather, xc_ref, xl_ref)

    wait_rows(i, slot, gather, xl_ref)

    @pl.when(i >= 2)
    def _():
        wait_rows(i - 2, slot, scatter, ol_ref)

    def tile_part(rows):
        x = xbuf[slot, rows, 0:D_MODEL]
        h = xbuf[slot, rows, D_MODEL:2 * D_MODEL].astype(BF16)
        meta = xbuf[slot, rows, 2 * D_MODEL:]
        w_lo = meta[:, META_W_LO:META_W_LO + 1]
        w_hi = meta[:, META_W_HI:META_W_HI + 1]
        mrow = meta[:, META_MOD_ROW:META_MOD_ROW + 1]
        n_rows = x.shape[0]

        def per_row(*ks):
            ids = jnp.broadcast_to(mrow, (n_rows, 128))
            masks = [ids == float(r) for r in range(1, n_mod_rows)]
            cols = [[] for _ in ks]
            for c in range(D_MODEL // 128):
                lanes = slice(c * 128, (c + 1) * 128)
                for q, k in enumerate(ks):
                    v = jnp.broadcast_to(mods_ref[0, k:k + 1, lanes], (n_rows, 128))
                    for r, mask in enumerate(masks, 1):
                        v = jnp.where(mask, mods_ref[r, k:k + 1, lanes], v)
                    cols[q].append(v)
            return [jnp.concatenate(col, axis=1) for col in cols]

        def expert(wg, wu, w):
            act = _silu(jnp.dot(h, wg[0], preferred_element_type=F32))
            act = act * jnp.dot(h, wu[0], preferred_element_type=F32) * w
            return act.astype(BF16)

        y = jnp.dot(expert(wg_lo, wu_lo, w_lo), wd_lo[0], preferred_element_type=F32)
        y = y + jnp.dot(expert(wg_hi, wu_hi, w_hi), wd_hi[0], preferred_element_type=F32)
        (gate,) = per_row(5)
        ybuf[slot, rows, :] = x + gate * y

    @pl.when(nv_ref[i] > 0)
    def _():
        part = tm // MOE_PARTS
        for p in range(MOE_PARTS):
            tile_part(slice(p * part, (p + 1) * part))

    start_rows(i, slot, scatter, oc_ref, ol_ref)

    @pl.when(i == n - 1)
    def _():
        wait_rows(i, slot, scatter, ol_ref)

        @pl.when(i >= 1)
        def _():
            wait_rows(i - 1, 1 - slot, scatter, ol_ref)


def _moe(x1c, x1l, route_c, route_l, mods, wg, wu, wd, n_mod_rows):
    t_ctx, t_lat = x1c.shape[0], x1l.shape[0]
    tm = MOE_TILE
    bucket = jnp.concatenate([route_c[:, 0, :].reshape(t_ctx), route_l[:, 0, :].reshape(t_lat)])
    order, first, e_lo, e_hi, n_valid, n_ctx = _moe_plan(bucket.astype(jnp.int32), t_ctx, tm)
    n_tiles = n_valid.shape[0]
    w_up = lambda sel: pl.BlockSpec((1, D_MODEL, D_EXPERT),
                                    lambda i, order, first, lo, hi, nv, nc: ((lo, hi)[sel][i], 0, 0))
    w_dn = lambda sel: pl.BlockSpec((1, D_EXPERT, D_MODEL),
                                    lambda i, order, first, lo, hi, nv, nc: ((lo, hi)[sel][i], 0, 0))
    hbm = pl.BlockSpec(memory_space=pl.ANY)
    return pl.pallas_call(
        functools.partial(_moe_kernel, tm=tm, t_ctx=t_ctx, n_mod_rows=n_mod_rows),
        out_shape=[jax.ShapeDtypeStruct((t_ctx, D_MODEL), F32), jax.ShapeDtypeStruct((t_lat, D_MODEL), F32)],
        grid_spec=pltpu.PrefetchScalarGridSpec(
            num_scalar_prefetch=6,
            grid=(n_tiles,),
            in_specs=[
                hbm, hbm,
                _const_spec((MOD_ROWS, 6, D_MODEL)),
                w_up(0), w_up(0), w_dn(0), w_up(1), w_up(1), w_dn(1),
            ],
            out_specs=[hbm, hbm],
            scratch_shapes=[
                pltpu.VMEM((2, tm, ROW_LANES), F32), pltpu.VMEM((2, tm, D_MODEL), F32),
                pltpu.SemaphoreType.DMA((2,)), pltpu.SemaphoreType.DMA((2,)),
            ],
        ),
        compiler_params=_cparams("arbitrary"),
        name="moe_pairs",
    )(order, first, e_lo, e_hi, n_valid, n_ctx, x1c, x1l, mods, wg, wu, wd, wg, wu, wd)


def _mixers(x, p, mods, row0, row_step, h0, layer, casts_in, casts_out, k_ctx=None, v_ctx=None):
    latent = k_ctx is not None
    bsz, seq, _ = x.shape
    tm = min(seq, 512)
    (z, q, k, v, lg, lx), cast_a = _inproj(x, mods, row0, row_step, p["norm1_g"], p["w_in"], p["q_norm_g"],
                                           p["k_norm_g"], latent, tm, layer, casts_in)
    conv = _conv_module(z, p["conv_dw"], p["conv_b"], p["conv_ln_g"], p["conv_ln_b"])
    attn = _attention(q, k, v, p["attn_sink"], k_ctx, v_ctx)
    lru, fin = _recurrent_mixer(lx, lg, h0, p["lru_conv_w"], p["lru_conv_b"], p["lru_wbd"], p["lru_gbias"],
                                p["lru_lam"])
    x1, route, cast_b = _outproj(conv, attn, lru, x, mods, row0, row_step, p["norm2_g"], p["w_out"],
                                 p["w_router_t"], p["b_router"], tm, layer, casts_out)
    return x1, route, k, v, fin, list(cast_a) + list(cast_b)


def kernel(x_prompt, x_sample, c, cache_k, cache_v, state_lru, c_ctx, w_mod, b_mod, norm1_g, norm2_g, w_in,
           conv_dw, conv_b, conv_ln_g, conv_ln_b, q_norm_g, k_norm_g, attn_sink, lru_conv_w, lru_conv_b,
           lru_wa, lru_ba, lru_wx, lru_bx, lru_lam, w_out, w_router, b_router, w_gate_e, w_up_e, w_down_e):
    bsz, seq, _ = x_prompt.shape
    dec_bsz, dec_seq, _ = x_sample.shape
    past = cache_k.shape[2]

    cvec = jnp.zeros((MOD_ROWS, D_MODEL), F32).at[0].set(c_ctx).at[1:1 + dec_bsz].set(c)
    mods_all = _modulation(cvec, w_mod, b_mod).reshape(DEPTH, MOD_ROWS, 6, D_MODEL)

    wr_hi = w_router.T.astype(BF16)
    wr_lo = (w_router.T - wr_hi.astype(F32)).astype(BF16)
    w_router_t = jnp.concatenate([wr_hi, wr_lo], axis=0)
    w_in_bf, w_out_bf = w_in.astype(BF16), w_out.astype(BF16)
    layers = []
    for l in range(DEPTH):
        wbd, gbias = _lru_gate_weights(lru_wa[l], lru_ba[l], lru_wx[l], lru_bx[l])
        layers.append({
            "norm1_g": norm1_g[l], "norm2_g": norm2_g[l], "w_in": w_in_bf,
            "conv_dw": conv_dw[l], "conv_b": conv_b[l], "conv_ln_g": conv_ln_g[l], "conv_ln_b": conv_ln_b[l],
            "q_norm_g": q_norm_g[l], "k_norm_g": k_norm_g[l], "attn_sink": attn_sink[l],
            "lru_conv_w": lru_conv_w[l], "lru_conv_b": lru_conv_b[l], "lru_wbd": wbd, "lru_gbias": gbias,
            "lru_lam": lru_lam[l],
            "w_out": w_out_bf, "w_router_t": w_router_t, "b_router": b_router,
        })

    y, z = x_prompt, x_sample
    ks, vs, hs = [], [], []
    h0_ctx = jnp.zeros((bsz, 2, D_LRU), F32)
    for l in range(DEPTH):
        p, mods = layers[l], mods_all[l]
        y1, route_c, k_l, v_l, h_l, (wg_bf, wd_bf) = _mixers(y, p, mods, 0, 0, h0_ctx, l, [w_gate_e], [w_down_e])
        ks.append(k_l.reshape(bsz, seq, N_KV_HEADS, HEAD_DIM))
        vs.append(v_l.reshape(bsz, seq, N_KV_HEADS, HEAD_DIM))
        hs.append(h_l)
        z1, route_l, _, _, _, (wu_bf,) = _mixers(z, p, mods, 1, 1, state_lru[:, l], l, [w_up_e], [],
                                                 cache_k[:, l].reshape(dec_bsz, past, D_KV).astype(BF16),
                                                 cache_v[:, l].reshape(dec_bsz, past, D_KV).astype(BF16))
        y, z = _moe(y1.reshape(bsz * seq, -1), z1.reshape(dec_bsz * dec_seq, -1), route_c, route_l,
                    mods, wg_bf, wu_bf, wd_bf, 1 + dec_bsz)
        y, z = y.reshape(bsz, seq, D_MODEL), z.reshape(dec_bsz, dec_seq, D_MODEL)
    new_cache_k = jnp.stack(ks, axis=1)
    new_cache_v = jnp.stack(vs, axis=1)
    new_state_lru = jnp.stack(hs, axis=1)
    return y, z, new_cache_k, new_cache_v, new_state_lru
```

```python
import functools

import numpy as np
import jax
import jax.numpy as jnp
from jax import lax
from jax.experimental import pallas as pl
from jax.experimental.pallas import tpu as pltpu

D_MODEL = 2048
DEPTH = 2
GRID_W = 64
D_CONV = 512
N_HEADS = 8
N_KV_HEADS = 2
HEAD_DIM = 128
GROUP = N_HEADS // N_KV_HEADS
D_ATTN = N_HEADS * HEAD_DIM
D_KV = N_KV_HEADS * HEAD_DIM
D_LRU = 512
CONV_K = 31
LRU_CONV_K = 4
LRU_BLOCKS = 8
LRU_BLK = D_LRU // LRU_BLOCKS
LRU_C = 8.0
WINDOW = 128
ROPE_BASE = 10000.0
ATTN_SCALE = HEAD_DIM ** -0.5
N_EXPERTS = 16
N_EXPERT_GROUPS = 4
EXPERTS_PER_GROUP = N_EXPERTS // N_EXPERT_GROUPS
D_EXPERT = 512
EPS = 1e-6
NEG = -1e30
D_IN = 2 * D_CONV + D_ATTN + 2 * D_KV + 2 * D_LRU
N_MOD = 6 * D_MODEL
SUB = 8
MOD_ROWS = SUB

V7X_VMEM_BYTES = 64 * 1024 * 1024
VMEM_LIMIT = V7X_VMEM_BYTES - 8 * 1024 * 1024

F32 = jnp.float32
BF16 = jnp.bfloat16
HIGHEST = lax.Precision.HIGHEST


def _cparams(*sem):
    return pltpu.CompilerParams(dimension_semantics=sem, vmem_limit_bytes=VMEM_LIMIT)


def _sigmoid(x):
    return 0.5 * jnp.tanh(0.5 * x) + 0.5


def _silu(x):
    return x * _sigmoid(x)


def _const_spec(shape):
    return pl.BlockSpec(shape, lambda *_: (0,) * len(shape), pipeline_mode=pl.Buffered(1))


def _layer_spec(shape, layer):
    return pl.BlockSpec((1,) + shape, lambda *_: (layer,) + (0,) * len(shape), pipeline_mode=pl.Buffered(1))


def _split_bf16(x):
    hi = x.astype(BF16)
    return hi, (x - hi.astype(F32)).astype(BF16)


def _mod_kernel(c_ref, w_ref, b_ref, o_ref):
    s = _silu(c_ref[...])
    s_hi = s.astype(BF16).astype(F32)
    s_both = jnp.concatenate([s_hi, s - s_hi], axis=0).astype(BF16)
    w_hi, w_lo = _split_bf16(w_ref[0])
    by_hi = jnp.dot(s_both, w_hi, preferred_element_type=F32)
    by_lo = jnp.dot(s_both, w_lo, preferred_element_type=F32)
    o_ref[0] = by_hi[0:MOD_ROWS] + by_hi[MOD_ROWS:] + by_lo[0:MOD_ROWS] + b_ref[0]


def _modulation(cvec, w_mod, b_mod):
    tn = 1536
    return pl.pallas_call(
        _mod_kernel,
        out_shape=jax.ShapeDtypeStruct((DEPTH, MOD_ROWS, N_MOD), F32),
        grid=(DEPTH, N_MOD // tn),
        in_specs=[
            _const_spec((MOD_ROWS, D_MODEL)),
            pl.BlockSpec((1, D_MODEL, tn), lambda l, j: (l, 0, j)),
            pl.BlockSpec((1, 1, tn), lambda l, j: (l, 0, j)),
        ],
        out_specs=pl.BlockSpec((1, MOD_ROWS, tn), lambda l, j: (l, 0, j)),
        compiler_params=_cparams("parallel", "parallel"),
        name="modulation",
    )(cvec, w_mod, b_mod.reshape(DEPTH, 1, N_MOD))


def _head_norm(x, g):
    return x * lax.rsqrt(jnp.mean(x * x, axis=-1, keepdims=True) + EPS) * g


def _rope(x, cos, sin_signed):
    lane = lax.broadcasted_iota(jnp.int32, x.shape, 1)
    partner = jnp.where((lane % 64) < 32, pltpu.roll(x, 96, 1), pltpu.roll(x, 32, 1))
    return x * cos + partner * sin_signed


def _inproj_kernel(*refs, rope, n_cast):
    refs = list(refs)
    cast_dst = [refs.pop() for _ in range(n_cast)][::-1]
    z_ref, q_ref, k_ref, v_ref, lg_ref, lx_ref = refs[-6:]
    del refs[-6:]
    cast_src = [refs.pop() for _ in range(n_cast)][::-1]
    if rope:
        x_ref, mod_ref, g1_ref, w_ref, qg_ref, kg_ref, cos_ref, sin_ref = refs
    else:
        x_ref, mod_ref, g1_ref, w_ref, qg_ref, kg_ref = refs
    for src, dst in zip(cast_src, cast_dst):
        dst[0] = src[0, 0].astype(BF16)
    x = x_ref[0]
    mod = mod_ref[0]
    shift, scale = mod[0:1], mod[1:2]
    y = x * lax.rsqrt(jnp.mean(x * x, axis=-1, keepdims=True) + EPS) * g1_ref[...]
    h = (y * (1.0 + scale) + shift).astype(BF16)

    def proj(c0, c1):
        return jnp.dot(h, w_ref[0, :, c0:c1], preferred_element_type=F32)

    glu = proj(0, 2 * D_CONV)
    z_ref[0] = glu[:, :D_CONV] * _sigmoid(glu[:, D_CONV:])

    c0 = 2 * D_CONV
    q = proj(c0, c0 + D_ATTN)
    qg = qg_ref[...]
    for hd in range(N_HEADS):
        qh = _head_norm(q[:, hd * HEAD_DIM:(hd + 1) * HEAD_DIM], qg)
        if rope:
            qh = _rope(qh, cos_ref[...], sin_ref[...])
        q_ref[0, :, hd * HEAD_DIM:(hd + 1) * HEAD_DIM] = (qh * ATTN_SCALE).astype(BF16)

    c0 += D_ATTN
    kv = proj(c0, c0 + 2 * D_KV)
    kg = kg_ref[...]
    for hd in range(N_KV_HEADS):
        kh = _head_norm(kv[:, hd * HEAD_DIM:(hd + 1) * HEAD_DIM], kg)
        if rope:
            kh = _rope(kh, cos_ref[...], sin_ref[...])
        k_ref[0, :, hd * HEAD_DIM:(hd + 1) * HEAD_DIM] = kh.astype(k_ref.dtype)
    v_ref[0] = kv[:, D_KV:].astype(v_ref.dtype)

    c0 += 2 * D_KV
    l2 = proj(c0, c0 + 2 * D_LRU)
    lg_ref[0] = l2[:, :D_LRU]
    lx_ref[0] = l2[:, D_LRU:]


def _rope_tables(seq):
    n_freq = HEAD_DIM // 4
    inv = (ROPE_BASE ** (-np.arange(n_freq, dtype=np.float32) / n_freq)).astype(np.float32)
    t = np.arange(seq)
    ang_r = (t // GRID_W).astype(np.float32)[:, None] * inv[None, :]
    ang_c = (t % GRID_W).astype(np.float32)[:, None] * inv[None, :]
    cos = np.concatenate([np.cos(ang_r)] * 2 + [np.cos(ang_c)] * 2, axis=-1)
    sin = np.concatenate([-np.sin(ang_r), np.sin(ang_r), -np.sin(ang_c), np.sin(ang_c)], axis=-1)
    return jnp.asarray(cos, F32), jnp.asarray(sin, F32)


def _cast_specs(w, layer, n_inner, n_steps):
    _, n_e, rows, cols = w.shape
    parts = n_steps // n_e
    assert parts * n_e == n_steps and rows % parts == 0
    piece = lambda b, i: divmod(b * n_inner + i, parts)
    src = pl.BlockSpec((1, 1, rows // parts, cols), lambda b, i: (layer, *piece(b, i), 0))
    dst = pl.BlockSpec((1, rows // parts, cols), lambda b, i: (*piece(b, i), 0))
    return src, dst, jax.ShapeDtypeStruct((n_e, rows, cols), BF16)


def _inproj(x, mods, row0, row_step, g1, w_in_bf, qg, kg, rope, tm, layer, cast_weights):
    bsz, seq, _ = x.shape
    tok = lambda width: pl.BlockSpec((1, tm, width), lambda b, i: (b, i, 0))
    in_specs = [
        tok(D_MODEL),
        pl.BlockSpec((1, 6, D_MODEL), lambda b, i: (row0 + row_step * b, 0, 0)),
        _const_spec((1, D_MODEL)),
        _layer_spec((D_MODEL, D_IN), layer),
        _const_spec((1, HEAD_DIM)),
        _const_spec((1, HEAD_DIM)),
    ]
    args = [x, mods, g1.reshape(1, D_MODEL), w_in_bf, qg.reshape(1, HEAD_DIM), kg.reshape(1, HEAD_DIM)]
    if rope:
        cos, sin = _rope_tables(seq)
        in_specs += [pl.BlockSpec((tm, HEAD_DIM), lambda b, i: (i, 0))] * 2
        args += [cos, sin]
    widths = (D_CONV, D_ATTN, D_KV, D_KV, D_LRU, D_LRU)
    kv_dtype = BF16 if rope else F32
    dtypes = (F32, BF16, kv_dtype, kv_dtype, F32, F32)
    out_shape = [jax.ShapeDtypeStruct((bsz, seq, w), dt) for w, dt in zip(widths, dtypes)]
    out_specs = [tok(w) for w in widths]
    n_inner = seq // tm
    for w in cast_weights:
        src, dst, shape = _cast_specs(w, layer, n_inner, bsz * n_inner)
        in_specs.append(src)
        args.append(w)
        out_specs.append(dst)
        out_shape.append(shape)
    outs = pl.pallas_call(
        functools.partial(_inproj_kernel, rope=rope, n_cast=len(cast_weights)),
        out_shape=out_shape,
        grid=(bsz, n_inner),
        in_specs=in_specs,
        out_specs=out_specs,
        compiler_params=_cparams("parallel", "parallel"),
        name="inproj_rope" if rope else "inproj",
    )(*args)
    return outs[:6], outs[6:]


CONV_PAD = (CONV_K - 1) // 2
CONV_HALO = 16
CONV_CHUNK = 64
CONV_COLS = 256
CONV_GROUP = 2


def _conv_kernel(z_ref, w_ref, b_ref, g_ref, beta_ref, o_ref, zp_ref, win_ref, acc_ref, *, seq):
    zeros = jnp.zeros((CONV_HALO, D_CONV), F32)
    zp_ref[0:CONV_HALO, :] = zeros
    zp_ref[CONV_HALO + seq:2 * CONV_HALO + seq, :] = zeros
    zp_ref[CONV_HALO:CONV_HALO + seq, :] = z_ref[0]
    n_parts = D_CONV // CONV_COLS

    group_rows = CONV_GROUP * CONV_CHUNK

    def body(c, carry):
        r0 = pl.multiple_of(c * group_rows, group_rows)
        for q in range(CONV_GROUP):
            for p in range(n_parts):
                win_ref[q * n_parts + p] = zp_ref[pl.ds(r0 + q * CONV_CHUNK, CONV_CHUNK + 2 * CONV_HALO),
                                                  p * CONV_COLS:(p + 1) * CONV_COLS]

        def taps(j, inner):
            p = lax.rem(j, n_parts)
            part = jnp.broadcast_to(b_ref[p], (CONV_CHUNK, CONV_COLS))
            for k in range(CONV_K):
                off = CONV_HALO - CONV_PAD + k
                part = part + jnp.tile(w_ref[p, k], (CONV_CHUNK // SUB, 1)) * win_ref[j, off:off + CONV_CHUNK, :]
            acc_ref[j] = part
            return inner

        lax.fori_loop(0, CONV_GROUP * n_parts, taps, 0)
        acc = jnp.concatenate(
            [jnp.concatenate([acc_ref[q * n_parts + p] for p in range(n_parts)], axis=1) for q in range(CONV_GROUP)],
            axis=0)
        mu = jnp.mean(acc, axis=-1, keepdims=True)
        xc = acc - mu
        var = jnp.mean(xc * xc, axis=-1, keepdims=True)
        y = xc * lax.rsqrt(var + EPS) * g_ref[...] + beta_ref[...]
        o_ref[0, pl.ds(r0, group_rows), :] = _silu(y).astype(BF16)
        return carry

    lax.fori_loop(0, seq // group_rows, body, 0)


def _conv_module(z, w, b, g, beta):
    bsz, seq, _ = z.shape
    row = lambda a: a.reshape(1, D_CONV)
    n_parts = D_CONV // CONV_COLS
    w_parts = jnp.broadcast_to(w.reshape(CONV_K, 1, n_parts, CONV_COLS), (CONV_K, SUB, n_parts, CONV_COLS))
    w_parts = w_parts.transpose(2, 0, 1, 3)
    return pl.pallas_call(
        functools.partial(_conv_kernel, seq=seq),
        out_shape=jax.ShapeDtypeStruct((bsz, seq, D_CONV), BF16),
        grid=(bsz,),
        in_specs=[
            pl.BlockSpec((1, seq, D_CONV), lambda i: (i, 0, 0)),
            _const_spec((n_parts, CONV_K, SUB, CONV_COLS)),
            _const_spec((n_parts, 1, CONV_COLS)), _const_spec((1, D_CONV)), _const_spec((1, D_CONV)),
        ],
        out_specs=pl.BlockSpec((1, seq, D_CONV), lambda i: (i, 0, 0)),
        scratch_shapes=[pltpu.VMEM((seq + 2 * CONV_HALO, D_CONV), F32),
                        pltpu.VMEM((CONV_GROUP * n_parts, CONV_CHUNK + 2 * CONV_HALO, CONV_COLS), F32),
                        pltpu.VMEM((CONV_GROUP * n_parts, CONV_CHUNK, CONV_COLS), F32)],
        compiler_params=_cparams("parallel"),
        name="conv_module",
    )(z, w_parts, b.reshape(n_parts, 1, CONV_COLS), row(g), row(beta))


ATTN_ROWS = 64


def _attn_kernel(*refs, latent, seq, tq, n_cast):
    n_in = 6 if latent else 4
    cast_src = refs[n_in:n_in + n_cast]
    cast_dst = refs[n_in + n_cast + 1:n_in + 2 * n_cast + 1]
    refs = refs[:n_in] + refs[n_in + n_cast:n_in + n_cast + 1] + refs[n_in + 2 * n_cast + 1:]
    if latent:
        sink_ref, q_ref, k_ref, v_ref, kc_ref, vc_ref, o_ref, s_ref, p_ref, bias_ref = refs
    else:
        sink_ref, q_ref, k_ref, v_ref, o_ref, s_ref, p_ref = refs
    for src, dst in zip(cast_src, cast_dst):
        dst[0] = src[0, 0].astype(BF16)
    q = q_ref[0]
    n_loc = 3 * tq if latent else 0
    if latent:
        n = pl.program_id(1)
        nblk = seq // tq
        blocks = (jnp.maximum(n - 1, 0), n, jnp.minimum(n + 1, nblk - 1))

        def window(ref):
            parts = [ref[0, pl.ds(pl.multiple_of(i * tq, tq), tq), :] for i in blocks]
            return jnp.concatenate(parts, axis=0)

        k_all = jnp.concatenate([window(k_ref), kc_ref[0]], axis=0).astype(BF16)
        v_all = jnp.concatenate([window(v_ref), vc_ref[0]], axis=0).astype(BF16)
        qpos = n * tq + lax.broadcasted_iota(jnp.int32, (tq, 3 * tq), 0)
        kpos = (n - 1) * tq + lax.broadcasted_iota(jnp.int32, (tq, 3 * tq), 1)
        ok = (jnp.abs(qpos - kpos) <= WINDOW) & (kpos >= 0) & (kpos < seq)
        bias_ref[...] = jnp.where(ok, 0.0, NEG).astype(F32)
    else:
        k_all = k_ref[0].astype(BF16)
        v_all = v_ref[0].astype(BF16)

    for j in range(N_KV_HEADS):
        heads = [j * GROUP + g for g in range(GROUP)]
        qs = jnp.concatenate([q[:, h * HEAD_DIM:(h + 1) * HEAD_DIM] for h in heads], axis=0)
        kj = k_all[:, j * HEAD_DIM:(j + 1) * HEAD_DIM]
        vj = v_all[:, j * HEAD_DIM:(j + 1) * HEAD_DIM]
        s_ref[...] = lax.dot_general(qs, kj, (((1,), (1,)), ((), ())), preferred_element_type=F32)
        inv = []
        for rb in range(GROUP * tq // ATTN_ROWS):
            rows = slice(rb * ATTN_ROWS, (rb + 1) * ATTN_ROWS)
            sk = sink_ref[heads[rb * ATTN_ROWS // tq]]
            parts = [s_ref[rows, n_loc:]]
            if latent:
                q0 = rb * ATTN_ROWS % tq
                parts.insert(0, s_ref[rows, :n_loc] + bias_ref[q0:q0 + ATTN_ROWS, :])
            m = sk
            for s in parts:
                m = jnp.maximum(m, jnp.max(s, axis=-1, keepdims=True))
            den = jnp.exp(sk - m)
            c0 = 0
            for s in parts:
                p = jnp.exp(s - m)
                den = den + jnp.sum(p, axis=-1, keepdims=True)
                p_ref[rows, c0:c0 + s.shape[1]] = p.astype(BF16)
                c0 += s.shape[1]
            inv.append(1.0 / den)
        o = jnp.dot(p_ref[...], vj, preferred_element_type=F32) * jnp.concatenate(inv, axis=0)
        for g, h in enumerate(heads):
            o_ref[0, :, h * HEAD_DIM:(h + 1) * HEAD_DIM] = o[g * tq:(g + 1) * tq].astype(BF16)


def _attention(q, k, v, sink, k_ctx=None, v_ctx=None, layer=0, cast_weights=()):
    bsz, seq, _ = q.shape
    latent = k_ctx is not None
    tq = WINDOW if latent else seq
    seq_spec = pl.BlockSpec((1, seq, D_KV), lambda b, i: (b, 0, 0))
    in_specs = [
        pl.BlockSpec(memory_space=pltpu.SMEM),
        pl.BlockSpec((1, tq, D_ATTN), lambda b, i: (b, i, 0)),
        seq_spec, seq_spec,
    ]
    args = [sink, q, k, v]
    n_keys = seq
    if latent:
        past = k_ctx.shape[1]
        ctx_spec = pl.BlockSpec((1, past, D_KV), lambda b, i: (b, 0, 0))
        in_specs += [ctx_spec, ctx_spec]
        args += [k_ctx, v_ctx]
        n_keys = 3 * tq + past
    scratch = [pltpu.VMEM((GROUP * tq, n_keys), F32), pltpu.VMEM((GROUP * tq, n_keys), BF16)]
    if latent:
        scratch.append(pltpu.VMEM((tq, 3 * tq), F32))
    out_shape = [jax.ShapeDtypeStruct((bsz, seq, D_ATTN), BF16)]
    out_specs = [pl.BlockSpec((1, tq, D_ATTN), lambda b, i: (b, i, 0))]
    n_inner = seq // tq
    for w in cast_weights:
        src, dst, shape = _cast_specs(w, layer, n_inner, bsz * n_inner)
        in_specs.append(src)
        args.append(w)
        out_specs.append(dst)
        out_shape.append(shape)
    outs = pl.pallas_call(
        functools.partial(_attn_kernel, latent=latent, seq=seq, tq=tq, n_cast=len(cast_weights)),
        out_shape=out_shape,
        grid=(bsz, n_inner),
        in_specs=in_specs,
        out_specs=out_specs,
        scratch_shapes=scratch,
        compiler_params=_cparams("parallel", "parallel"),
        name="attn_latent" if latent else "attn_context",
    )(*args)
    return outs[0], outs[1:]


LRU_HALO = 8
LRU_CHUNK = 128
LRU_HALF = D_LRU // 2


def _softplus(x):
    return jnp.maximum(x, 0.0) + jnp.log(1.0 + jnp.exp(-jnp.abs(x)))


def _gelu_tanh(x):
    return 0.5 * x * (1.0 + jnp.tanh(0.7978845608028654 * (x + 0.044715 * (x * x * x))))


def _scan_tile(a, b, carry, reverse):
    row = lax.broadcasted_iota(jnp.int32, a.shape, 0)
    for d in (1, 2, 4):
        if reverse:
            valid = row < SUB - d
            shift = SUB - d
        else:
            valid = row >= d
            shift = d
        a_prev = jnp.where(valid, pltpu.roll(a, shift, 0), 1.0)
        b_prev = jnp.where(valid, pltpu.roll(b, shift, 0), 0.0)
        b = a * b_prev + b
        a = a * a_prev
    h = a * carry + b
    last = h[0:1, :] if reverse else h[SUB - 1:SUB, :]
    return h, last


def _lru_kernel(lx_ref, lg_ref, h0_ref, cw_ref, cb_ref, wbd_ref, gbias_ref, lam_ref,
                o_ref, fin_ref, xp_ref, win_ref, af_ref, bf_ref, ab_ref, bb_ref, *, seq):
    zeros = jnp.zeros((LRU_HALO, D_LRU), F32)
    xp_ref[0:LRU_HALO, :] = zeros
    xp_ref[LRU_HALO + seq:2 * LRU_HALO + seq, :] = zeros
    xp_ref[LRU_HALO:LRU_HALO + seq, :] = lx_ref[0]
    a_refs = (af_ref, ab_ref)
    b_refs = (bf_ref, bb_ref)
    half_c = (-0.5 * LRU_C) * _softplus(-lam_ref[...])

    def gates(c, carry):
        r0 = pl.multiple_of(c * LRU_CHUNK, LRU_CHUNK)
        win_ref[...] = xp_ref[pl.ds(r0, LRU_CHUNK + 2 * LRU_HALO), :]
        xc = jnp.broadcast_to(cb_ref[...], (LRU_CHUNK, D_LRU))
        for k in range(LRU_CONV_K):
            off = LRU_HALO - 2 + k
            xc = xc + cw_ref[k:k + 1, :] * win_ref[off:off + LRU_CHUNK, :]
        for s in range(2):
            cols = slice(s * LRU_HALF, (s + 1) * LRU_HALF)
            xs = xc[:, cols]
            g = jnp.dot(xs.astype(BF16), wbd_ref[s], preferred_element_type=F32)
            g = g + gbias_ref[s:s + 1, :]
            for d in range(2):
                base = d * 2 * LRU_HALF
                hc = half_c[d:d + 1, cols]
                log_a = hc * jnp.tanh(g[:, base:base + LRU_HALF]) + hc
                i = 0.5 * jnp.tanh(g[:, base + LRU_HALF:base + 2 * LRU_HALF]) + 0.5
                a = jnp.exp(log_a)
                t = jnp.tanh(log_a)
                b = jnp.sqrt(-2.0 * t / (1.0 - t)) * (i * xs)
                a_refs[d][pl.ds(r0, LRU_CHUNK), cols] = a
                b_refs[d][pl.ds(r0, LRU_CHUNK), cols] = b
        return carry

    lax.fori_loop(0, seq // LRU_CHUNK, gates, 0)

    ntile = seq // SUB

    def scan(t, carry):
        cf, cb = carry
        rf = pl.multiple_of(t * SUB, SUB)
        rb = pl.multiple_of((ntile - 1 - t) * SUB, SUB)
        hf, cf = _scan_tile(af_ref[pl.ds(rf, SUB), :], bf_ref[pl.ds(rf, SUB), :], cf, False)
        hb, cb = _scan_tile(ab_ref[pl.ds(rb, SUB), :], bb_ref[pl.ds(rb, SUB), :], cb, True)
        bf_ref[pl.ds(rf, SUB), :] = hf
        bb_ref[pl.ds(rb, SUB), :] = hb
        return cf, cb

    h0 = h0_ref[0]
    cf, cb = lax.fori_loop(0, ntile, scan, (h0[0:1, :], h0[1:2, :]))
    fin_ref[0, 0:1, :] = cf
    fin_ref[0, 1:2, :] = cb

    def finish(c, carry):
        r0 = pl.multiple_of(c * LRU_CHUNK, LRU_CHUNK)
        rows = pl.ds(r0, LRU_CHUNK)
        o_ref[0, rows, :] = ((bf_ref[rows, :] + bb_ref[rows, :]) * _gelu_tanh(lg_ref[0, rows, :])).astype(BF16)
        return carry

    lax.fori_loop(0, seq // LRU_CHUNK, finish, 0)


def _lru_gate_weights(wa, ba, wx, bx):
    per_half = LRU_HALF // LRU_BLK
    eye = jnp.eye(per_half, dtype=F32)

    def dense(w):
        blocks = w.reshape(2, per_half, LRU_BLK, LRU_BLK)
        return jnp.einsum("snkj,nm->snkmj", blocks, eye).reshape(2, LRU_HALF, LRU_HALF)

    wbd = jnp.concatenate([dense(wa[0]), dense(wx[0]), dense(wa[1]), dense(wx[1])], axis=2)
    gbias = jnp.stack([ba[0], bx[0], ba[1], bx[1]]).reshape(4, 2, LRU_HALF).transpose(1, 0, 2)
    return (0.5 * wbd).astype(BF16), 0.5 * gbias.reshape(2, 4 * LRU_HALF)


def _recurrent_mixer(lx, lg, h0, cw, cb, wbd, gbias, lam):
    bsz, seq, _ = lx.shape
    seq_spec = pl.BlockSpec((1, seq, D_LRU), lambda i: (i, 0, 0))
    state_spec = pl.BlockSpec((1, 2, D_LRU), lambda i: (i, 0, 0))
    return pl.pallas_call(
        functools.partial(_lru_kernel, seq=seq),
        out_shape=[jax.ShapeDtypeStruct((bsz, seq, D_LRU), BF16),
                   jax.ShapeDtypeStruct((bsz, 2, D_LRU), F32)],
        grid=(bsz,),
        in_specs=[
            seq_spec, seq_spec, state_spec,
            _const_spec((LRU_CONV_K, D_LRU)), _const_spec((1, D_LRU)),
            _const_spec((2, LRU_HALF, 4 * LRU_HALF)), _const_spec((2, 4 * LRU_HALF)),
            _const_spec((2, D_LRU)),
        ],
        out_specs=[seq_spec, state_spec],
        scratch_shapes=[pltpu.VMEM((seq + 2 * LRU_HALO, D_LRU), F32),
                        pltpu.VMEM((LRU_CHUNK + 2 * LRU_HALO, D_LRU), F32)] + [pltpu.VMEM((seq, D_LRU), F32)] * 4,
        compiler_params=_cparams("parallel"),
        name="rglru",
    )(lx, lg, h0, cw, cb.reshape(1, D_LRU), wbd, gbias, lam)


N_PAIRS = EXPERTS_PER_GROUP * (EXPERTS_PER_GROUP - 1) // 2
N_BUCKETS = N_EXPERT_GROUPS * N_PAIRS
PAIR_LO = (0, 0, 0, 1, 1, 2)
PAIR_HI = (1, 2, 3, 2, 3, 3)
ROUTE_ROWS = SUB


def _route(scores, biased):
    rows = [biased[e:e + 1, :] for e in range(N_EXPERTS)]
    group_score = []
    for g in range(N_EXPERT_GROUPS):
        a, b, c, d = rows[4 * g:4 * g + 4]
        hi1, lo1 = jnp.maximum(a, b), jnp.minimum(a, b)
        hi2, lo2 = jnp.maximum(c, d), jnp.minimum(c, d)
        top = jnp.maximum(hi1, hi2)
        second = jnp.maximum(jnp.minimum(hi1, hi2), jnp.maximum(lo1, lo2))
        group_score.append(top + second)
    best = group_score[0]
    g_sel = jnp.zeros_like(best, dtype=jnp.int32)
    for g in range(1, N_EXPERT_GROUPS):
        better = group_score[g] > best
        g_sel = jnp.where(better, g, g_sel)
        best = jnp.where(better, group_score[g], best)
    sel = []
    for e in range(N_EXPERTS):
        g = e // EXPERTS_PER_GROUP
        rank = jnp.zeros_like(g_sel)
        for o in range(g * EXPERTS_PER_GROUP, (g + 1) * EXPERTS_PER_GROUP):
            if o == e:
                continue
            ahead = (rows[o] >= rows[e]) if o < e else (rows[o] > rows[e])
            rank = rank + ahead.astype(jnp.int32)
        sel.append(jnp.where(jnp.where(g_sel == g, rank, 2) < 2, 1, 0))
    zero = jnp.zeros_like(best)
    lo_w, hi_w = zero, zero
    lo_idx = jnp.zeros_like(g_sel)
    hi_idx = jnp.zeros_like(g_sel)
    for g in range(N_EXPERT_GROUPS):
        seen = jnp.zeros_like(g_sel)
        for j in range(EXPERTS_PER_GROUP):
            e = g * EXPERTS_PER_GROUP + j
            order = jnp.where(sel[e] == 1, seen, 2)
            lo_w = jnp.where(order == 0, scores[e:e + 1, :], lo_w)
            hi_w = jnp.where(order == 1, scores[e:e + 1, :], hi_w)
            lo_idx = jnp.where(order == 0, j, lo_idx)
            hi_idx = jnp.where(order == 1, j, hi_idx)
            seen = seen + sel[e]
    pair_base = jnp.where(lo_idx == 0, 0, jnp.where(lo_idx == 1, 3, 5))
    bucket = g_sel * N_PAIRS + pair_base + hi_idx - lo_idx - 1
    total = lo_w + hi_w
    pad = jnp.zeros((ROUTE_ROWS - 3, best.shape[1]), F32)
    return jnp.concatenate([bucket.astype(F32), lo_w / total, hi_w / total, pad], axis=0)


def _modulated_norm(x, g, shift, scale):
    y = x * lax.rsqrt(jnp.mean(x * x, axis=-1, keepdims=True) + EPS) * g
    return y * (1.0 + scale) + shift


META_LANES = 128
ROW_LANES = 2 * D_MODEL + META_LANES
META_W_LO, META_W_HI, META_MOD_ROW = 1, 2, 3
OUTPROJ_PARTS = 1


def _outproj_kernel(*refs, row0, row_step, n_cast):
    conv_ref, attn_ref, lru_ref, x_ref, mod_ref, g2_ref, w_ref, wr_ref, br_ref = refs[:9]
    cast_src = refs[9:9 + n_cast]
    x1_ref, route_ref = refs[9 + n_cast:11 + n_cast]
    cast_dst = refs[11 + n_cast:11 + 2 * n_cast]
    cat_ref = refs[-1]
    for src, dst in zip(cast_src, cast_dst):
        dst[0] = src[0, 0].astype(BF16)
    cat_ref[:, 0:D_CONV] = conv_ref[0]
    cat_ref[:, D_CONV:D_CONV + D_ATTN] = attn_ref[0]
    cat_ref[:, D_CONV + D_ATTN:] = lru_ref[0]
    mod = mod_ref[0]
    tm = cat_ref.shape[0]
    part = tm // OUTPROJ_PARTS
    for p in range(OUTPROJ_PARTS):
        rows = slice(p * part, (p + 1) * part)
        mix = jnp.dot(cat_ref[rows, :], w_ref[0], preferred_element_type=F32)
        x1 = x_ref[0, rows, :] + mod[2:3] * mix
        x1_ref[0, rows, 0:D_MODEL] = x1
        h2 = _modulated_norm(x1, g2_ref[...], mod[3:4], mod[4:5])
        x1_ref[0, rows, D_MODEL:2 * D_MODEL] = h2
        h_hi, h_lo = _split_bf16(h2)
        nt = (((1,), (1,)), ((), ()))
        by_hi = lax.dot_general(wr_ref[...], h_hi, nt, preferred_element_type=F32)
        by_lo = lax.dot_general(wr_ref[0:N_EXPERTS, :], h_lo, nt, preferred_element_type=F32)
        logits = by_hi[0:N_EXPERTS] + by_hi[N_EXPERTS:] + by_lo
        scores = _sigmoid(logits)
        route = _route(scores, scores + br_ref[...])
        route_ref[0, :, rows] = route
        mod_row = jnp.full((1, part), row0, jnp.int32) + row_step * pl.program_id(0)
        meta = jnp.concatenate([route[0:META_MOD_ROW], mod_row.astype(F32),
                                jnp.zeros((META_LANES - META_MOD_ROW - 1, part), F32)], axis=0)
        x1_ref[0, rows, 2 * D_MODEL:] = meta.T


def _outproj(conv, attn, lru, x, mods, row0, row_step, g2, w_out_bf, w_router_t, b_router, tm, layer,
             cast_weights):
    bsz, seq, _ = x.shape
    tok = lambda width: pl.BlockSpec((1, tm, width), lambda b, i: (b, i, 0))
    in_specs = [
        tok(D_CONV), tok(D_ATTN), tok(D_LRU), tok(D_MODEL),
        pl.BlockSpec((1, 6, D_MODEL), lambda b, i: (row0 + row_step * b, 0, 0)),
        _const_spec((1, D_MODEL)),
        _layer_spec((D_MODEL, D_MODEL), layer),
        _const_spec((2 * N_EXPERTS, D_MODEL)),
        _const_spec((N_EXPERTS, 1)),
    ]
    args = [conv, attn, lru, x, mods, g2.reshape(1, D_MODEL), w_out_bf, w_router_t, b_router.reshape(N_EXPERTS, 1)]
    out_shape = [jax.ShapeDtypeStruct((bsz, seq, ROW_LANES), F32),
                 jax.ShapeDtypeStruct((bsz, ROUTE_ROWS, seq), F32)]
    out_specs = [tok(ROW_LANES), pl.BlockSpec((1, ROUTE_ROWS, tm), lambda b, i: (b, 0, i))]
    n_inner = seq // tm
    for w in cast_weights:
        src, dst, shape = _cast_specs(w, layer, n_inner, bsz * n_inner)
        in_specs.append(src)
        args.append(w)
        out_specs.append(dst)
        out_shape.append(shape)
    outs = pl.pallas_call(
        functools.partial(_outproj_kernel, row0=row0, row_step=row_step, n_cast=len(cast_weights)),
        out_shape=out_shape,
        grid=(bsz, n_inner),
        in_specs=in_specs,
        out_specs=out_specs,
        scratch_shapes=[pltpu.VMEM((tm, D_MODEL), BF16)],
        compiler_params=_cparams("parallel", "parallel"),
        name="outproj_router",
    )(*args)
    return outs[0], outs[1], outs[2:]


MOE_TILE = 256
MOE_UNROLL = 8
MOE_PARTS = 1


def _moe_plan(bucket, t_ctx, tm):
    t_all = bucket.shape[0]
    n_tiles = (t_all + N_BUCKETS * (tm - 1) + tm - 1) // tm
    ids = jnp.arange(N_BUCKETS, dtype=jnp.int32)
    order = jnp.argsort(bucket, stable=True).astype(jnp.int32)
    member = (bucket[None, :] == ids[:, None]).astype(jnp.int32)
    counts = jnp.sum(member, axis=1)
    counts_ctx = jnp.sum(member[:, :t_ctx], axis=1)
    tiles_per = (counts + tm - 1) // tm
    tile_end = jnp.cumsum(tiles_per)
    tile_start = tile_end - tiles_per
    sorted_start = jnp.cumsum(counts) - counts
    n_used = tile_end[-1]
    j = jnp.arange(n_tiles, dtype=jnp.int32)
    jj = jnp.minimum(j, n_used - 1)
    b_of = jnp.sum((tile_end[None, :] <= jj[:, None]).astype(jnp.int32), axis=1)
    local = jj - tile_start[b_of]
    n_valid = jnp.where(j < n_used, jnp.clip(counts[b_of] - local * tm, 0, tm), 0)
    n_ctx = jnp.clip(counts_ctx[b_of] - local * tm, 0, n_valid)
    first = jnp.where(j < n_used, sorted_start[b_of] + local * tm, 0)
    group, pair = b_of // N_PAIRS, b_of % N_PAIRS
    e_lo = group * EXPERTS_PER_GROUP + jnp.asarray(PAIR_LO, jnp.int32)[pair]
    e_hi = group * EXPERTS_PER_GROUP + jnp.asarray(PAIR_HI, jnp.int32)[pair]
    i32 = lambda a: a.astype(jnp.int32)
    return order, i32(first), i32(e_lo), i32(e_hi), i32(n_valid), i32(n_ctx)


def _moe_kernel(order_ref, first_ref, elo_ref, ehi_ref, nv_ref, nc_ref,
                xc_ref, xl_ref, mods_ref,
                wg_lo, wu_lo, wd_lo, wg_hi, wu_hi, wd_hi,
                oc_ref, ol_ref, xbuf, ybuf, gsem, ssem, *, tm, t_ctx, n_mod_rows):
    i = pl.program_id(0)
    n = pl.num_programs(0)
    slot = lax.rem(i, 2)

    def gather(hbm, t, r, s, size):
        return pltpu.make_async_copy(hbm.at[pl.ds(t, size), :], xbuf.at[s, pl.ds(r, size), :], gsem.at[s])

    def scatter(hbm, t, r, s, size):
        return pltpu.make_async_copy(ybuf.at[s, pl.ds(r, size), :], hbm.at[pl.ds(t, size), :], ssem.at[s])

    def start_rows(j, s, copy, hbm_ctx, hbm_lat):
        base = first_ref[j]

        def ctx_row(r):
            copy(hbm_ctx, order_ref[base + r], r, s, 1).start()

        def lat_row(r):
            copy(hbm_lat, order_ref[base + r] - t_ctx, r, s, 1).start()

        def rows(lo, hi, one_row):
            groups = lax.shift_right_logical(hi - lo, MOE_UNROLL.bit_length() - 1)

            def group(g, c):
                for u in range(MOE_UNROLL):
                    one_row(lo + g * MOE_UNROLL + u)
                return c

            def single(r, c):
                one_row(r)
                return c

            lax.fori_loop(0, groups, group, 0)
            lax.fori_loop(lo + groups * MOE_UNROLL, hi, single, 0)

        rows(0, nc_ref[j], ctx_row)
        rows(nc_ref[j], nv_ref[j], lat_row)

    def wait_rows(j, s, copy, hbm):
        cnt = nv_ref[j]
        bulk = pl.multiple_of(lax.shift_left(lax.shift_right_logical(cnt, 3), 3), SUB)

        @pl.when(bulk > 0)
        def _():
            copy(hbm, 0, 0, s, bulk).wait()

        def one(r, c):
            copy(hbm, 0, 0, s, 1).wait()
            return c

        lax.fori_loop(bulk, cnt, one, 0)

    @pl.when(i == 0)
    def _():
        xbuf[...] = jnp.zeros_like(xbuf)
        start_rows(0, 0, gather, xc_ref, xl_ref)

    @pl.when(i + 1 < n)
    def _():
        start_rows(i + 1, 1 - slot, gather, xc_ref, xl_ref)

    wait_rows(i, slot, gather, xl_ref)

    @pl.when(i >= 2)
    def _():
        wait_rows(i - 2, slot, scatter, ol_ref)

    def tile_part(rows):
        x = xbuf[slot, rows, 0:D_MODEL]
        h = xbuf[slot, rows, D_MODEL:2 * D_MODEL].astype(BF16)
        meta = xbuf[slot, rows, 2 * D_MODEL:]
        w_lo = meta[:, META_W_LO:META_W_LO + 1]
        w_hi = meta[:, META_W_HI:META_W_HI + 1]
        mrow = meta[:, META_MOD_ROW:META_MOD_ROW + 1]
        n_rows = x.shape[0]

        def per_row(*ks):
            ids = jnp.broadcast_to(mrow, (n_rows, 128))
            masks = [ids == float(r) for r in range(1, n_mod_rows)]
            cols = [[] for _ in ks]
            for c in range(D_MODEL // 128):
                lanes = slice(c * 128, (c + 1) * 128)
                for q, k in enumerate(ks):
                    v = jnp.broadcast_to(mods_ref[0, k:k + 1, lanes], (n_rows, 128))
                    for r, mask in enumerate(masks, 1):
                        v = jnp.where(mask, mods_ref[r, k:k + 1, lanes], v)
                    cols[q].append(v)
            return [jnp.concatenate(col, axis=1) for col in cols]

        def expert(wg, wu, w):
            act = _silu(jnp.dot(h, wg[0], preferred_element_type=F32))
            act = act * jnp.dot(h, wu[0], preferred_element_type=F32) * w
            return act.astype(BF16)

        y = jnp.dot(expert(wg_lo, wu_lo, w_lo), wd_lo[0], preferred_element_type=F32)
        y = y + jnp.dot(expert(wg_hi, wu_hi, w_hi), wd_hi[0], preferred_element_type=F32)
        (gate,) = per_row(5)
        ybuf[slot, rows, :] = x + gate * y

    @pl.when(nv_ref[i] > 0)
    def _():
        part = tm // MOE_PARTS
        for p in range(MOE_PARTS):
            tile_part(slice(p * part, (p + 1) * part))

    start_rows(i, slot, scatter, oc_ref, ol_ref)

    @pl.when(i == n - 1)
    def _():
        wait_rows(i, slot, scatter, ol_ref)

        @pl.when(i >= 1)
        def _():
            wait_rows(i - 1, 1 - slot, scatter, ol_ref)


def _moe(x1c, x1l, route_c, route_l, mods, wg, wu, wd, n_mod_rows):
    t_ctx, t_lat = x1c.shape[0], x1l.shape[0]
    tm = MOE_TILE
    bucket = jnp.concatenate([route_c[:, 0, :].reshape(t_ctx), route_l[:, 0, :].reshape(t_lat)])
    order, first, e_lo, e_hi, n_valid, n_ctx = _moe_plan(bucket.astype(jnp.int32), t_ctx, tm)
    n_tiles = n_valid.shape[0]
    w_up = lambda sel: pl.BlockSpec((1, D_MODEL, D_EXPERT),
                                    lambda i, order, first, lo, hi, nv, nc: ((lo, hi)[sel][i], 0, 0))
    w_dn = lambda sel: pl.BlockSpec((1, D_EXPERT, D_MODEL),
                                    lambda i, order, first, lo, hi, nv, nc: ((lo, hi)[sel][i], 0, 0))
    hbm = pl.BlockSpec(memory_space=pl.ANY)
    return pl.pallas_call(
        functools.partial(_moe_kernel, tm=tm, t_ctx=t_ctx, n_mod_rows=n_mod_rows),
        out_shape=[jax.ShapeDtypeStruct((t_ctx, D_MODEL), F32), jax.ShapeDtypeStruct((t_lat, D_MODEL), F32)],
        grid_spec=pltpu.PrefetchScalarGridSpec(
            num_scalar_prefetch=6,
            grid=(n_tiles,),
            in_specs=[
                hbm, hbm,
                _const_spec((MOD_ROWS, 6, D_MODEL)),
                w_up(0), w_up(0), w_dn(0), w_up(1), w_up(1), w_dn(1),
            ],
            out_specs=[hbm, hbm],
            scratch_shapes=[
                pltpu.VMEM((2, tm, ROW_LANES), F32), pltpu.VMEM((2, tm, D_MODEL), F32),
                pltpu.SemaphoreType.DMA((2,)), pltpu.SemaphoreType.DMA((2,)),
            ],
        ),
        compiler_params=_cparams("arbitrary"),
        name="moe_pairs",
    )(order, first, e_lo, e_hi, n_valid, n_ctx, x1c, x1l, mods, wg, wu, wd, wg, wu, wd)


PROJ_TILE = 512


def _mixers(x, p, mods, row0, row_step, h0, layer, casts_in, casts_attn, k_ctx=None, v_ctx=None):
    latent = k_ctx is not None
    bsz, seq, _ = x.shape
    tm = PROJ_TILE if (row_step == 0 or seq % PROJ_TILE == 0) else seq
    rows = bsz * seq
    flat = lambda a: a.reshape(1 if row_step == 0 else bsz, rows if row_step == 0 else seq, a.shape[-1])
    (z, q, k, v, lg, lx), cast_a = _inproj(flat(x), mods, row0, row_step, p["norm1_g"], p["w_in"], p["q_norm_g"],
                                           p["k_norm_g"], latent, tm, layer, casts_in)
    seqs = lambda a: a.reshape(bsz, seq, a.shape[-1])
    z, q, k, v, lg, lx = map(seqs, (z, q, k, v, lg, lx))
    conv = _conv_module(z, p["conv_dw"], p["conv_b"], p["conv_ln_g"], p["conv_ln_b"])
    attn, cast_b = _attention(q, k, v, p["attn_sink"], k_ctx, v_ctx, layer, casts_attn)
    lru, fin = _recurrent_mixer(lx, lg, h0, p["lru_conv_w"], p["lru_conv_b"], p["lru_wbd"], p["lru_gbias"],
                                p["lru_lam"])
    x1, route, _ = _outproj(flat(conv), flat(attn), flat(lru), flat(x), mods, row0, row_step, p["norm2_g"],
                            p["w_out"], p["w_router_t"], p["b_router"], tm, layer, [])
    return x1, route, k, v, fin, list(cast_a) + list(cast_b)


def kernel(x_prompt, x_sample, c, cache_k, cache_v, state_lru, c_ctx, w_mod, b_mod, norm1_g, norm2_g, w_in,
           conv_dw, conv_b, conv_ln_g, conv_ln_b, q_norm_g, k_norm_g, attn_sink, lru_conv_w, lru_conv_b,
           lru_wa, lru_ba, lru_wx, lru_bx, lru_lam, w_out, w_router, b_router, w_gate_e, w_up_e, w_down_e):
    bsz, seq, _ = x_prompt.shape
    dec_bsz, dec_seq, _ = x_sample.shape
    past = cache_k.shape[2]

    cvec = jnp.zeros((MOD_ROWS, D_MODEL), F32).at[0].set(c_ctx).at[1:1 + dec_bsz].set(c)
    mods_all = _modulation(cvec, w_mod, b_mod).reshape(DEPTH, MOD_ROWS, 6, D_MODEL)

    wr_hi = w_router.T.astype(BF16)
    wr_lo = (w_router.T - wr_hi.astype(F32)).astype(BF16)
    w_router_t = jnp.concatenate([wr_hi, wr_lo], axis=0)
    w_in_bf, w_out_bf = w_in.astype(BF16), w_out.astype(BF16)
    layers = []
    for l in range(DEPTH):
        wbd, gbias = _lru_gate_weights(lru_wa[l], lru_ba[l], lru_wx[l], lru_bx[l])
        layers.append({
            "norm1_g": norm1_g[l], "norm2_g": norm2_g[l], "w_in": w_in_bf,
            "conv_dw": conv_dw[l], "conv_b": conv_b[l], "conv_ln_g": conv_ln_g[l], "conv_ln_b": conv_ln_b[l],
            "q_norm_g": q_norm_g[l], "k_norm_g": k_norm_g[l], "attn_sink": attn_sink[l],
            "lru_conv_w": lru_conv_w[l], "lru_conv_b": lru_conv_b[l], "lru_wbd": wbd, "lru_gbias": gbias,
            "lru_lam": lru_lam[l],
            "w_out": w_out_bf, "w_router_t": w_router_t, "b_router": b_router,
        })

    y, z = x_prompt, x_sample
    ks, vs, hs = [], [], []
    h0_ctx = jnp.zeros((bsz, 2, D_LRU), F32)
    for l in range(DEPTH):
        p, mods = layers[l], mods_all[l]
        y1, route_c, k_l, v_l, h_l, _ = _mixers(y, p, mods, 0, 0, h0_ctx, l, [], [])
        ks.append(k_l.reshape(bsz, seq, N_KV_HEADS, HEAD_DIM))
        vs.append(v_l.reshape(bsz, seq, N_KV_HEADS, HEAD_DIM))
        hs.append(h_l)
        z1, route_l, _, _, _, (wu_bf, wg_bf, wd_bf) = _mixers(
            z, p, mods, 1, 1, state_lru[:, l], l, [w_up_e], [w_gate_e, w_down_e],
            cache_k[:, l].reshape(dec_bsz, past, D_KV).astype(BF16),
            cache_v[:, l].reshape(dec_bsz, past, D_KV).astype(BF16))
        y, z = _moe(y1.reshape(bsz * seq, -1), z1.reshape(dec_bsz * dec_seq, -1), route_c, route_l,
                    mods, wg_bf, wu_bf, wd_bf, 1 + dec_bsz)
        y, z = y.reshape(bsz, seq, D_MODEL), z.reshape(dec_bsz, dec_seq, D_MODEL)
    new_cache_k = jnp.stack(ks, axis=1)
    new_cache_v = jnp.stack(vs, axis=1)
    new_state_lru = jnp.stack(hs, axis=1)
    return y, z, new_cache_k, new_cache_v, new_state_lru
```

```python
import functools

import numpy as np
import jax
import jax.numpy as jnp
from jax import lax
from jax.experimental import pallas as pl
from jax.experimental.pallas import tpu as pltpu

D_MODEL = 2048
DEPTH = 2
GRID_W = 64
D_CONV = 512
N_HEADS = 8
N_KV_HEADS = 2
HEAD_DIM = 128
GROUP = N_HEADS // N_KV_HEADS
D_ATTN = N_HEADS * HEAD_DIM
D_KV = N_KV_HEADS * HEAD_DIM
D_LRU = 512
CONV_K = 31
LRU_CONV_K = 4
LRU_BLOCKS = 8
LRU_BLK = D_LRU // LRU_BLOCKS
LRU_C = 8.0
WINDOW = 128
ROPE_BASE = 10000.0
ATTN_SCALE = HEAD_DIM ** -0.5
N_EXPERTS = 16
N_EXPERT_GROUPS = 4
EXPERTS_PER_GROUP = N_EXPERTS // N_EXPERT_GROUPS
D_EXPERT = 512
EPS = 1e-6
NEG = -1e30
D_IN = 2 * D_CONV + D_ATTN + 2 * D_KV + 2 * D_LRU
N_MOD = 6 * D_MODEL
SUB = 8
MOD_ROWS = SUB

V7X_VMEM_BYTES = 64 * 1024 * 1024
VMEM_LIMIT = V7X_VMEM_BYTES - 8 * 1024 * 1024

F32 = jnp.float32
BF16 = jnp.bfloat16
HIGHEST = lax.Precision.HIGHEST


def _cparams(*sem):
    return pltpu.CompilerParams(dimension_semantics=sem, vmem_limit_bytes=VMEM_LIMIT)


def _sigmoid(x):
    return 0.5 * jnp.tanh(0.5 * x) + 0.5


def _silu(x):
    return x * _sigmoid(x)


def _const_spec(shape):
    return pl.BlockSpec(shape, lambda *_: (0,) * len(shape), pipeline_mode=pl.Buffered(1))


def _layer_spec(shape, layer):
    return pl.BlockSpec((1,) + shape, lambda *_: (layer,) + (0,) * len(shape), pipeline_mode=pl.Buffered(1))


def _split_bf16(x):
    hi = x.astype(BF16)
    return hi, (x - hi.astype(F32)).astype(BF16)


def _mod_kernel(c_ref, w_ref, b_ref, o_ref):
    s = _silu(c_ref[...])
    s_hi = s.astype(BF16).astype(F32)
    s_both = jnp.concatenate([s_hi, s - s_hi], axis=0).astype(BF16)
    w_hi, w_lo = _split_bf16(w_ref[0])
    by_hi = jnp.dot(s_both, w_hi, preferred_element_type=F32)
    by_lo = jnp.dot(s_both, w_lo, preferred_element_type=F32)
    o_ref[0] = by_hi[0:MOD_ROWS] + by_hi[MOD_ROWS:] + by_lo[0:MOD_ROWS] + b_ref[0]


def _modulation(cvec, w_mod, b_mod):
    tn = 1536
    return pl.pallas_call(
        _mod_kernel,
        out_shape=jax.ShapeDtypeStruct((DEPTH, MOD_ROWS, N_MOD), F32),
        grid=(DEPTH, N_MOD // tn),
        in_specs=[
            _const_spec((MOD_ROWS, D_MODEL)),
            pl.BlockSpec((1, D_MODEL, tn), lambda l, j: (l, 0, j)),
            pl.BlockSpec((1, 1, tn), lambda l, j: (l, 0, j)),
        ],
        out_specs=pl.BlockSpec((1, MOD_ROWS, tn), lambda l, j: (l, 0, j)),
        compiler_params=_cparams("parallel", "parallel"),
        name="modulation",
    )(cvec, w_mod, b_mod.reshape(DEPTH, 1, N_MOD))


def _head_norm(x, g):
    return x * lax.rsqrt(jnp.mean(x * x, axis=-1, keepdims=True) + EPS) * g


def _rope(x, cos, sin_signed):
    lane = lax.broadcasted_iota(jnp.int32, x.shape, 1)
    partner = jnp.where((lane % 64) < 32, pltpu.roll(x, 96, 1), pltpu.roll(x, 32, 1))
    return x * cos + partner * sin_signed


def _inproj_kernel(*refs, rope, n_cast):
    refs = list(refs)
    cast_dst = [refs.pop() for _ in range(n_cast)][::-1]
    z_ref, q_ref, k_ref, v_ref, lg_ref, lx_ref = refs[-6:]
    del refs[-6:]
    cast_src = [refs.pop() for _ in range(n_cast)][::-1]
    if rope:
        x_ref, mod_ref, g1_ref, w_ref, qg_ref, kg_ref, cos_ref, sin_ref = refs
    else:
        x_ref, mod_ref, g1_ref, w_ref, qg_ref, kg_ref = refs
    for src, dst in zip(cast_src, cast_dst):
        dst[0] = src[0, 0].astype(BF16)
    x = x_ref[0]
    mod = mod_ref[0]
    shift, scale = mod[0:1], mod[1:2]
    y = x * lax.rsqrt(jnp.mean(x * x, axis=-1, keepdims=True) + EPS) * g1_ref[...]
    h = (y * (1.0 + scale) + shift).astype(BF16)

    def proj(c0, c1):
        return jnp.dot(h, w_ref[0, :, c0:c1], preferred_element_type=F32)

    glu = proj(0, 2 * D_CONV)
    z_ref[0] = glu[:, :D_CONV] * _sigmoid(glu[:, D_CONV:])

    c0 = 2 * D_CONV
    q = proj(c0, c0 + D_ATTN)
    qg = qg_ref[...]
    for hd in range(N_HEADS):
        qh = _head_norm(q[:, hd * HEAD_DIM:(hd + 1) * HEAD_DIM], qg)
        if rope:
            qh = _rope(qh, cos_ref[...], sin_ref[...])
        q_ref[0, :, hd * HEAD_DIM:(hd + 1) * HEAD_DIM] = (qh * ATTN_SCALE).astype(BF16)

    c0 += D_ATTN
    kv = proj(c0, c0 + 2 * D_KV)
    kg = kg_ref[...]
    for hd in range(N_KV_HEADS):
        kh = _head_norm(kv[:, hd * HEAD_DIM:(hd + 1) * HEAD_DIM], kg)
        if rope:
            kh = _rope(kh, cos_ref[...], sin_ref[...])
        k_ref[0, :, hd * HEAD_DIM:(hd + 1) * HEAD_DIM] = kh.astype(k_ref.dtype)
    v_ref[0] = kv[:, D_KV:].astype(v_ref.dtype)

    c0 += 2 * D_KV
    l2 = proj(c0, c0 + 2 * D_LRU)
    lg_ref[0] = l2[:, :D_LRU]
    lx_ref[0] = l2[:, D_LRU:]


def _rope_tables(seq):
    n_freq = HEAD_DIM // 4
    inv = (ROPE_BASE ** (-np.arange(n_freq, dtype=np.float32) / n_freq)).astype(np.float32)
    t = np.arange(seq)
    ang_r = (t // GRID_W).astype(np.float32)[:, None] * inv[None, :]
    ang_c = (t % GRID_W).astype(np.float32)[:, None] * inv[None, :]
    cos = np.concatenate([np.cos(ang_r)] * 2 + [np.cos(ang_c)] * 2, axis=-1)
    sin = np.concatenate([-np.sin(ang_r), np.sin(ang_r), -np.sin(ang_c), np.sin(ang_c)], axis=-1)
    return jnp.asarray(cos, F32), jnp.asarray(sin, F32)


def _cast_specs(w, layer, n_inner, n_steps):
    _, n_e, rows, cols = w.shape
    parts = n_steps // n_e
    assert parts * n_e == n_steps and rows % parts == 0
    piece = lambda b, i: divmod(b * n_inner + i, parts)
    src = pl.BlockSpec((1, 1, rows // parts, cols), lambda b, i: (layer, *piece(b, i), 0))
    dst = pl.BlockSpec((1, rows // parts, cols), lambda b, i: (*piece(b, i), 0))
    return src, dst, jax.ShapeDtypeStruct((n_e, rows, cols), BF16)


def _inproj(x, mods, row0, row_step, g1, w_in_bf, qg, kg, rope, tm, layer, cast_weights):
    bsz, seq, _ = x.shape
    tok = lambda width: pl.BlockSpec((1, tm, width), lambda b, i: (b, i, 0))
    in_specs = [
        tok(D_MODEL),
        pl.BlockSpec((1, 6, D_MODEL), lambda b, i: (row0 + row_step * b, 0, 0)),
        _const_spec((1, D_MODEL)),
        _layer_spec((D_MODEL, D_IN), layer),
        _const_spec((1, HEAD_DIM)),
        _const_spec((1, HEAD_DIM)),
    ]
    args = [x, mods, g1.reshape(1, D_MODEL), w_in_bf, qg.reshape(1, HEAD_DIM), kg.reshape(1, HEAD_DIM)]
    if rope:
        cos, sin = _rope_tables(seq)
        in_specs += [pl.BlockSpec((tm, HEAD_DIM), lambda b, i: (i, 0))] * 2
        args += [cos, sin]
    widths = (D_CONV, D_ATTN, D_KV, D_KV, D_LRU, D_LRU)
    kv_dtype = BF16 if rope else F32
    dtypes = (F32, BF16, kv_dtype, kv_dtype, F32, F32)
    out_shape = [jax.ShapeDtypeStruct((bsz, seq, w), dt) for w, dt in zip(widths, dtypes)]
    out_specs = [tok(w) for w in widths]
    n_inner = seq // tm
    for w in cast_weights:
        src, dst, shape = _cast_specs(w, layer, n_inner, bsz * n_inner)
        in_specs.append(src)
        args.append(w)
        out_specs.append(dst)
        out_shape.append(shape)
    outs = pl.pallas_call(
        functools.partial(_inproj_kernel, rope=rope, n_cast=len(cast_weights)),
        out_shape=out_shape,
        grid=(bsz, n_inner),
        in_specs=in_specs,
        out_specs=out_specs,
        compiler_params=_cparams("parallel", "parallel"),
        name="inproj_rope" if rope else "inproj",
    )(*args)
    return outs[:6], outs[6:]


CONV_PAD = (CONV_K - 1) // 2
CONV_HALO = 16
CONV_CHUNK = 64
CONV_COLS = 256
CONV_GROUP = 4


def _conv_kernel(z_ref, w_ref, b_ref, g_ref, beta_ref, o_ref, zp_ref, win_ref, acc_ref, *, seq):
    zeros = jnp.zeros((CONV_HALO, D_CONV), F32)
    zp_ref[0:CONV_HALO, :] = zeros
    zp_ref[CONV_HALO + seq:2 * CONV_HALO + seq, :] = zeros
    zp_ref[CONV_HALO:CONV_HALO + seq, :] = z_ref[0]
    n_parts = D_CONV // CONV_COLS

    group_rows = CONV_GROUP * CONV_CHUNK

    def body(c, carry):
        r0 = pl.multiple_of(c * group_rows, group_rows)
        for q in range(CONV_GROUP):
            for p in range(n_parts):
                win_ref[q * n_parts + p] = zp_ref[pl.ds(r0 + q * CONV_CHUNK, CONV_CHUNK + 2 * CONV_HALO),
                                                  p * CONV_COLS:(p + 1) * CONV_COLS]

        def taps(j, inner):
            p = lax.rem(j, n_parts)
            part = jnp.broadcast_to(b_ref[p], (CONV_CHUNK, CONV_COLS))
            for k in range(CONV_K):
                off = CONV_HALO - CONV_PAD + k
                part = part + jnp.tile(w_ref[p, k], (CONV_CHUNK // SUB, 1)) * win_ref[j, off:off + CONV_CHUNK, :]
            acc_ref[j] = part
            return inner

        lax.fori_loop(0, CONV_GROUP * n_parts, taps, 0)
        acc = jnp.concatenate(
            [jnp.concatenate([acc_ref[q * n_parts + p] for p in range(n_parts)], axis=1) for q in range(CONV_GROUP)],
            axis=0)
        mu = jnp.mean(acc, axis=-1, keepdims=True)
        xc = acc - mu
        var = jnp.mean(xc * xc, axis=-1, keepdims=True)
        y = xc * lax.rsqrt(var + EPS) * g_ref[...] + beta_ref[...]
        o_ref[0, pl.ds(r0, group_rows), :] = _silu(y).astype(BF16)
        return carry

    lax.fori_loop(0, seq // group_rows, body, 0)


def _conv_module(z, w, b, g, beta):
    bsz, seq, _ = z.shape
    row = lambda a: a.reshape(1, D_CONV)
    n_parts = D_CONV // CONV_COLS
    w_parts = jnp.broadcast_to(w.reshape(CONV_K, 1, n_parts, CONV_COLS), (CONV_K, SUB, n_parts, CONV_COLS))
    w_parts = w_parts.transpose(2, 0, 1, 3)
    return pl.pallas_call(
        functools.partial(_conv_kernel, seq=seq),
        out_shape=jax.ShapeDtypeStruct((bsz, seq, D_CONV), BF16),
        grid=(bsz,),
        in_specs=[
            pl.BlockSpec((1, seq, D_CONV), lambda i: (i, 0, 0)),
            _const_spec((n_parts, CONV_K, SUB, CONV_COLS)),
            _const_spec((n_parts, 1, CONV_COLS)), _const_spec((1, D_CONV)), _const_spec((1, D_CONV)),
        ],
        out_specs=pl.BlockSpec((1, seq, D_CONV), lambda i: (i, 0, 0)),
        scratch_shapes=[pltpu.VMEM((seq + 2 * CONV_HALO, D_CONV), F32),
                        pltpu.VMEM((CONV_GROUP * n_parts, CONV_CHUNK + 2 * CONV_HALO, CONV_COLS), F32),
                        pltpu.VMEM((CONV_GROUP * n_parts, CONV_CHUNK, CONV_COLS), F32)],
        compiler_params=_cparams("parallel"),
        name="conv_module",
    )(z, w_parts, b.reshape(n_parts, 1, CONV_COLS), row(g), row(beta))


ATTN_ROWS = 64


def _attn_kernel(*refs, latent, seq, tq, n_cast):
    n_in = 6 if latent else 4
    cast_src = refs[n_in:n_in + n_cast]
    cast_dst = refs[n_in + n_cast + 1:n_in + 2 * n_cast + 1]
    refs = refs[:n_in] + refs[n_in + n_cast:n_in + n_cast + 1] + refs[n_in + 2 * n_cast + 1:]
    if latent:
        sink_ref, q_ref, k_ref, v_ref, kc_ref, vc_ref, o_ref, s_ref, p_ref, bias_ref = refs
    else:
        sink_ref, q_ref, k_ref, v_ref, o_ref, s_ref, p_ref = refs
    for src, dst in zip(cast_src, cast_dst):
        dst[0] = src[0, 0].astype(BF16)
    q = q_ref[0]
    n_loc = 3 * tq if latent else 0
    if latent:
        n = pl.program_id(1)
        nblk = seq // tq
        blocks = (jnp.maximum(n - 1, 0), n, jnp.minimum(n + 1, nblk - 1))

        def window(ref):
            parts = [ref[0, pl.ds(pl.multiple_of(i * tq, tq), tq), :] for i in blocks]
            return jnp.concatenate(parts, axis=0)

        k_all = jnp.concatenate([window(k_ref), kc_ref[0]], axis=0).astype(BF16)
        v_all = jnp.concatenate([window(v_ref), vc_ref[0]], axis=0).astype(BF16)
        qpos = n * tq + lax.broadcasted_iota(jnp.int32, (tq, 3 * tq), 0)
        kpos = (n - 1) * tq + lax.broadcasted_iota(jnp.int32, (tq, 3 * tq), 1)
        ok = (jnp.abs(qpos - kpos) <= WINDOW) & (kpos >= 0) & (kpos < seq)
        bias_ref[...] = jnp.where(ok, 0.0, NEG).astype(F32)
    else:
        k_all = k_ref[0].astype(BF16)
        v_all = v_ref[0].astype(BF16)

    for j in range(N_KV_HEADS):
        heads = [j * GROUP + g for g in range(GROUP)]
        qs = jnp.concatenate([q[:, h * HEAD_DIM:(h + 1) * HEAD_DIM] for h in heads], axis=0)
        kj = k_all[:, j * HEAD_DIM:(j + 1) * HEAD_DIM]
        vj = v_all[:, j * HEAD_DIM:(j + 1) * HEAD_DIM]
        s_ref[...] = lax.dot_general(qs, kj, (((1,), (1,)), ((), ())), preferred_element_type=F32)
        inv = []
        for rb in range(GROUP * tq // ATTN_ROWS):
            rows = slice(rb * ATTN_ROWS, (rb + 1) * ATTN_ROWS)
            sk = sink_ref[heads[rb * ATTN_ROWS // tq]]
            parts = [s_ref[rows, n_loc:]]
            if latent:
                q0 = rb * ATTN_ROWS % tq
                parts.insert(0, s_ref[rows, :n_loc] + bias_ref[q0:q0 + ATTN_ROWS, :])
            m = sk
            for s in parts:
                m = jnp.maximum(m, jnp.max(s, axis=-1, keepdims=True))
            den = jnp.exp(sk - m)
            c0 = 0
            for s in parts:
                p = jnp.exp(s - m)
                den = den + jnp.sum(p, axis=-1, keepdims=True)
                p_ref[rows, c0:c0 + s.shape[1]] = p.astype(BF16)
                c0 += s.shape[1]
            inv.append(1.0 / den)
        o = jnp.dot(p_ref[...], vj, preferred_element_type=F32) * jnp.concatenate(inv, axis=0)
        for g, h in enumerate(heads):
            o_ref[0, :, h * HEAD_DIM:(h + 1) * HEAD_DIM] = o[g * tq:(g + 1) * tq].astype(BF16)


def _attention(q, k, v, sink, k_ctx=None, v_ctx=None, layer=0, cast_weights=()):
    bsz, seq, _ = q.shape
    latent = k_ctx is not None
    tq = WINDOW if latent else seq
    seq_spec = pl.BlockSpec((1, seq, D_KV), lambda b, i: (b, 0, 0))
    in_specs = [
        pl.BlockSpec(memory_space=pltpu.SMEM),
        pl.BlockSpec((1, tq, D_ATTN), lambda b, i: (b, i, 0)),
        seq_spec, seq_spec,
    ]
    args = [sink, q, k, v]
    n_keys = seq
    if latent:
        past = k_ctx.shape[1]
        ctx_spec = pl.BlockSpec((1, past, D_KV), lambda b, i: (b, 0, 0))
        in_specs += [ctx_spec, ctx_spec]
        args += [k_ctx, v_ctx]
        n_keys = 3 * tq + past
    scratch = [pltpu.VMEM((GROUP * tq, n_keys), F32), pltpu.VMEM((GROUP * tq, n_keys), BF16)]
    if latent:
        scratch.append(pltpu.VMEM((tq, 3 * tq), F32))
    out_shape = [jax.ShapeDtypeStruct((bsz, seq, D_ATTN), BF16)]
    out_specs = [pl.BlockSpec((1, tq, D_ATTN), lambda b, i: (b, i, 0))]
    n_inner = seq // tq
    for w in cast_weights:
        src, dst, shape = _cast_specs(w, layer, n_inner, bsz * n_inner)
        in_specs.append(src)
        args.append(w)
        out_specs.append(dst)
        out_shape.append(shape)
    outs = pl.pallas_call(
        functools.partial(_attn_kernel, latent=latent, seq=seq, tq=tq, n_cast=len(cast_weights)),
        out_shape=out_shape,
        grid=(bsz, n_inner),
        in_specs=in_specs,
        out_specs=out_specs,
        scratch_shapes=scratch,
        compiler_params=_cparams("parallel", "parallel"),
        name="attn_latent" if latent else "attn_context",
    )(*args)
    return outs[0], outs[1:]


LRU_HALO = 8
LRU_CHUNK = 128
LRU_HALF = D_LRU // 2


def _softplus(x):
    return jnp.maximum(x, 0.0) + jnp.log(1.0 + jnp.exp(-jnp.abs(x)))


def _gelu_tanh(x):
    return 0.5 * x * (1.0 + jnp.tanh(0.7978845608028654 * (x + 0.044715 * (x * x * x))))


def _scan_tile(a, b, carry, reverse):
    row = lax.broadcasted_iota(jnp.int32, a.shape, 0)
    for d in (1, 2, 4):
        if reverse:
            valid = row < SUB - d
            shift = SUB - d
        else:
            valid = row >= d
            shift = d
        a_prev = jnp.where(valid, pltpu.roll(a, shift, 0), 1.0)
        b_prev = jnp.where(valid, pltpu.roll(b, shift, 0), 0.0)
        b = a * b_prev + b
        a = a * a_prev
    h = a * carry + b
    last = h[0:1, :] if reverse else h[SUB - 1:SUB, :]
    return h, last


def _lru_kernel(lx_ref, lg_ref, h0_ref, cw_ref, cb_ref, wbd_ref, gbias_ref, lam_ref,
                o_ref, fin_ref, xp_ref, win_ref, af_ref, bf_ref, ab_ref, bb_ref, *, seq):
    zeros = jnp.zeros((LRU_HALO, D_LRU), F32)
    xp_ref[0:LRU_HALO, :] = zeros
    xp_ref[LRU_HALO + seq:2 * LRU_HALO + seq, :] = zeros
    xp_ref[LRU_HALO:LRU_HALO + seq, :] = lx_ref[0]
    a_refs = (af_ref, ab_ref)
    b_refs = (bf_ref, bb_ref)
    half_c = (-0.5 * LRU_C) * _softplus(-lam_ref[...])

    def gates(c, carry):
        r0 = pl.multiple_of(c * LRU_CHUNK, LRU_CHUNK)
        win_ref[...] = xp_ref[pl.ds(r0, LRU_CHUNK + 2 * LRU_HALO), :]
        xc = jnp.broadcast_to(cb_ref[...], (LRU_CHUNK, D_LRU))
        for k in range(LRU_CONV_K):
            off = LRU_HALO - 2 + k
            xc = xc + cw_ref[k:k + 1, :] * win_ref[off:off + LRU_CHUNK, :]
        for s in range(2):
            cols = slice(s * LRU_HALF, (s + 1) * LRU_HALF)
            xs = xc[:, cols]
            g = jnp.dot(xs.astype(BF16), wbd_ref[s], preferred_element_type=F32)
            g = g + gbias_ref[s:s + 1, :]
            for d in range(2):
                base = d * 2 * LRU_HALF
                hc = half_c[d:d + 1, cols]
                log_a = hc * jnp.tanh(g[:, base:base + LRU_HALF]) + hc
                i = 0.5 * jnp.tanh(g[:, base + LRU_HALF:base + 2 * LRU_HALF]) + 0.5
                a = jnp.exp(log_a)
                t = jnp.tanh(log_a)
                b = jnp.sqrt(-2.0 * t / (1.0 - t)) * (i * xs)
                a_refs[d][pl.ds(r0, LRU_CHUNK), cols] = a
                b_refs[d][pl.ds(r0, LRU_CHUNK), cols] = b
        return carry

    lax.fori_loop(0, seq // LRU_CHUNK, gates, 0)

    ntile = seq // SUB

    def scan(t, carry):
        cf, cb = carry
        rf = pl.multiple_of(t * SUB, SUB)
        rb = pl.multiple_of((ntile - 1 - t) * SUB, SUB)
        hf, cf = _scan_tile(af_ref[pl.ds(rf, SUB), :], bf_ref[pl.ds(rf, SUB), :], cf, False)
        hb, cb = _scan_tile(ab_ref[pl.ds(rb, SUB), :], bb_ref[pl.ds(rb, SUB), :], cb, True)
        bf_ref[pl.ds(rf, SUB), :] = hf
        bb_ref[pl.ds(rb, SUB), :] = hb
        return cf, cb

    h0 = h0_ref[0]
    cf, cb = lax.fori_loop(0, ntile, scan, (h0[0:1, :], h0[1:2, :]))
    fin_ref[0, 0:1, :] = cf
    fin_ref[0, 1:2, :] = cb

    def finish(c, carry):
        r0 = pl.multiple_of(c * LRU_CHUNK, LRU_CHUNK)
        rows = pl.ds(r0, LRU_CHUNK)
        o_ref[0, rows, :] = ((bf_ref[rows, :] + bb_ref[rows, :]) * _gelu_tanh(lg_ref[0, rows, :])).astype(BF16)
        return carry

    lax.fori_loop(0, seq // LRU_CHUNK, finish, 0)


def _lru_gate_weights(wa, ba, wx, bx):
    per_half = LRU_HALF // LRU_BLK
    on_diagonal = jnp.asarray(np.kron(np.eye(per_half), np.ones((LRU_BLK, LRU_BLK))), F32)

    def dense(w):
        blocks = w.reshape(2, per_half, LRU_BLK, 1, LRU_BLK)
        tiled = jnp.broadcast_to(blocks, (2, per_half, LRU_BLK, per_half, LRU_BLK))
        return tiled.reshape(2, LRU_HALF, LRU_HALF) * on_diagonal

    wbd = jnp.concatenate([dense(wa[0]), dense(wx[0]), dense(wa[1]), dense(wx[1])], axis=2)
    gbias = jnp.stack([ba[0], bx[0], ba[1], bx[1]]).reshape(4, 2, LRU_HALF).transpose(1, 0, 2)
    return (0.5 * wbd).astype(BF16), 0.5 * gbias.reshape(2, 4 * LRU_HALF)


def _recurrent_mixer(lx, lg, h0, cw, cb, wbd, gbias, lam):
    bsz, seq, _ = lx.shape
    seq_spec = pl.BlockSpec((1, seq, D_LRU), lambda i: (i, 0, 0))
    state_spec = pl.BlockSpec((1, 2, D_LRU), lambda i: (i, 0, 0))
    return pl.pallas_call(
        functools.partial(_lru_kernel, seq=seq),
        out_shape=[jax.ShapeDtypeStruct((bsz, seq, D_LRU), BF16),
                   jax.ShapeDtypeStruct((bsz, 2, D_LRU), F32)],
        grid=(bsz,),
        in_specs=[
            seq_spec, seq_spec, state_spec,
            _const_spec((LRU_CONV_K, D_LRU)), _const_spec((1, D_LRU)),
            _const_spec((2, LRU_HALF, 4 * LRU_HALF)), _const_spec((2, 4 * LRU_HALF)),
            _const_spec((2, D_LRU)),
        ],
        out_specs=[seq_spec, state_spec],
        scratch_shapes=[pltpu.VMEM((seq + 2 * LRU_HALO, D_LRU), F32),
                        pltpu.VMEM((LRU_CHUNK + 2 * LRU_HALO, D_LRU), F32)] + [pltpu.VMEM((seq, D_LRU), F32)] * 4,
        compiler_params=_cparams("parallel"),
        name="rglru",
    )(lx, lg, h0, cw, cb.reshape(1, D_LRU), wbd, gbias, lam)


N_PAIRS = EXPERTS_PER_GROUP * (EXPERTS_PER_GROUP - 1) // 2
N_BUCKETS = N_EXPERT_GROUPS * N_PAIRS
PAIR_LO = (0, 0, 0, 1, 1, 2)
PAIR_HI = (1, 2, 3, 2, 3, 3)
ROUTE_ROWS = SUB


def _route(scores, biased):
    rows = [biased[e:e + 1, :] for e in range(N_EXPERTS)]
    group_score = []
    for g in range(N_EXPERT_GROUPS):
        a, b, c, d = rows[4 * g:4 * g + 4]
        hi1, lo1 = jnp.maximum(a, b), jnp.minimum(a, b)
        hi2, lo2 = jnp.maximum(c, d), jnp.minimum(c, d)
        top = jnp.maximum(hi1, hi2)
        second = jnp.maximum(jnp.minimum(hi1, hi2), jnp.maximum(lo1, lo2))
        group_score.append(top + second)
    best = group_score[0]
    g_sel = jnp.zeros_like(best, dtype=jnp.int32)
    for g in range(1, N_EXPERT_GROUPS):
        better = group_score[g] > best
        g_sel = jnp.where(better, g, g_sel)
        best = jnp.where(better, group_score[g], best)
    sel = []
    for e in range(N_EXPERTS):
        g = e // EXPERTS_PER_GROUP
        rank = jnp.zeros_like(g_sel)
        for o in range(g * EXPERTS_PER_GROUP, (g + 1) * EXPERTS_PER_GROUP):
            if o == e:
                continue
            ahead = (rows[o] >= rows[e]) if o < e else (rows[o] > rows[e])
            rank = rank + ahead.astype(jnp.int32)
        sel.append(jnp.where(jnp.where(g_sel == g, rank, 2) < 2, 1, 0))
    zero = jnp.zeros_like(best)
    lo_w, hi_w = zero, zero
    lo_idx = jnp.zeros_like(g_sel)
    hi_idx = jnp.zeros_like(g_sel)
    for g in range(N_EXPERT_GROUPS):
        seen = jnp.zeros_like(g_sel)
        for j in range(EXPERTS_PER_GROUP):
            e = g * EXPERTS_PER_GROUP + j
            order = jnp.where(sel[e] == 1, seen, 2)
            lo_w = jnp.where(order == 0, scores[e:e + 1, :], lo_w)
            hi_w = jnp.where(order == 1, scores[e:e + 1, :], hi_w)
            lo_idx = jnp.where(order == 0, j, lo_idx)
            hi_idx = jnp.where(order == 1, j, hi_idx)
            seen = seen + sel[e]
    pair_base = jnp.where(lo_idx == 0, 0, jnp.where(lo_idx == 1, 3, 5))
    bucket = g_sel * N_PAIRS + pair_base + hi_idx - lo_idx - 1
    total = lo_w + hi_w
    pad = jnp.zeros((ROUTE_ROWS - 3, best.shape[1]), F32)
    return jnp.concatenate([bucket.astype(F32), lo_w / total, hi_w / total, pad], axis=0)


def _modulated_norm(x, g, shift, scale):
    y = x * lax.rsqrt(jnp.mean(x * x, axis=-1, keepdims=True) + EPS) * g
    return y * (1.0 + scale) + shift


META_LANES = 128
ROW_LANES = 2 * D_MODEL + META_LANES
META_W_LO, META_W_HI, META_MOD_ROW = 1, 2, 3
OUTPROJ_PARTS = 1


def _outproj_kernel(*refs, row0, row_step, n_cast):
    conv_ref, attn_ref, lru_ref, x_ref, mod_ref, g2_ref, w_ref, wr_ref, br_ref = refs[:9]
    cast_src = refs[9:9 + n_cast]
    x1_ref, route_ref = refs[9 + n_cast:11 + n_cast]
    cast_dst = refs[11 + n_cast:11 + 2 * n_cast]
    cat_ref = refs[-1]
    for src, dst in zip(cast_src, cast_dst):
        dst[0] = src[0, 0].astype(BF16)
    cat_ref[:, 0:D_CONV] = conv_ref[0]
    cat_ref[:, D_CONV:D_CONV + D_ATTN] = attn_ref[0]
    cat_ref[:, D_CONV + D_ATTN:] = lru_ref[0]
    mod = mod_ref[0]
    tm = cat_ref.shape[0]
    part = tm // OUTPROJ_PARTS
    for p in range(OUTPROJ_PARTS):
        rows = slice(p * part, (p + 1) * part)
        mix = jnp.dot(cat_ref[rows, :], w_ref[0], preferred_element_type=F32)
        x1 = x_ref[0, rows, :] + mod[2:3] * mix
        x1_ref[0, rows, 0:D_MODEL] = x1
        h2 = _modulated_norm(x1, g2_ref[...], mod[3:4], mod[4:5])
        x1_ref[0, rows, D_MODEL:2 * D_MODEL] = h2
        h_hi, h_lo = _split_bf16(h2)
        nt = (((1,), (1,)), ((), ()))
        by_hi = lax.dot_general(wr_ref[...], h_hi, nt, preferred_element_type=F32)
        by_lo = lax.dot_general(wr_ref[0:N_EXPERTS, :], h_lo, nt, preferred_element_type=F32)
        logits = by_hi[0:N_EXPERTS] + by_hi[N_EXPERTS:] + by_lo
        scores = _sigmoid(logits)
        route = _route(scores, scores + br_ref[...])
        route_ref[0, :, rows] = route
        mod_row = jnp.full((1, part), row0, jnp.int32) + row_step * pl.program_id(0)
        meta = jnp.concatenate([route[0:META_MOD_ROW], mod_row.astype(F32),
                                jnp.zeros((META_LANES - META_MOD_ROW - 1, part), F32)], axis=0)
        x1_ref[0, rows, 2 * D_MODEL:] = meta.T


def _outproj(conv, attn, lru, x, mods, row0, row_step, g2, w_out_bf, w_router_t, b_router, tm, layer,
             cast_weights):
    bsz, seq, _ = x.shape
    tok = lambda width: pl.BlockSpec((1, tm, width), lambda b, i: (b, i, 0))
    in_specs = [
        tok(D_CONV), tok(D_ATTN), tok(D_LRU), tok(D_MODEL),
        pl.BlockSpec((1, 6, D_MODEL), lambda b, i: (row0 + row_step * b, 0, 0)),
        _const_spec((1, D_MODEL)),
        _layer_spec((D_MODEL, D_MODEL), layer),
        _const_spec((2 * N_EXPERTS, D_MODEL)),
        _const_spec((N_EXPERTS, 1)),
    ]
    args = [conv, attn, lru, x, mods, g2.reshape(1, D_MODEL), w_out_bf, w_router_t, b_router.reshape(N_EXPERTS, 1)]
    out_shape = [jax.ShapeDtypeStruct((bsz, seq, ROW_LANES), F32),
                 jax.ShapeDtypeStruct((bsz, ROUTE_ROWS, seq), F32)]
    out_specs = [tok(ROW_LANES), pl.BlockSpec((1, ROUTE_ROWS, tm), lambda b, i: (b, 0, i))]
    n_inner = seq // tm
    for w in cast_weights:
        src, dst, shape = _cast_specs(w, layer, n_inner, bsz * n_inner)
        in_specs.append(src)
        args.append(w)
        out_specs.append(dst)
        out_shape.append(shape)
    outs = pl.pallas_call(
        functools.partial(_outproj_kernel, row0=row0, row_step=row_step, n_cast=len(cast_weights)),
        out_shape=out_shape,
        grid=(bsz, n_inner),
        in_specs=in_specs,
        out_specs=out_specs,
        scratch_shapes=[pltpu.VMEM((tm, D_MODEL), BF16)],
        compiler_params=_cparams("parallel", "parallel"),
        name="outproj_router",
    )(*args)
    return outs[0], outs[1], outs[2:]


MOE_TILE = 256
MOE_UNROLL = 8
MOE_PARTS = 1


def _moe_plan(bucket, t_ctx, tm):
    t_all = bucket.shape[0]
    n_tiles = (t_all + N_BUCKETS * (tm - 1) + tm - 1) // tm
    ids = jnp.arange(N_BUCKETS, dtype=jnp.int32)
    order = jnp.argsort(bucket, stable=True).astype(jnp.int32)
    member = (bucket[None, :] == ids[:, None]).astype(jnp.int32)
    counts = jnp.sum(member, axis=1)
    counts_ctx = jnp.sum(member[:, :t_ctx], axis=1)
    tiles_per = (counts + tm - 1) // tm
    tile_end = jnp.cumsum(tiles_per)
    tile_start = tile_end - tiles_per
    sorted_start = jnp.cumsum(counts) - counts
    n_used = tile_end[-1]
    j = jnp.arange(n_tiles, dtype=jnp.int32)
    jj = jnp.minimum(j, n_used - 1)
    b_of = jnp.sum((tile_end[None, :] <= jj[:, None]).astype(jnp.int32), axis=1)
    local = jj - tile_start[b_of]
    n_valid = jnp.where(j < n_used, jnp.clip(counts[b_of] - local * tm, 0, tm), 0)
    n_ctx = jnp.clip(counts_ctx[b_of] - local * tm, 0, n_valid)
    first = jnp.where(j < n_used, sorted_start[b_of] + local * tm, 0)
    group, pair = b_of // N_PAIRS, b_of % N_PAIRS
    e_lo = group * EXPERTS_PER_GROUP + jnp.asarray(PAIR_LO, jnp.int32)[pair]
    e_hi = group * EXPERTS_PER_GROUP + jnp.asarray(PAIR_HI, jnp.int32)[pair]
    i32 = lambda a: a.astype(jnp.int32)
    return order, i32(first), i32(e_lo), i32(e_hi), i32(n_valid), i32(n_ctx)


def _moe_kernel(order_ref, first_ref, elo_ref, ehi_ref, nv_ref, nc_ref,
                xc_ref, xl_ref, mods_ref,
                wg_lo, wu_lo, wd_lo, wg_hi, wu_hi, wd_hi,
                oc_ref, ol_ref, xbuf, ybuf, gsem, ssem, *, tm, t_ctx, n_mod_rows):
    i = pl.program_id(0)
    n = pl.num_programs(0)
    slot = lax.rem(i, 2)

    def gather(hbm, t, r, s, size):
        return pltpu.make_async_copy(hbm.at[pl.ds(t, size), :], xbuf.at[s, pl.ds(r, size), :], gsem.at[s])

    def scatter(hbm, t, r, s, size):
        return pltpu.make_async_copy(ybuf.at[s, pl.ds(r, size), :], hbm.at[pl.ds(t, size), :], ssem.at[s])

    def start_rows(j, s, copy, hbm_ctx, hbm_lat):
        base = first_ref[j]

        def ctx_row(r):
            copy(hbm_ctx, order_ref[base + r], r, s, 1).start()

        def lat_row(r):
            copy(hbm_lat, order_ref[base + r] - t_ctx, r, s, 1).start()

        def rows(lo, hi, one_row):
            groups = lax.shift_right_logical(hi - lo, MOE_UNROLL.bit_length() - 1)

            def group(g, c):
                for u in range(MOE_UNROLL):
                    one_row(lo + g * MOE_UNROLL + u)
                return c

            def single(r, c):
                one_row(r)
                return c

            lax.fori_loop(0, groups, group, 0)
            lax.fori_loop(lo + groups * MOE_UNROLL, hi, single, 0)

        rows(0, nc_ref[j], ctx_row)
        rows(nc_ref[j], nv_ref[j], lat_row)

    def wait_rows(j, s, copy, hbm):
        cnt = nv_ref[j]
        bulk = pl.multiple_of(lax.shift_left(lax.shift_right_logical(cnt, 3), 3), SUB)

        @pl.when(bulk > 0)
        def _():
            copy(hbm, 0, 0, s, bulk).wait()

        def one(r, c):
            copy(hbm, 0, 0, s, 1).wait()
            return c

        lax.fori_loop(bulk, cnt, one, 0)

    @pl.when(i == 0)
    def _():
        xbuf[...] = jnp.zeros_like(xbuf)
        start_rows(0, 0, gather, xc_ref, xl_ref)

    @pl.when(i + 1 < n)
    def _():
        start_rows(i + 1, 1 - slot, gather, xc_ref, xl_ref)

    wait_rows(i, slot, gather, xl_ref)

    @pl.when(i >= 2)
    def _():
        wait_rows(i - 2, slot, scatter, ol_ref)

    def tile_part(rows):
        x = xbuf[slot, rows, 0:D_MODEL]
        h = xbuf[slot, rows, D_MODEL:2 * D_MODEL].astype(BF16)
        meta = xbuf[slot, rows, 2 * D_MODEL:]
        w_lo = meta[:, META_W_LO:META_W_LO + 1]
        w_hi = meta[:, META_W_HI:META_W_HI + 1]
        mrow = meta[:, META_MOD_ROW:META_MOD_ROW + 1]
        n_rows = x.shape[0]

        def per_row(*ks):
            ids = jnp.broadcast_to(mrow, (n_rows, 128))
            masks = [ids == float(r) for r in range(1, n_mod_rows)]
            cols = [[] for _ in ks]
            for c in range(D_MODEL // 128):
                lanes = slice(c * 128, (c + 1) * 128)
                for q, k in enumerate(ks):
                    v = jnp.broadcast_to(mods_ref[0, k:k + 1, lanes], (n_rows, 128))
                    for r, mask in enumerate(masks, 1):
                        v = jnp.where(mask, mods_ref[r, k:k + 1, lanes], v)
                    cols[q].append(v)
            return [jnp.concatenate(col, axis=1) for col in cols]

        def expert(wg, wu, w):
            act = _silu(jnp.dot(h, wg[0], preferred_element_type=F32))
            act = act * jnp.dot(h, wu[0], preferred_element_type=F32) * w
            return act.astype(BF16)

        y = jnp.dot(expert(wg_lo, wu_lo, w_lo), wd_lo[0], preferred_element_type=F32)
        y = y + jnp.dot(expert(wg_hi, wu_hi, w_hi), wd_hi[0], preferred_element_type=F32)
        (gate,) = per_row(5)
        ybuf[slot, rows, :] = x + gate * y

    @pl.when(nv_ref[i] > 0)
    def _():
        part = tm // MOE_PARTS
        for p in range(MOE_PARTS):
            tile_part(slice(p * part, (p + 1) * part))

    start_rows(i, slot, scatter, oc_ref, ol_ref)

    @pl.when(i == n - 1)
    def _():
        wait_rows(i, slot, scatter, ol_ref)

        @pl.when(i >= 1)
        def _():
            wait_rows(i - 1, 1 - slot, scatter, ol_ref)


def _moe(x1c, x1l, route_c, route_l, mods, wg, wu, wd, n_mod_rows):
    t_ctx, t_lat = x1c.shape[0], x1l.shape[0]
    tm = MOE_TILE
    bucket = jnp.concatenate([route_c[:, 0, :].reshape(t_ctx), route_l[:, 0, :].reshape(t_lat)])
    order, first, e_lo, e_hi, n_valid, n_ctx = _moe_plan(bucket.astype(jnp.int32), t_ctx, tm)
    n_tiles = n_valid.shape[0]
    w_up = lambda sel: pl.BlockSpec((1, D_MODEL, D_EXPERT),
                                    lambda i, order, first, lo, hi, nv, nc: ((lo, hi)[sel][i], 0, 0))
    w_dn = lambda sel: pl.BlockSpec((1, D_EXPERT, D_MODEL),
                                    lambda i, order, first, lo, hi, nv, nc: ((lo, hi)[sel][i], 0, 0))
    hbm = pl.BlockSpec(memory_space=pl.ANY)
    return pl.pallas_call(
        functools.partial(_moe_kernel, tm=tm, t_ctx=t_ctx, n_mod_rows=n_mod_rows),
        out_shape=[jax.ShapeDtypeStruct((t_ctx, D_MODEL), F32), jax.ShapeDtypeStruct((t_lat, D_MODEL), F32)],
        grid_spec=pltpu.PrefetchScalarGridSpec(
            num_scalar_prefetch=6,
            grid=(n_tiles,),
            in_specs=[
                hbm, hbm,
                _const_spec((MOD_ROWS, 6, D_MODEL)),
                w_up(0), w_up(0), w_dn(0), w_up(1), w_up(1), w_dn(1),
            ],
            out_specs=[hbm, hbm],
            scratch_shapes=[
                pltpu.VMEM((2, tm, ROW_LANES), F32), pltpu.VMEM((2, tm, D_MODEL), F32),
                pltpu.SemaphoreType.DMA((2,)), pltpu.SemaphoreType.DMA((2,)),
            ],
        ),
        compiler_params=_cparams("arbitrary"),
        name="moe_pairs",
    )(order, first, e_lo, e_hi, n_valid, n_ctx, x1c, x1l, mods, wg, wu, wd, wg, wu, wd)


PROJ_TILE = 512


def _mixers(x, p, mods, row0, row_step, h0, layer, casts_in, casts_attn, k_ctx=None, v_ctx=None):
    latent = k_ctx is not None
    bsz, seq, _ = x.shape
    tm = PROJ_TILE if (row_step == 0 or seq % PROJ_TILE == 0) else seq
    rows = bsz * seq
    flat = lambda a: a.reshape(1 if row_step == 0 else bsz, rows if row_step == 0 else seq, a.shape[-1])
    (z, q, k, v, lg, lx), cast_a = _inproj(flat(x), mods, row0, row_step, p["norm1_g"], p["w_in"], p["q_norm_g"],
                                           p["k_norm_g"], latent, tm, layer, casts_in)
    seqs = lambda a: a.reshape(bsz, seq, a.shape[-1])
    z, q, k, v, lg, lx = map(seqs, (z, q, k, v, lg, lx))
    conv = _conv_module(z, p["conv_dw"], p["conv_b"], p["conv_ln_g"], p["conv_ln_b"])
    attn, cast_b = _attention(q, k, v, p["attn_sink"], k_ctx, v_ctx, layer, casts_attn)
    lru, fin = _recurrent_mixer(lx, lg, h0, p["lru_conv_w"], p["lru_conv_b"], p["lru_wbd"], p["lru_gbias"],
                                p["lru_lam"])
    x1, route, _ = _outproj(flat(conv), flat(attn), flat(lru), flat(x), mods, row0, row_step, p["norm2_g"],
                            p["w_out"], p["w_router_t"], p["b_router"], tm, layer, [])
    return x1, route, k, v, fin, list(cast_a) + list(cast_b)


def kernel(x_prompt, x_sample, c, cache_k, cache_v, state_lru, c_ctx, w_mod, b_mod, norm1_g, norm2_g, w_in,
           conv_dw, conv_b, conv_ln_g, conv_ln_b, q_norm_g, k_norm_g, attn_sink, lru_conv_w, lru_conv_b,
           lru_wa, lru_ba, lru_wx, lru_bx, lru_lam, w_out, w_router, b_router, w_gate_e, w_up_e, w_down_e):
    bsz, seq, _ = x_prompt.shape
    dec_bsz, dec_seq, _ = x_sample.shape
    past = cache_k.shape[2]

    cvec = jnp.zeros((MOD_ROWS, D_MODEL), F32).at[0].set(c_ctx).at[1:1 + dec_bsz].set(c)
    mods_all = _modulation(cvec, w_mod, b_mod).reshape(DEPTH, MOD_ROWS, 6, D_MODEL)

    wr_hi = w_router.T.astype(BF16)
    wr_lo = (w_router.T - wr_hi.astype(F32)).astype(BF16)
    w_router_t = jnp.concatenate([wr_hi, wr_lo], axis=0)
    w_in_bf, w_out_bf = w_in.astype(BF16), w_out.astype(BF16)
    layers = []
    for l in range(DEPTH):
        wbd, gbias = _lru_gate_weights(lru_wa[l], lru_ba[l], lru_wx[l], lru_bx[l])
        layers.append({
            "norm1_g": norm1_g[l], "norm2_g": norm2_g[l], "w_in": w_in_bf,
            "conv_dw": conv_dw[l], "conv_b": conv_b[l], "conv_ln_g": conv_ln_g[l], "conv_ln_b": conv_ln_b[l],
            "q_norm_g": q_norm_g[l], "k_norm_g": k_norm_g[l], "attn_sink": attn_sink[l],
            "lru_conv_w": lru_conv_w[l], "lru_conv_b": lru_conv_b[l], "lru_wbd": wbd, "lru_gbias": gbias,
            "lru_lam": lru_lam[l],
            "w_out": w_out_bf, "w_router_t": w_router_t, "b_router": b_router,
        })

    y, z = x_prompt, x_sample
    ks, vs, hs = [], [], []
    h0_ctx = jnp.zeros((bsz, 2, D_LRU), F32)
    for l in range(DEPTH):
        p, mods = layers[l], mods_all[l]
        y1, route_c, k_l, v_l, h_l, _ = _mixers(y, p, mods, 0, 0, h0_ctx, l, [], [])
        ks.append(k_l.reshape(bsz, seq, N_KV_HEADS, HEAD_DIM))
        vs.append(v_l.reshape(bsz, seq, N_KV_HEADS, HEAD_DIM))
        hs.append(h_l)
        z1, route_l, _, _, _, (wu_bf, wg_bf, wd_bf) = _mixers(
            z, p, mods, 1, 1, state_lru[:, l], l, [w_up_e], [w_gate_e, w_down_e],
            cache_k[:, l].reshape(dec_bsz, past, D_KV).astype(BF16),
            cache_v[:, l].reshape(dec_bsz, past, D_KV).astype(BF16))
        y, z = _moe(y1.reshape(bsz * seq, -1), z1.reshape(dec_bsz * dec_seq, -1), route_c, route_l,
                    mods, wg_bf, wu_bf, wd_bf, 1 + dec_bsz)
        y, z = y.reshape(bsz, seq, D_MODEL), z.reshape(dec_bsz, dec_seq, D_MODEL)
    new_cache_k = jnp.stack(ks, axis=1)
    new_cache_v = jnp.stack(vs, axis=1)
    new_state_lru = jnp.stack(hs, axis=1)
    return y, z, new_cache_k, new_cache_v, new_state_lru
```

```python
import functools

import numpy as np
import jax
import jax.numpy as jnp
from jax import lax
from jax.experimental import pallas as pl
from jax.experimental.pallas import tpu as pltpu

D_MODEL = 2048
DEPTH = 2
GRID_W = 64
D_CONV = 512
N_HEADS = 8
N_KV_HEADS = 2
HEAD_DIM = 128
GROUP = N_HEADS // N_KV_HEADS
D_ATTN = N_HEADS * HEAD_DIM
D_KV = N_KV_HEADS * HEAD_DIM
D_LRU = 512
CONV_K = 31
LRU_CONV_K = 4
LRU_BLOCKS = 8
LRU_BLK = D_LRU // LRU_BLOCKS
LRU_C = 8.0
WINDOW = 128
ROPE_BASE = 10000.0
ATTN_SCALE = HEAD_DIM ** -0.5
N_EXPERTS = 16
N_EXPERT_GROUPS = 4
EXPERTS_PER_GROUP = N_EXPERTS // N_EXPERT_GROUPS
D_EXPERT = 512
EPS = 1e-6
NEG = -1e30
D_IN = 2 * D_CONV + D_ATTN + 2 * D_KV + 2 * D_LRU
N_MOD = 6 * D_MODEL
SUB = 8
MOD_ROWS = SUB

V7X_VMEM_BYTES = 64 * 1024 * 1024
VMEM_LIMIT = V7X_VMEM_BYTES - 8 * 1024 * 1024

F32 = jnp.float32
BF16 = jnp.bfloat16
HIGHEST = lax.Precision.HIGHEST


def _cparams(*sem):
    return pltpu.CompilerParams(dimension_semantics=sem, vmem_limit_bytes=VMEM_LIMIT)


def _sigmoid(x):
    return 0.5 * jnp.tanh(0.5 * x) + 0.5


def _silu(x):
    return x * _sigmoid(x)


def _const_spec(shape):
    return pl.BlockSpec(shape, lambda *_: (0,) * len(shape), pipeline_mode=pl.Buffered(1))


def _layer_spec(shape, layer):
    return pl.BlockSpec((1,) + shape, lambda *_: (layer,) + (0,) * len(shape), pipeline_mode=pl.Buffered(1))


def _split_bf16(x):
    hi = x.astype(BF16)
    return hi, (x - hi.astype(F32)).astype(BF16)


def _mod_kernel(c_ref, w_ref, b_ref, o_ref):
    s = _silu(c_ref[...])
    s_hi = s.astype(BF16).astype(F32)
    s_both = jnp.concatenate([s_hi, s - s_hi], axis=0).astype(BF16)
    w_hi, w_lo = _split_bf16(w_ref[0])
    by_hi = jnp.dot(s_both, w_hi, preferred_element_type=F32)
    by_lo = jnp.dot(s_both, w_lo, preferred_element_type=F32)
    o_ref[0] = by_hi[0:MOD_ROWS] + by_hi[MOD_ROWS:] + by_lo[0:MOD_ROWS] + b_ref[0]


def _modulation(cvec, w_mod, b_mod):
    tn = 1536
    return pl.pallas_call(
        _mod_kernel,
        out_shape=jax.ShapeDtypeStruct((DEPTH, MOD_ROWS, N_MOD), F32),
        grid=(DEPTH, N_MOD // tn),
        in_specs=[
            _const_spec((MOD_ROWS, D_MODEL)),
            pl.BlockSpec((1, D_MODEL, tn), lambda l, j: (l, 0, j)),
            pl.BlockSpec((1, 1, tn), lambda l, j: (l, 0, j)),
        ],
        out_specs=pl.BlockSpec((1, MOD_ROWS, tn), lambda l, j: (l, 0, j)),
        compiler_params=_cparams("parallel", "parallel"),
        name="modulation",
    )(cvec, w_mod, b_mod.reshape(DEPTH, 1, N_MOD))


def _head_norm(x, g):
    return x * lax.rsqrt(jnp.mean(x * x, axis=-1, keepdims=True) + EPS) * g


def _rope(x, cos, sin_signed):
    lane = lax.broadcasted_iota(jnp.int32, x.shape, 1)
    partner = jnp.where((lane % 64) < 32, pltpu.roll(x, 96, 1), pltpu.roll(x, 32, 1))
    return x * cos + partner * sin_signed


def _inproj_kernel(*refs, rope, n_cast):
    refs = list(refs)
    cast_dst = [refs.pop() for _ in range(n_cast)][::-1]
    z_ref, q_ref, k_ref, v_ref, lg_ref, lx_ref = refs[-6:]
    del refs[-6:]
    cast_src = [refs.pop() for _ in range(n_cast)][::-1]
    if rope:
        x_ref, mod_ref, g1_ref, w_ref, qg_ref, kg_ref, cos_ref, sin_ref = refs
    else:
        x_ref, mod_ref, g1_ref, w_ref, qg_ref, kg_ref = refs
    for src, dst in zip(cast_src, cast_dst):
        dst[0] = src[0, 0].astype(BF16)
    x = x_ref[0]
    mod = mod_ref[0]
    shift, scale = mod[0:1], mod[1:2]
    y = x * lax.rsqrt(jnp.mean(x * x, axis=-1, keepdims=True) + EPS) * g1_ref[...]
    h = (y * (1.0 + scale) + shift).astype(BF16)

    def proj(c0, c1):
        return jnp.dot(h, w_ref[0, :, c0:c1], preferred_element_type=F32)

    glu = proj(0, 2 * D_CONV)
    z_ref[0] = glu[:, :D_CONV] * _sigmoid(glu[:, D_CONV:])

    c0 = 2 * D_CONV
    q = proj(c0, c0 + D_ATTN)
    qg = qg_ref[...]
    for hd in range(N_HEADS):
        qh = _head_norm(q[:, hd * HEAD_DIM:(hd + 1) * HEAD_DIM], qg)
        if rope:
            qh = _rope(qh, cos_ref[...], sin_ref[...])
        q_ref[0, :, hd * HEAD_DIM:(hd + 1) * HEAD_DIM] = (qh * ATTN_SCALE).astype(BF16)

    c0 += D_ATTN
    kv = proj(c0, c0 + 2 * D_KV)
    kg = kg_ref[...]
    for hd in range(N_KV_HEADS):
        kh = _head_norm(kv[:, hd * HEAD_DIM:(hd + 1) * HEAD_DIM], kg)
        if rope:
            kh = _rope(kh, cos_ref[...], sin_ref[...])
        k_ref[0, :, hd * HEAD_DIM:(hd + 1) * HEAD_DIM] = kh.astype(k_ref.dtype)
    v_ref[0] = kv[:, D_KV:].astype(v_ref.dtype)

    c0 += 2 * D_KV
    l2 = proj(c0, c0 + 2 * D_LRU)
    lg_ref[0] = l2[:, :D_LRU]
    lx_ref[0] = l2[:, D_LRU:]


def _rope_tables(seq):
    n_freq = HEAD_DIM // 4
    inv = (ROPE_BASE ** (-np.arange(n_freq, dtype=np.float32) / n_freq)).astype(np.float32)
    t = np.arange(seq)
    ang_r = (t // GRID_W).astype(np.float32)[:, None] * inv[None, :]
    ang_c = (t % GRID_W).astype(np.float32)[:, None] * inv[None, :]
    cos = np.concatenate([np.cos(ang_r)] * 2 + [np.cos(ang_c)] * 2, axis=-1)
    sin = np.concatenate([-np.sin(ang_r), np.sin(ang_r), -np.sin(ang_c), np.sin(ang_c)], axis=-1)
    return jnp.asarray(cos, F32), jnp.asarray(sin, F32)


def _add_casts(cast_weights, n_inner, n_steps, in_specs, args, out_specs, out_shape):
    for w, layer in cast_weights:
        _, n_g, rows, cols = w.shape
        parts = n_steps // n_g
        assert parts * n_g == n_steps and rows % parts == 0
        piece = lambda b, i, parts=parts: divmod(b * n_inner + i, parts)
        in_specs.append(pl.BlockSpec((1, 1, rows // parts, cols),
                                     lambda b, i, layer=layer, piece=piece: (layer, *piece(b, i), 0)))
        args.append(w)
        out_specs.append(pl.BlockSpec((1, rows // parts, cols), lambda b, i, piece=piece: (*piece(b, i), 0)))
        out_shape.append(jax.ShapeDtypeStruct((n_g, rows, cols), BF16))


def _inproj(x, mods, row0, row_step, g1, w_in_bf, qg, kg, rope, tm, layer, cast_weights):
    bsz, seq, _ = x.shape
    tok = lambda width: pl.BlockSpec((1, tm, width), lambda b, i: (b, i, 0))
    in_specs = [
        tok(D_MODEL),
        pl.BlockSpec((1, 6, D_MODEL), lambda b, i: (row0 + row_step * b, 0, 0)),
        _const_spec((1, D_MODEL)),
        _layer_spec((D_MODEL, D_IN), layer),
        _const_spec((1, HEAD_DIM)),
        _const_spec((1, HEAD_DIM)),
    ]
    args = [x, mods, g1.reshape(1, D_MODEL), w_in_bf, qg.reshape(1, HEAD_DIM), kg.reshape(1, HEAD_DIM)]
    if rope:
        cos, sin = _rope_tables(seq)
        in_specs += [pl.BlockSpec((tm, HEAD_DIM), lambda b, i: (i, 0))] * 2
        args += [cos, sin]
    widths = (D_CONV, D_ATTN, D_KV, D_KV, D_LRU, D_LRU)
    kv_dtype = BF16 if rope else F32
    dtypes = (F32, BF16, kv_dtype, kv_dtype, F32, F32)
    out_shape = [jax.ShapeDtypeStruct((bsz, seq, w), dt) for w, dt in zip(widths, dtypes)]
    out_specs = [tok(w) for w in widths]
    n_inner = seq // tm
    _add_casts(cast_weights, n_inner, bsz * n_inner, in_specs, args, out_specs, out_shape)
    outs = pl.pallas_call(
        functools.partial(_inproj_kernel, rope=rope, n_cast=len(cast_weights)),
        out_shape=out_shape,
        grid=(bsz, n_inner),
        in_specs=in_specs,
        out_specs=out_specs,
        compiler_params=_cparams("parallel", "parallel"),
        name="inproj_rope" if rope else "inproj",
    )(*args)
    return outs[:6], outs[6:]


CONV_PAD = (CONV_K - 1) // 2
CONV_HALO = 16
CONV_CHUNK = 64
CONV_COLS = 256
CONV_GROUP = 4


def _conv_kernel(z_ref, w_ref, b_ref, g_ref, beta_ref, o_ref, zp_ref, win_ref, acc_ref, *, seq):
    zeros = jnp.zeros((CONV_HALO, D_CONV), F32)
    zp_ref[0:CONV_HALO, :] = zeros
    zp_ref[CONV_HALO + seq:2 * CONV_HALO + seq, :] = zeros
    zp_ref[CONV_HALO:CONV_HALO + seq, :] = z_ref[0]
    n_parts = D_CONV // CONV_COLS

    group_rows = CONV_GROUP * CONV_CHUNK

    def body(c, carry):
        r0 = pl.multiple_of(c * group_rows, group_rows)
        for q in range(CONV_GROUP):
            for p in range(n_parts):
                win_ref[q * n_parts + p] = zp_ref[pl.ds(r0 + q * CONV_CHUNK, CONV_CHUNK + 2 * CONV_HALO),
                                                  p * CONV_COLS:(p + 1) * CONV_COLS]

        def taps(j, inner):
            p = lax.rem(j, n_parts)
            part = jnp.broadcast_to(b_ref[p], (CONV_CHUNK, CONV_COLS))
            for k in range(CONV_K):
                off = CONV_HALO - CONV_PAD + k
                part = part + jnp.tile(w_ref[p, k], (CONV_CHUNK // SUB, 1)) * win_ref[j, off:off + CONV_CHUNK, :]
            acc_ref[j] = part
            return inner

        lax.fori_loop(0, CONV_GROUP * n_parts, taps, 0)
        acc = jnp.concatenate(
            [jnp.concatenate([acc_ref[q * n_parts + p] for p in range(n_parts)], axis=1) for q in range(CONV_GROUP)],
            axis=0)
        mu = jnp.mean(acc, axis=-1, keepdims=True)
        xc = acc - mu
        var = jnp.mean(xc * xc, axis=-1, keepdims=True)
        y = xc * lax.rsqrt(var + EPS) * g_ref[...] + beta_ref[...]
        o_ref[0, pl.ds(r0, group_rows), :] = _silu(y).astype(BF16)
        return carry

    lax.fori_loop(0, seq // group_rows, body, 0)


def _conv_module(z, w, b, g, beta):
    bsz, seq, _ = z.shape
    row = lambda a: a.reshape(1, D_CONV)
    n_parts = D_CONV // CONV_COLS
    w_parts = jnp.broadcast_to(w.reshape(CONV_K, 1, n_parts, CONV_COLS), (CONV_K, SUB, n_parts, CONV_COLS))
    w_parts = w_parts.transpose(2, 0, 1, 3)
    return pl.pallas_call(
        functools.partial(_conv_kernel, seq=seq),
        out_shape=jax.ShapeDtypeStruct((bsz, seq, D_CONV), BF16),
        grid=(bsz,),
        in_specs=[
            pl.BlockSpec((1, seq, D_CONV), lambda i: (i, 0, 0)),
            _const_spec((n_parts, CONV_K, SUB, CONV_COLS)),
            _const_spec((n_parts, 1, CONV_COLS)), _const_spec((1, D_CONV)), _const_spec((1, D_CONV)),
        ],
        out_specs=pl.BlockSpec((1, seq, D_CONV), lambda i: (i, 0, 0)),
        scratch_shapes=[pltpu.VMEM((seq + 2 * CONV_HALO, D_CONV), F32),
                        pltpu.VMEM((CONV_GROUP * n_parts, CONV_CHUNK + 2 * CONV_HALO, CONV_COLS), F32),
                        pltpu.VMEM((CONV_GROUP * n_parts, CONV_CHUNK, CONV_COLS), F32)],
        compiler_params=_cparams("parallel"),
        name="conv_module",
    )(z, w_parts, b.reshape(n_parts, 1, CONV_COLS), row(g), row(beta))


ATTN_ROWS = 64


def _attn_kernel(*refs, latent, seq, tq, n_cast):
    n_in = 6 if latent else 4
    cast_src = refs[n_in:n_in + n_cast]
    cast_dst = refs[n_in + n_cast + 1:n_in + 2 * n_cast + 1]
    refs = refs[:n_in] + refs[n_in + n_cast:n_in + n_cast + 1] + refs[n_in + 2 * n_cast + 1:]
    if latent:
        sink_ref, q_ref, k_ref, v_ref, kc_ref, vc_ref, o_ref, s_ref, p_ref, bias_ref = refs
    else:
        sink_ref, q_ref, k_ref, v_ref, o_ref, s_ref, p_ref = refs
    for src, dst in zip(cast_src, cast_dst):
        dst[0] = src[0, 0].astype(BF16)
    q = q_ref[0]
    n_loc = 3 * tq if latent else 0
    if latent:
        n = pl.program_id(1)
        nblk = seq // tq
        blocks = (jnp.maximum(n - 1, 0), n, jnp.minimum(n + 1, nblk - 1))

        def window(ref):
            parts = [ref[0, pl.ds(pl.multiple_of(i * tq, tq), tq), :] for i in blocks]
            return jnp.concatenate(parts, axis=0)

        k_all = jnp.concatenate([window(k_ref), kc_ref[0]], axis=0).astype(BF16)
        v_all = jnp.concatenate([window(v_ref), vc_ref[0]], axis=0).astype(BF16)
        qpos = n * tq + lax.broadcasted_iota(jnp.int32, (tq, 3 * tq), 0)
        kpos = (n - 1) * tq + lax.broadcasted_iota(jnp.int32, (tq, 3 * tq), 1)
        ok = (jnp.abs(qpos - kpos) <= WINDOW) & (kpos >= 0) & (kpos < seq)
        bias_ref[...] = jnp.where(ok, 0.0, NEG).astype(F32)
    else:
        k_all = k_ref[0].astype(BF16)
        v_all = v_ref[0].astype(BF16)

    for j in range(N_KV_HEADS):
        heads = [j * GROUP + g for g in range(GROUP)]
        qs = jnp.concatenate([q[:, h * HEAD_DIM:(h + 1) * HEAD_DIM] for h in heads], axis=0)
        kj = k_all[:, j * HEAD_DIM:(j + 1) * HEAD_DIM]
        vj = v_all[:, j * HEAD_DIM:(j + 1) * HEAD_DIM]
        s_ref[...] = lax.dot_general(qs, kj, (((1,), (1,)), ((), ())), preferred_element_type=F32)
        inv = []
        for rb in range(GROUP * tq // ATTN_ROWS):
            rows = slice(rb * ATTN_ROWS, (rb + 1) * ATTN_ROWS)
            sk = sink_ref[heads[rb * ATTN_ROWS // tq]]
            parts = [s_ref[rows, n_loc:]]
            if latent:
                q0 = rb * ATTN_ROWS % tq
                parts.insert(0, s_ref[rows, :n_loc] + bias_ref[q0:q0 + ATTN_ROWS, :])
            m = sk
            for s in parts:
                m = jnp.maximum(m, jnp.max(s, axis=-1, keepdims=True))
            den = jnp.exp(sk - m)
            c0 = 0
            for s in parts:
                p = jnp.exp(s - m)
                den = den + jnp.sum(p, axis=-1, keepdims=True)
                p_ref[rows, c0:c0 + s.shape[1]] = p.astype(BF16)
                c0 += s.shape[1]
            inv.append(1.0 / den)
        o = jnp.dot(p_ref[...], vj, preferred_element_type=F32) * jnp.concatenate(inv, axis=0)
        for g, h in enumerate(heads):
            o_ref[0, :, h * HEAD_DIM:(h + 1) * HEAD_DIM] = o[g * tq:(g + 1) * tq].astype(BF16)


def _attention(q, k, v, sink, k_ctx=None, v_ctx=None, layer=0, cast_weights=()):
    bsz, seq, _ = q.shape
    latent = k_ctx is not None
    tq = WINDOW if latent else seq
    seq_spec = pl.BlockSpec((1, seq, D_KV), lambda b, i: (b, 0, 0))
    in_specs = [
        pl.BlockSpec(memory_space=pltpu.SMEM),
        pl.BlockSpec((1, tq, D_ATTN), lambda b, i: (b, i, 0)),
        seq_spec, seq_spec,
    ]
    args = [sink, q, k, v]
    n_keys = seq
    if latent:
        past = k_ctx.shape[1]
        ctx_spec = pl.BlockSpec((1, past, D_KV), lambda b, i: (b, 0, 0))
        in_specs += [ctx_spec, ctx_spec]
        args += [k_ctx, v_ctx]
        n_keys = 3 * tq + past
    scratch = [pltpu.VMEM((GROUP * tq, n_keys), F32), pltpu.VMEM((GROUP * tq, n_keys), BF16)]
    if latent:
        scratch.append(pltpu.VMEM((tq, 3 * tq), F32))
    out_shape = [jax.ShapeDtypeStruct((bsz, seq, D_ATTN), BF16)]
    out_specs = [pl.BlockSpec((1, tq, D_ATTN), lambda b, i: (b, i, 0))]
    n_inner = seq // tq
    _add_casts(cast_weights, n_inner, bsz * n_inner, in_specs, args, out_specs, out_shape)
    outs = pl.pallas_call(
        functools.partial(_attn_kernel, latent=latent, seq=seq, tq=tq, n_cast=len(cast_weights)),
        out_shape=out_shape,
        grid=(bsz, n_inner),
        in_specs=in_specs,
        out_specs=out_specs,
        scratch_shapes=scratch,
        compiler_params=_cparams("parallel", "parallel"),
        name="attn_latent" if latent else "attn_context",
    )(*args)
    return outs[0], outs[1:]


LRU_HALO = 8
LRU_CHUNK = 128
LRU_HALF = D_LRU // 2


def _softplus(x):
    return jnp.maximum(x, 0.0) + jnp.log(1.0 + jnp.exp(-jnp.abs(x)))


def _gelu_tanh(x):
    return 0.5 * x * (1.0 + jnp.tanh(0.7978845608028654 * (x + 0.044715 * (x * x * x))))


def _scan_tile(a, b, carry, reverse):
    row = lax.broadcasted_iota(jnp.int32, a.shape, 0)
    for d in (1, 2, 4):
        if reverse:
            valid = row < SUB - d
            shift = SUB - d
        else:
            valid = row >= d
            shift = d
        a_prev = jnp.where(valid, pltpu.roll(a, shift, 0), 1.0)
        b_prev = jnp.where(valid, pltpu.roll(b, shift, 0), 0.0)
        b = a * b_prev + b
        a = a * a_prev
    h = a * carry + b
    last = h[0:1, :] if reverse else h[SUB - 1:SUB, :]
    return h, last


def _lru_kernel(lx_ref, lg_ref, h0_ref, cw_ref, cb_ref, wbd_ref, gbias_ref, lam_ref,
                o_ref, fin_ref, xp_ref, win_ref, af_ref, bf_ref, ab_ref, bb_ref, *, seq):
    zeros = jnp.zeros((LRU_HALO, D_LRU), F32)
    xp_ref[0:LRU_HALO, :] = zeros
    xp_ref[LRU_HALO + seq:2 * LRU_HALO + seq, :] = zeros
    xp_ref[LRU_HALO:LRU_HALO + seq, :] = lx_ref[0]
    a_refs = (af_ref, ab_ref)
    b_refs = (bf_ref, bb_ref)
    half_c = (-0.5 * LRU_C) * _softplus(-lam_ref[...])

    def gates(c, carry):
        r0 = pl.multiple_of(c * LRU_CHUNK, LRU_CHUNK)
        win_ref[...] = xp_ref[pl.ds(r0, LRU_CHUNK + 2 * LRU_HALO), :]
        xc = jnp.broadcast_to(cb_ref[...], (LRU_CHUNK, D_LRU))
        for k in range(LRU_CONV_K):
            off = LRU_HALO - 2 + k
            xc = xc + cw_ref[k:k + 1, :] * win_ref[off:off + LRU_CHUNK, :]
        for s in range(2):
            cols = slice(s * LRU_HALF, (s + 1) * LRU_HALF)
            xs = xc[:, cols]
            g = jnp.dot(xs.astype(BF16), wbd_ref[s], preferred_element_type=F32)
            g = g + gbias_ref[s:s + 1, :]
            for d in range(2):
                base = d * 2 * LRU_HALF
                hc = half_c[d:d + 1, cols]
                log_a = hc * jnp.tanh(g[:, base:base + LRU_HALF]) + hc
                i = 0.5 * jnp.tanh(g[:, base + LRU_HALF:base + 2 * LRU_HALF]) + 0.5
                a = jnp.exp(log_a)
                t = jnp.tanh(log_a)
                b = jnp.sqrt(-2.0 * t / (1.0 - t)) * (i * xs)
                a_refs[d][pl.ds(r0, LRU_CHUNK), cols] = a
                b_refs[d][pl.ds(r0, LRU_CHUNK), cols] = b
        return carry

    lax.fori_loop(0, seq // LRU_CHUNK, gates, 0)

    ntile = seq // SUB

    def scan(t, carry):
        cf, cb = carry
        rf = pl.multiple_of(t * SUB, SUB)
        rb = pl.multiple_of((ntile - 1 - t) * SUB, SUB)
        hf, cf = _scan_tile(af_ref[pl.ds(rf, SUB), :], bf_ref[pl.ds(rf, SUB), :], cf, False)
        hb, cb = _scan_tile(ab_ref[pl.ds(rb, SUB), :], bb_ref[pl.ds(rb, SUB), :], cb, True)
        bf_ref[pl.ds(rf, SUB), :] = hf
        bb_ref[pl.ds(rb, SUB), :] = hb
        return cf, cb

    h0 = h0_ref[0]
    cf, cb = lax.fori_loop(0, ntile, scan, (h0[0:1, :], h0[1:2, :]))
    fin_ref[0, 0:1, :] = cf
    fin_ref[0, 1:2, :] = cb

    def finish(c, carry):
        r0 = pl.multiple_of(c * LRU_CHUNK, LRU_CHUNK)
        rows = pl.ds(r0, LRU_CHUNK)
        o_ref[0, rows, :] = ((bf_ref[rows, :] + bb_ref[rows, :]) * _gelu_tanh(lg_ref[0, rows, :])).astype(BF16)
        return carry

    lax.fori_loop(0, seq // LRU_CHUNK, finish, 0)


def _lru_gate_weights(wa, ba, wx, bx):
    per_half = LRU_HALF // LRU_BLK
    on_diagonal = jnp.asarray(np.kron(np.eye(per_half), np.ones((LRU_BLK, LRU_BLK))), F32)

    def dense(w):
        blocks = w.reshape(2, per_half, LRU_BLK, 1, LRU_BLK)
        tiled = jnp.broadcast_to(blocks, (2, per_half, LRU_BLK, per_half, LRU_BLK))
        return tiled.reshape(2, LRU_HALF, LRU_HALF) * on_diagonal

    wbd = jnp.concatenate([dense(wa[0]), dense(wx[0]), dense(wa[1]), dense(wx[1])], axis=2)
    gbias = jnp.stack([ba[0], bx[0], ba[1], bx[1]]).reshape(4, 2, LRU_HALF).transpose(1, 0, 2)
    return (0.5 * wbd).astype(BF16), 0.5 * gbias.reshape(2, 4 * LRU_HALF)


def _recurrent_mixer(lx, lg, h0, cw, cb, wbd, gbias, lam):
    bsz, seq, _ = lx.shape
    seq_spec = pl.BlockSpec((1, seq, D_LRU), lambda i: (i, 0, 0))
    state_spec = pl.BlockSpec((1, 2, D_LRU), lambda i: (i, 0, 0))
    return pl.pallas_call(
        functools.partial(_lru_kernel, seq=seq),
        out_shape=[jax.ShapeDtypeStruct((bsz, seq, D_LRU), BF16),
                   jax.ShapeDtypeStruct((bsz, 2, D_LRU), F32)],
        grid=(bsz,),
        in_specs=[
            seq_spec, seq_spec, state_spec,
            _const_spec((LRU_CONV_K, D_LRU)), _const_spec((1, D_LRU)),
            _const_spec((2, LRU_HALF, 4 * LRU_HALF)), _const_spec((2, 4 * LRU_HALF)),
            _const_spec((2, D_LRU)),
        ],
        out_specs=[seq_spec, state_spec],
        scratch_shapes=[pltpu.VMEM((seq + 2 * LRU_HALO, D_LRU), F32),
                        pltpu.VMEM((LRU_CHUNK + 2 * LRU_HALO, D_LRU), F32)] + [pltpu.VMEM((seq, D_LRU), F32)] * 4,
        compiler_params=_cparams("parallel"),
        name="rglru",
    )(lx, lg, h0, cw, cb.reshape(1, D_LRU), wbd, gbias, lam)


N_PAIRS = EXPERTS_PER_GROUP * (EXPERTS_PER_GROUP - 1) // 2
N_BUCKETS = N_EXPERT_GROUPS * N_PAIRS
PAIR_LO = (0, 0, 0, 1, 1, 2)
PAIR_HI = (1, 2, 3, 2, 3, 3)
ROUTE_ROWS = SUB


def _route(scores, biased):
    rows = [biased[e:e + 1, :] for e in range(N_EXPERTS)]
    group_score = []
    for g in range(N_EXPERT_GROUPS):
        a, b, c, d = rows[4 * g:4 * g + 4]
        hi1, lo1 = jnp.maximum(a, b), jnp.minimum(a, b)
        hi2, lo2 = jnp.maximum(c, d), jnp.minimum(c, d)
        top = jnp.maximum(hi1, hi2)
        second = jnp.maximum(jnp.minimum(hi1, hi2), jnp.maximum(lo1, lo2))
        group_score.append(top + second)
    best = group_score[0]
    g_sel = jnp.zeros_like(best, dtype=jnp.int32)
    for g in range(1, N_EXPERT_GROUPS):
        better = group_score[g] > best
        g_sel = jnp.where(better, g, g_sel)
        best = jnp.where(better, group_score[g], best)
    sel = []
    for e in range(N_EXPERTS):
        g = e // EXPERTS_PER_GROUP
        rank = jnp.zeros_like(g_sel)
        for o in range(g * EXPERTS_PER_GROUP, (g + 1) * EXPERTS_PER_GROUP):
            if o == e:
                continue
            ahead = (rows[o] >= rows[e]) if o < e else (rows[o] > rows[e])
            rank = rank + ahead.astype(jnp.int32)
        sel.append(jnp.where(jnp.where(g_sel == g, rank, 2) < 2, 1, 0))
    zero = jnp.zeros_like(best)
    lo_w, hi_w = zero, zero
    lo_idx = jnp.zeros_like(g_sel)
    hi_idx = jnp.zeros_like(g_sel)
    for g in range(N_EXPERT_GROUPS):
        seen = jnp.zeros_like(g_sel)
        for j in range(EXPERTS_PER_GROUP):
            e = g * EXPERTS_PER_GROUP + j
            order = jnp.where(sel[e] == 1, seen, 2)
            lo_w = jnp.where(order == 0, scores[e:e + 1, :], lo_w)
            hi_w = jnp.where(order == 1, scores[e:e + 1, :], hi_w)
            lo_idx = jnp.where(order == 0, j, lo_idx)
            hi_idx = jnp.where(order == 1, j, hi_idx)
            seen = seen + sel[e]
    pair_base = jnp.where(lo_idx == 0, 0, jnp.where(lo_idx == 1, 3, 5))
    bucket = g_sel * N_PAIRS + pair_base + hi_idx - lo_idx - 1
    total = lo_w + hi_w
    pad = jnp.zeros((ROUTE_ROWS - 3, best.shape[1]), F32)
    return jnp.concatenate([bucket.astype(F32), lo_w / total, hi_w / total, pad], axis=0)


def _modulated_norm(x, g, shift, scale):
    y = x * lax.rsqrt(jnp.mean(x * x, axis=-1, keepdims=True) + EPS) * g
    return y * (1.0 + scale) + shift


META_LANES = 128
ROW_LANES = 2 * D_MODEL + META_LANES
META_W_LO, META_W_HI, META_MOD_ROW = 1, 2, 3
OUTPROJ_PARTS = 1


def _outproj_kernel(*refs, row0, row_step, n_cast):
    conv_ref, attn_ref, lru_ref, x_ref, mod_ref, g2_ref, w_ref, wr_ref, br_ref = refs[:9]
    cast_src = refs[9:9 + n_cast]
    x1_ref, route_ref = refs[9 + n_cast:11 + n_cast]
    cast_dst = refs[11 + n_cast:11 + 2 * n_cast]
    cat_ref = refs[-1]
    for src, dst in zip(cast_src, cast_dst):
        dst[0] = src[0, 0].astype(BF16)
    cat_ref[:, 0:D_CONV] = conv_ref[0]
    cat_ref[:, D_CONV:D_CONV + D_ATTN] = attn_ref[0]
    cat_ref[:, D_CONV + D_ATTN:] = lru_ref[0]
    mod = mod_ref[0]
    tm = cat_ref.shape[0]
    part = tm // OUTPROJ_PARTS
    for p in range(OUTPROJ_PARTS):
        rows = slice(p * part, (p + 1) * part)
        mix = jnp.dot(cat_ref[rows, :], w_ref[0], preferred_element_type=F32)
        x1 = x_ref[0, rows, :] + mod[2:3] * mix
        x1_ref[0, rows, 0:D_MODEL] = x1
        h2 = _modulated_norm(x1, g2_ref[...], mod[3:4], mod[4:5])
        x1_ref[0, rows, D_MODEL:2 * D_MODEL] = h2
        h_hi, h_lo = _split_bf16(h2)
        nt = (((1,), (1,)), ((), ()))
        by_hi = lax.dot_general(wr_ref[...], h_hi, nt, preferred_element_type=F32)
        by_lo = lax.dot_general(wr_ref[0:N_EXPERTS, :], h_lo, nt, preferred_element_type=F32)
        logits = by_hi[0:N_EXPERTS] + by_hi[N_EXPERTS:] + by_lo
        scores = _sigmoid(logits)
        route = _route(scores, scores + br_ref[...])
        route_ref[0, :, rows] = route
        mod_row = jnp.full((1, part), row0, jnp.int32) + row_step * pl.program_id(0)
        meta = jnp.concatenate([route[0:META_MOD_ROW], mod_row.astype(F32),
                                jnp.zeros((META_LANES - META_MOD_ROW - 1, part), F32)], axis=0)
        x1_ref[0, rows, 2 * D_MODEL:] = meta.T


def _outproj(conv, attn, lru, x, mods, row0, row_step, g2, w_out_bf, w_router_t, b_router, tm, layer,
             cast_weights):
    bsz, seq, _ = x.shape
    tok = lambda width: pl.BlockSpec((1, tm, width), lambda b, i: (b, i, 0))
    in_specs = [
        tok(D_CONV), tok(D_ATTN), tok(D_LRU), tok(D_MODEL),
        pl.BlockSpec((1, 6, D_MODEL), lambda b, i: (row0 + row_step * b, 0, 0)),
        _const_spec((1, D_MODEL)),
        _layer_spec((D_MODEL, D_MODEL), layer),
        _const_spec((2 * N_EXPERTS, D_MODEL)),
        _const_spec((N_EXPERTS, 1)),
    ]
    args = [conv, attn, lru, x, mods, g2.reshape(1, D_MODEL), w_out_bf, w_router_t, b_router.reshape(N_EXPERTS, 1)]
    out_shape = [jax.ShapeDtypeStruct((bsz, seq, ROW_LANES), F32),
                 jax.ShapeDtypeStruct((bsz, ROUTE_ROWS, seq), F32)]
    out_specs = [tok(ROW_LANES), pl.BlockSpec((1, ROUTE_ROWS, tm), lambda b, i: (b, 0, i))]
    n_inner = seq // tm
    _add_casts(cast_weights, n_inner, bsz * n_inner, in_specs, args, out_specs, out_shape)
    outs = pl.pallas_call(
        functools.partial(_outproj_kernel, row0=row0, row_step=row_step, n_cast=len(cast_weights)),
        out_shape=out_shape,
        grid=(bsz, n_inner),
        in_specs=in_specs,
        out_specs=out_specs,
        scratch_shapes=[pltpu.VMEM((tm, D_MODEL), BF16)],
        compiler_params=_cparams("parallel", "parallel"),
        name="outproj_router",
    )(*args)
    return outs[0], outs[1], outs[2:]


MOE_TILE = 256
MOE_UNROLL = 8
MOE_PARTS = 1


def _moe_plan(bucket, t_ctx, tm):
    t_all = bucket.shape[0]
    n_tiles = (t_all + N_BUCKETS * (tm - 1) + tm - 1) // tm
    ids = jnp.arange(N_BUCKETS, dtype=jnp.int32)
    order = jnp.argsort(bucket, stable=True).astype(jnp.int32)
    member = (bucket[None, :] == ids[:, None]).astype(jnp.int32)
    counts = jnp.sum(member, axis=1)
    counts_ctx = jnp.sum(member[:, :t_ctx], axis=1)
    tiles_per = (counts + tm - 1) // tm
    tile_end = jnp.cumsum(tiles_per)
    tile_start = tile_end - tiles_per
    sorted_start = jnp.cumsum(counts) - counts
    n_used = tile_end[-1]
    j = jnp.arange(n_tiles, dtype=jnp.int32)
    jj = jnp.minimum(j, n_used - 1)
    b_of = jnp.sum((tile_end[None, :] <= jj[:, None]).astype(jnp.int32), axis=1)
    local = jj - tile_start[b_of]
    n_valid = jnp.where(j < n_used, jnp.clip(counts[b_of] - local * tm, 0, tm), 0)
    n_ctx = jnp.clip(counts_ctx[b_of] - local * tm, 0, n_valid)
    first = jnp.where(j < n_used, sorted_start[b_of] + local * tm, 0)
    group, pair = b_of // N_PAIRS, b_of % N_PAIRS
    e_lo = group * EXPERTS_PER_GROUP + jnp.asarray(PAIR_LO, jnp.int32)[pair]
    e_hi = group * EXPERTS_PER_GROUP + jnp.asarray(PAIR_HI, jnp.int32)[pair]
    i32 = lambda a: a.astype(jnp.int32)
    return order, i32(first), i32(e_lo), i32(e_hi), i32(n_valid), i32(n_ctx)


def _moe_kernel(order_ref, first_ref, elo_ref, ehi_ref, nv_ref, nc_ref,
                xc_ref, xl_ref, mods_ref,
                wg_lo, wu_lo, wd_lo, wg_hi, wu_hi, wd_hi,
                oc_ref, ol_ref, xbuf, ybuf, gsem, ssem, *, tm, t_ctx, n_mod_rows):
    i = pl.program_id(0)
    n = pl.num_programs(0)
    slot = lax.rem(i, 2)

    def gather(hbm, t, r, s, size):
        return pltpu.make_async_copy(hbm.at[pl.ds(t, size), :], xbuf.at[s, pl.ds(r, size), :], gsem.at[s])

    def scatter(hbm, t, r, s, size):
        return pltpu.make_async_copy(ybuf.at[s, pl.ds(r, size), :], hbm.at[pl.ds(t, size), :], ssem.at[s])

    def start_rows(j, s, copy, hbm_ctx, hbm_lat):
        base = first_ref[j]

        def ctx_row(r):
            copy(hbm_ctx, order_ref[base + r], r, s, 1).start()

        def lat_row(r):
            copy(hbm_lat, order_ref[base + r] - t_ctx, r, s, 1).start()

        def rows(lo, hi, one_row):
            groups = lax.shift_right_logical(hi - lo, MOE_UNROLL.bit_length() - 1)

            def group(g, c):
                for u in range(MOE_UNROLL):
                    one_row(lo + g * MOE_UNROLL + u)
                return c

            def single(r, c):
                one_row(r)
                return c

            lax.fori_loop(0, groups, group, 0)
            lax.fori_loop(lo + groups * MOE_UNROLL, hi, single, 0)

        rows(0, nc_ref[j], ctx_row)
        rows(nc_ref[j], nv_ref[j], lat_row)

    def wait_rows(j, s, copy, hbm):
        cnt = nv_ref[j]
        bulk = pl.multiple_of(lax.shift_left(lax.shift_right_logical(cnt, 3), 3), SUB)

        @pl.when(bulk > 0)
        def _():
            copy(hbm, 0, 0, s, bulk).wait()

        def one(r, c):
            copy(hbm, 0, 0, s, 1).wait()
            return c

        lax.fori_loop(bulk, cnt, one, 0)

    @pl.when(i == 0)
    def _():
        xbuf[...] = jnp.zeros_like(xbuf)
        start_rows(0, 0, gather, xc_ref, xl_ref)

    @pl.when(i + 1 < n)
    def _():
        start_rows(i + 1, 1 - slot, gather, xc_ref, xl_ref)

    wait_rows(i, slot, gather, xl_ref)

    @pl.when(i >= 2)
    def _():
        wait_rows(i - 2, slot, scatter, ol_ref)

    def tile_part(rows):
        x = xbuf[slot, rows, 0:D_MODEL]
        h = xbuf[slot, rows, D_MODEL:2 * D_MODEL].astype(BF16)
        meta = xbuf[slot, rows, 2 * D_MODEL:]
        w_lo = meta[:, META_W_LO:META_W_LO + 1]
        w_hi = meta[:, META_W_HI:META_W_HI + 1]
        mrow = meta[:, META_MOD_ROW:META_MOD_ROW + 1]
        n_rows = x.shape[0]

        def per_row(*ks):
            ids = jnp.broadcast_to(mrow, (n_rows, 128))
            masks = [ids == float(r) for r in range(1, n_mod_rows)]
            cols = [[] for _ in ks]
            for c in range(D_MODEL // 128):
                lanes = slice(c * 128, (c + 1) * 128)
                for q, k in enumerate(ks):
                    v = jnp.broadcast_to(mods_ref[0, k:k + 1, lanes], (n_rows, 128))
                    for r, mask in enumerate(masks, 1):
                        v = jnp.where(mask, mods_ref[r, k:k + 1, lanes], v)
                    cols[q].append(v)
            return [jnp.concatenate(col, axis=1) for col in cols]

        def expert(wg, wu, w):
            act = _silu(jnp.dot(h, wg[0], preferred_element_type=F32))
            act = act * jnp.dot(h, wu[0], preferred_element_type=F32) * w
            return act.astype(BF16)

        y = jnp.dot(expert(wg_lo, wu_lo, w_lo), wd_lo[0], preferred_element_type=F32)
        y = y + jnp.dot(expert(wg_hi, wu_hi, w_hi), wd_hi[0], preferred_element_type=F32)
        (gate,) = per_row(5)
        ybuf[slot, rows, :] = x + gate * y

    @pl.when(nv_ref[i] > 0)
    def _():
        part = tm // MOE_PARTS
        for p in range(MOE_PARTS):
            tile_part(slice(p * part, (p + 1) * part))

    start_rows(i, slot, scatter, oc_ref, ol_ref)

    @pl.when(i == n - 1)
    def _():
        wait_rows(i, slot, scatter, ol_ref)

        @pl.when(i >= 1)
        def _():
            wait_rows(i - 1, 1 - slot, scatter, ol_ref)


def _moe(x1c, x1l, route_c, route_l, mods, wg, wu, wd, n_mod_rows):
    t_ctx, t_lat = x1c.shape[0], x1l.shape[0]
    tm = MOE_TILE
    bucket = jnp.concatenate([route_c[:, 0, :].reshape(t_ctx), route_l[:, 0, :].reshape(t_lat)])
    order, first, e_lo, e_hi, n_valid, n_ctx = _moe_plan(bucket.astype(jnp.int32), t_ctx, tm)
    n_tiles = n_valid.shape[0]
    w_up = lambda sel: pl.BlockSpec((1, D_MODEL, D_EXPERT),
                                    lambda i, order, first, lo, hi, nv, nc: ((lo, hi)[sel][i], 0, 0))
    w_dn = lambda sel: pl.BlockSpec((1, D_EXPERT, D_MODEL),
                                    lambda i, order, first, lo, hi, nv, nc: ((lo, hi)[sel][i], 0, 0))
    hbm = pl.BlockSpec(memory_space=pl.ANY)
    return pl.pallas_call(
        functools.partial(_moe_kernel, tm=tm, t_ctx=t_ctx, n_mod_rows=n_mod_rows),
        out_shape=[jax.ShapeDtypeStruct((t_ctx, D_MODEL), F32), jax.ShapeDtypeStruct((t_lat, D_MODEL), F32)],
        grid_spec=pltpu.PrefetchScalarGridSpec(
            num_scalar_prefetch=6,
            grid=(n_tiles,),
            in_specs=[
                hbm, hbm,
                _const_spec((MOD_ROWS, 6, D_MODEL)),
                w_up(0), w_up(0), w_dn(0), w_up(1), w_up(1), w_dn(1),
            ],
            out_specs=[hbm, hbm],
            scratch_shapes=[
                pltpu.VMEM((2, tm, ROW_LANES), F32), pltpu.VMEM((2, tm, D_MODEL), F32),
                pltpu.SemaphoreType.DMA((2,)), pltpu.SemaphoreType.DMA((2,)),
            ],
        ),
        compiler_params=_cparams("arbitrary"),
        name="moe_pairs",
    )(order, first, e_lo, e_hi, n_valid, n_ctx, x1c, x1l, mods, wg, wu, wd, wg, wu, wd)


PROJ_TILE = 512
CAST_GROUPS = N_EXPERTS


def _mixers(x, p, mods, row0, row_step, h0, casts_in, casts_attn, k_ctx=None, v_ctx=None):
    layer = 0
    latent = k_ctx is not None
    bsz, seq, _ = x.shape
    tm = PROJ_TILE if (row_step == 0 or seq % PROJ_TILE == 0) else seq
    rows = bsz * seq
    flat = lambda a: a.reshape(1 if row_step == 0 else bsz, rows if row_step == 0 else seq, a.shape[-1])
    (z, q, k, v, lg, lx), cast_a = _inproj(flat(x), mods, row0, row_step, p["norm1_g"], p["w_in"], p["q_norm_g"],
                                           p["k_norm_g"], latent, tm, layer, casts_in)
    seqs = lambda a: a.reshape(bsz, seq, a.shape[-1])
    z, q, k, v, lg, lx = map(seqs, (z, q, k, v, lg, lx))
    conv = _conv_module(z, p["conv_dw"], p["conv_b"], p["conv_ln_g"], p["conv_ln_b"])
    attn, cast_b = _attention(q, k, v, p["attn_sink"], k_ctx, v_ctx, layer, casts_attn)
    lru, fin = _recurrent_mixer(lx, lg, h0, p["lru_conv_w"], p["lru_conv_b"], p["lru_wbd"], p["lru_gbias"],
                                p["lru_lam"])
    x1, route, _ = _outproj(flat(conv), flat(attn), flat(lru), flat(x), mods, row0, row_step, p["norm2_g"],
                            p["w_out"], p["w_router_t"], p["b_router"], tm, layer, [])
    return x1, route, k, v, fin, list(cast_a) + list(cast_b)


def kernel(x_prompt, x_sample, c, cache_k, cache_v, state_lru, c_ctx, w_mod, b_mod, norm1_g, norm2_g, w_in,
           conv_dw, conv_b, conv_ln_g, conv_ln_b, q_norm_g, k_norm_g, attn_sink, lru_conv_w, lru_conv_b,
           lru_wa, lru_ba, lru_wx, lru_bx, lru_lam, w_out, w_router, b_router, w_gate_e, w_up_e, w_down_e):
    bsz, seq, _ = x_prompt.shape
    dec_bsz, dec_seq, _ = x_sample.shape
    past = cache_k.shape[2]

    cvec = jnp.zeros((MOD_ROWS, D_MODEL), F32).at[0].set(c_ctx).at[1:1 + dec_bsz].set(c)
    mods_all = _modulation(cvec, w_mod, b_mod).reshape(DEPTH, MOD_ROWS, 6, D_MODEL)

    wr_hi = w_router.T.astype(BF16)
    wr_lo = (w_router.T - wr_hi.astype(F32)).astype(BF16)
    w_router_t = jnp.concatenate([wr_hi, wr_lo], axis=0)
    w_in_bf, w_out_bf = w_in[0:1].astype(BF16), w_out[0:1].astype(BF16)
    w_in_groups = w_in.reshape(DEPTH, CAST_GROUPS, D_MODEL // CAST_GROUPS, D_IN)
    w_out_groups = w_out.reshape(DEPTH, CAST_GROUPS, D_MODEL // CAST_GROUPS, D_MODEL)
    layers = []
    for l in range(DEPTH):
        wbd, gbias = _lru_gate_weights(lru_wa[l], lru_ba[l], lru_wx[l], lru_bx[l])
        layers.append({
            "norm1_g": norm1_g[l], "norm2_g": norm2_g[l],
            "conv_dw": conv_dw[l], "conv_b": conv_b[l], "conv_ln_g": conv_ln_g[l], "conv_ln_b": conv_ln_b[l],
            "q_norm_g": q_norm_g[l], "k_norm_g": k_norm_g[l], "attn_sink": attn_sink[l],
            "lru_conv_w": lru_conv_w[l], "lru_conv_b": lru_conv_b[l], "lru_wbd": wbd, "lru_gbias": gbias,
            "lru_lam": lru_lam[l],
            "w_router_t": w_router_t, "b_router": b_router,
        })

    y, z = x_prompt, x_sample
    ks, vs, hs = [], [], []
    h0_ctx = jnp.zeros((bsz, 2, D_LRU), F32)
    for l in range(DEPTH):
        p, mods = dict(layers[l], w_in=w_in_bf, w_out=w_out_bf), mods_all[l]
        y1, route_c, k_l, v_l, h_l, _ = _mixers(y, p, mods, 0, 0, h0_ctx, [], [])
        ks.append(k_l.reshape(bsz, seq, N_KV_HEADS, HEAD_DIM))
        vs.append(v_l.reshape(bsz, seq, N_KV_HEADS, HEAD_DIM))
        hs.append(h_l)
        casts_attn = [(w_gate_e, l), (w_down_e, l)]
        if l + 1 < DEPTH:
            casts_attn += [(w_in_groups, l + 1), (w_out_groups, l + 1)]
        z1, route_l, _, _, _, (wu_bf, wg_bf, wd_bf, *next_proj) = _mixers(
            z, p, mods, 1, 1, state_lru[:, l], [(w_up_e, l)], casts_attn,
            cache_k[:, l].reshape(dec_bsz, past, D_KV).astype(BF16),
            cache_v[:, l].reshape(dec_bsz, past, D_KV).astype(BF16))
        if next_proj:
            w_in_bf = next_proj[0].reshape(1, D_MODEL, D_IN)
            w_out_bf = next_proj[1].reshape(1, D_MODEL, D_MODEL)
        y, z = _moe(y1.reshape(bsz * seq, -1), z1.reshape(dec_bsz * dec_seq, -1), route_c, route_l,
                    mods, wg_bf, wu_bf, wd_bf, 1 + dec_bsz)
        y, z = y.reshape(bsz, seq, D_MODEL), z.reshape(dec_bsz, dec_seq, D_MODEL)
    new_cache_k = jnp.stack(ks, axis=1)
    new_cache_v = jnp.stack(vs, axis=1)
    new_state_lru = jnp.stack(hs, axis=1)
    return y, z, new_cache_k, new_cache_v, new_state_lru
```

```python
import functools

import numpy as np
import jax
import jax.numpy as jnp
from jax import lax
from jax.experimental import pallas as pl
from jax.experimental.pallas import tpu as pltpu

D_MODEL = 2048
DEPTH = 2
GRID_W = 64
D_CONV = 512
N_HEADS = 8
N_KV_HEADS = 2
HEAD_DIM = 128
GROUP = N_HEADS // N_KV_HEADS
D_ATTN = N_HEADS * HEAD_DIM
D_KV = N_KV_HEADS * HEAD_DIM
D_LRU = 512
CONV_K = 31
LRU_CONV_K = 4
LRU_BLOCKS = 8
LRU_BLK = D_LRU // LRU_BLOCKS
LRU_C = 8.0
WINDOW = 128
ROPE_BASE = 10000.0
ATTN_SCALE = HEAD_DIM ** -0.5
N_EXPERTS = 16
N_EXPERT_GROUPS = 4
EXPERTS_PER_GROUP = N_EXPERTS // N_EXPERT_GROUPS
D_EXPERT = 512
EPS = 1e-6
NEG = -1e30
D_IN = 2 * D_CONV + D_ATTN + 2 * D_KV + 2 * D_LRU
N_MOD = 6 * D_MODEL
SUB = 8
MOD_ROWS = SUB

V7X_VMEM_BYTES = 64 * 1024 * 1024
VMEM_LIMIT = V7X_VMEM_BYTES - 8 * 1024 * 1024

F32 = jnp.float32
BF16 = jnp.bfloat16


def _cparams(*sem):
    return pltpu.CompilerParams(dimension_semantics=sem, vmem_limit_bytes=VMEM_LIMIT)


def _sigmoid(x):
    return 0.5 * jnp.tanh(0.5 * x) + 0.5


def _silu(x):
    return x * _sigmoid(x)


def _const_spec(shape):
    return pl.BlockSpec(shape, lambda *_: (0,) * len(shape), pipeline_mode=pl.Buffered(1))


def _split_bf16(x):
    hi = x.astype(BF16)
    return hi, (x - hi.astype(F32)).astype(BF16)


def _mod_kernel(c_ref, w_ref, b_ref, o_ref):
    s = _silu(c_ref[...])
    s_hi = s.astype(BF16).astype(F32)
    s_both = jnp.concatenate([s_hi, s - s_hi], axis=0).astype(BF16)
    w_hi, w_lo = _split_bf16(w_ref[0])
    by_hi = jnp.dot(s_both, w_hi, preferred_element_type=F32)
    by_lo = jnp.dot(s_both, w_lo, preferred_element_type=F32)
    o_ref[0] = by_hi[0:MOD_ROWS] + by_hi[MOD_ROWS:] + by_lo[0:MOD_ROWS] + b_ref[0]


def _modulation(cvec, w_mod, b_mod):
    tn = 1536
    return pl.pallas_call(
        _mod_kernel,
        out_shape=jax.ShapeDtypeStruct((DEPTH, MOD_ROWS, N_MOD), F32),
        grid=(DEPTH, N_MOD // tn),
        in_specs=[
            _const_spec((MOD_ROWS, D_MODEL)),
            pl.BlockSpec((1, D_MODEL, tn), lambda l, j: (l, 0, j)),
            pl.BlockSpec((1, 1, tn), lambda l, j: (l, 0, j)),
        ],
        out_specs=pl.BlockSpec((1, MOD_ROWS, tn), lambda l, j: (l, 0, j)),
        compiler_params=_cparams("parallel", "parallel"),
        name="modulation",
    )(cvec, w_mod, b_mod.reshape(DEPTH, 1, N_MOD))


def _head_norm(x, g):
    return x * lax.rsqrt(jnp.mean(x * x, axis=-1, keepdims=True) + EPS) * g


def _rope(x, cos, sin_signed):
    lane = lax.broadcasted_iota(jnp.int32, x.shape, 1)
    partner = jnp.where((lane % 64) < 32, pltpu.roll(x, 96, 1), pltpu.roll(x, 32, 1))
    return x * cos + partner * sin_signed


def _inproj_kernel(*refs, rope, n_cast):
    refs = list(refs)
    cast_dst = [refs.pop() for _ in range(n_cast)][::-1]
    z_ref, q_ref, k_ref, v_ref, lg_ref, lx_ref = refs[-6:]
    del refs[-6:]
    cast_src = [refs.pop() for _ in range(n_cast)][::-1]
    if rope:
        x_ref, mod_ref, g1_ref, w_ref, qg_ref, kg_ref, cos_ref, sin_ref = refs
    else:
        x_ref, mod_ref, g1_ref, w_ref, qg_ref, kg_ref = refs
    for src, dst in zip(cast_src, cast_dst):
        dst[0] = src[0, 0].astype(BF16)
    x = x_ref[0]
    mod = mod_ref[0]
    shift, scale = mod[0:1], mod[1:2]
    y = x * lax.rsqrt(jnp.mean(x * x, axis=-1, keepdims=True) + EPS) * g1_ref[...]
    h = (y * (1.0 + scale) + shift).astype(BF16)

    def proj(c0, c1):
        return jnp.dot(h, w_ref[:, c0:c1], preferred_element_type=F32)

    glu = proj(0, 2 * D_CONV)
    z_ref[0] = glu[:, :D_CONV] * _sigmoid(glu[:, D_CONV:])

    c0 = 2 * D_CONV
    q = proj(c0, c0 + D_ATTN)
    qg = qg_ref[...]
    for hd in range(N_HEADS):
        qh = _head_norm(q[:, hd * HEAD_DIM:(hd + 1) * HEAD_DIM], qg)
        if rope:
            qh = _rope(qh, cos_ref[...], sin_ref[...])
        q_ref[0, :, hd * HEAD_DIM:(hd + 1) * HEAD_DIM] = (qh * ATTN_SCALE).astype(BF16)

    c0 += D_ATTN
    kv = proj(c0, c0 + 2 * D_KV)
    kg = kg_ref[...]
    for hd in range(N_KV_HEADS):
        kh = _head_norm(kv[:, hd * HEAD_DIM:(hd + 1) * HEAD_DIM], kg)
        if rope:
            kh = _rope(kh, cos_ref[...], sin_ref[...])
        k_ref[0, :, hd * HEAD_DIM:(hd + 1) * HEAD_DIM] = kh.astype(k_ref.dtype)
    v_ref[0] = kv[:, D_KV:].astype(v_ref.dtype)

    c0 += 2 * D_KV
    l2 = proj(c0, c0 + 2 * D_LRU)
    lg_ref[0] = l2[:, :D_LRU]
    lx_ref[0] = l2[:, D_LRU:]


def _rope_tables(seq):
    n_freq = HEAD_DIM // 4
    inv = (ROPE_BASE ** (-np.arange(n_freq, dtype=np.float32) / n_freq)).astype(np.float32)
    t = np.arange(seq)
    ang_r = (t // GRID_W).astype(np.float32)[:, None] * inv[None, :]
    ang_c = (t % GRID_W).astype(np.float32)[:, None] * inv[None, :]
    cos = np.concatenate([np.cos(ang_r)] * 2 + [np.cos(ang_c)] * 2, axis=-1)
    sin = np.concatenate([-np.sin(ang_r), np.sin(ang_r), -np.sin(ang_c), np.sin(ang_c)], axis=-1)
    return jnp.asarray(cos, F32), jnp.asarray(sin, F32)


def _add_casts(cast_weights, n_inner, n_steps, in_specs, args, out_specs, out_shape):
    for w, layer in cast_weights:
        _, n_g, rows, cols = w.shape
        parts = n_steps // n_g
        assert parts * n_g == n_steps and rows % parts == 0
        piece = lambda b, i, parts=parts: divmod(b * n_inner + i, parts)
        in_specs.append(pl.BlockSpec((1, 1, rows // parts, cols),
                                     lambda b, i, layer=layer, piece=piece: (layer, *piece(b, i), 0)))
        args.append(w)
        out_specs.append(pl.BlockSpec((1, rows // parts, cols), lambda b, i, piece=piece: (*piece(b, i), 0)))
        out_shape.append(jax.ShapeDtypeStruct((n_g, rows, cols), BF16))


def _inproj(x, mods, row0, row_step, g1, w_in_bf, qg, kg, rope, tm, cast_weights):
    bsz, seq, _ = x.shape
    tok = lambda width: pl.BlockSpec((1, tm, width), lambda b, i: (b, i, 0))
    in_specs = [
        tok(D_MODEL),
        pl.BlockSpec((1, 6, D_MODEL), lambda b, i: (row0 + row_step * b, 0, 0)),
        _const_spec((1, D_MODEL)),
        _const_spec((D_MODEL, D_IN)),
        _const_spec((1, HEAD_DIM)),
        _const_spec((1, HEAD_DIM)),
    ]
    args = [x, mods, g1.reshape(1, D_MODEL), w_in_bf, qg.reshape(1, HEAD_DIM), kg.reshape(1, HEAD_DIM)]
    if rope:
        cos, sin = _rope_tables(seq)
        in_specs += [pl.BlockSpec((tm, HEAD_DIM), lambda b, i: (i, 0))] * 2
        args += [cos, sin]
    widths = (D_CONV, D_ATTN, D_KV, D_KV, D_LRU, D_LRU)
    kv_dtype = BF16 if rope else F32
    dtypes = (F32, BF16, kv_dtype, kv_dtype, F32, F32)
    out_shape = [jax.ShapeDtypeStruct((bsz, seq, w), dt) for w, dt in zip(widths, dtypes)]
    out_specs = [tok(w) for w in widths]
    n_inner = seq // tm
    _add_casts(cast_weights, n_inner, bsz * n_inner, in_specs, args, out_specs, out_shape)
    outs = pl.pallas_call(
        functools.partial(_inproj_kernel, rope=rope, n_cast=len(cast_weights)),
        out_shape=out_shape,
        grid=(bsz, n_inner),
        in_specs=in_specs,
        out_specs=out_specs,
        compiler_params=_cparams("parallel", "parallel"),
        name="inproj_rope" if rope else "inproj",
    )(*args)
    return outs[:6], outs[6:]


CONV_PAD = (CONV_K - 1) // 2
CONV_HALO = 16
CONV_CHUNK = 64
CONV_COLS = 256
CONV_GROUP = 4


def _conv_kernel(z_ref, w_ref, b_ref, g_ref, beta_ref, o_ref, zp_ref, win_ref, acc_ref, *, seq):
    zeros = jnp.zeros((CONV_HALO, D_CONV), F32)
    zp_ref[0:CONV_HALO, :] = zeros
    zp_ref[CONV_HALO + seq:2 * CONV_HALO + seq, :] = zeros
    zp_ref[CONV_HALO:CONV_HALO + seq, :] = z_ref[0]
    n_parts = D_CONV // CONV_COLS

    group_rows = CONV_GROUP * CONV_CHUNK

    def body(c, carry):
        r0 = pl.multiple_of(c * group_rows, group_rows)
        for q in range(CONV_GROUP):
            for p in range(n_parts):
                win_ref[q * n_parts + p] = zp_ref[pl.ds(r0 + q * CONV_CHUNK, CONV_CHUNK + 2 * CONV_HALO),
                                                  p * CONV_COLS:(p + 1) * CONV_COLS]

        def taps(j, inner):
            p = lax.rem(j, n_parts)
            part = jnp.broadcast_to(b_ref[p], (CONV_CHUNK, CONV_COLS))
            for k in range(CONV_K):
                off = CONV_HALO - CONV_PAD + k
                part = part + jnp.tile(w_ref[p, k], (CONV_CHUNK // SUB, 1)) * win_ref[j, off:off + CONV_CHUNK, :]
            acc_ref[j] = part
            return inner

        lax.fori_loop(0, CONV_GROUP * n_parts, taps, 0)
        acc = jnp.concatenate(
            [jnp.concatenate([acc_ref[q * n_parts + p] for p in range(n_parts)], axis=1) for q in range(CONV_GROUP)],
            axis=0)
        mu = jnp.mean(acc, axis=-1, keepdims=True)
        xc = acc - mu
        var = jnp.mean(xc * xc, axis=-1, keepdims=True)
        y = xc * lax.rsqrt(var + EPS) * g_ref[...] + beta_ref[...]
        o_ref[0, pl.ds(r0, group_rows), :] = _silu(y).astype(BF16)
        return carry

    lax.fori_loop(0, seq // group_rows, body, 0)


def _conv_module(z, w, b, g, beta):
    bsz, seq, _ = z.shape
    row = lambda a: a.reshape(1, D_CONV)
    n_parts = D_CONV // CONV_COLS
    w_parts = jnp.broadcast_to(w.reshape(CONV_K, 1, n_parts, CONV_COLS), (CONV_K, SUB, n_parts, CONV_COLS))
    w_parts = w_parts.transpose(2, 0, 1, 3)
    return pl.pallas_call(
        functools.partial(_conv_kernel, seq=seq),
        out_shape=jax.ShapeDtypeStruct((bsz, seq, D_CONV), BF16),
        grid=(bsz,),
        in_specs=[
            pl.BlockSpec((1, seq, D_CONV), lambda i: (i, 0, 0)),
            _const_spec((n_parts, CONV_K, SUB, CONV_COLS)),
            _const_spec((n_parts, 1, CONV_COLS)), _const_spec((1, D_CONV)), _const_spec((1, D_CONV)),
        ],
        out_specs=pl.BlockSpec((1, seq, D_CONV), lambda i: (i, 0, 0)),
        scratch_shapes=[pltpu.VMEM((seq + 2 * CONV_HALO, D_CONV), F32),
                        pltpu.VMEM((CONV_GROUP * n_parts, CONV_CHUNK + 2 * CONV_HALO, CONV_COLS), F32),
                        pltpu.VMEM((CONV_GROUP * n_parts, CONV_CHUNK, CONV_COLS), F32)],
        compiler_params=_cparams("parallel"),
        name="conv_module",
    )(z, w_parts, b.reshape(n_parts, 1, CONV_COLS), row(g), row(beta))


ATTN_ROWS = 64


def _attn_kernel(*refs, latent, seq, tq, n_cast):
    n_in = 6 if latent else 4
    cast_src = refs[n_in:n_in + n_cast]
    cast_dst = refs[n_in + n_cast + 1:n_in + 2 * n_cast + 1]
    refs = refs[:n_in] + refs[n_in + n_cast:n_in + n_cast + 1] + refs[n_in + 2 * n_cast + 1:]
    if latent:
        sink_ref, q_ref, k_ref, v_ref, kc_ref, vc_ref, o_ref, s_ref, p_ref, bias_ref = refs
    else:
        sink_ref, q_ref, k_ref, v_ref, o_ref, s_ref, p_ref = refs
    for src, dst in zip(cast_src, cast_dst):
        dst[0] = src[0, 0].astype(BF16)
    q = q_ref[0]
    n_loc = 3 * tq if latent else 0
    if latent:
        n = pl.program_id(1)
        nblk = seq // tq
        blocks = (jnp.maximum(n - 1, 0), n, jnp.minimum(n + 1, nblk - 1))

        def window(ref):
            parts = [ref[0, pl.ds(pl.multiple_of(i * tq, tq), tq), :] for i in blocks]
            return jnp.concatenate(parts, axis=0)

        k_all = jnp.concatenate([window(k_ref), kc_ref[0]], axis=0).astype(BF16)
        v_all = jnp.concatenate([window(v_ref), vc_ref[0]], axis=0).astype(BF16)
        qpos = n * tq + lax.broadcasted_iota(jnp.int32, (tq, 3 * tq), 0)
        kpos = (n - 1) * tq + lax.broadcasted_iota(jnp.int32, (tq, 3 * tq), 1)
        ok = (jnp.abs(qpos - kpos) <= WINDOW) & (kpos >= 0) & (kpos < seq)
        bias_ref[...] = jnp.where(ok, 0.0, NEG).astype(F32)
    else:
        k_all = k_ref[0].astype(BF16)
        v_all = v_ref[0].astype(BF16)

    for j in range(N_KV_HEADS):
        heads = [j * GROUP + g for g in range(GROUP)]
        qs = jnp.concatenate([q[:, h * HEAD_DIM:(h + 1) * HEAD_DIM] for h in heads], axis=0)
        kj = k_all[:, j * HEAD_DIM:(j + 1) * HEAD_DIM]
        vj = v_all[:, j * HEAD_DIM:(j + 1) * HEAD_DIM]
        s_ref[...] = lax.dot_general(qs, kj, (((1,), (1,)), ((), ())), preferred_element_type=F32)
        inv = []
        for rb in range(GROUP * tq // ATTN_ROWS):
            rows = slice(rb * ATTN_ROWS, (rb + 1) * ATTN_ROWS)
            sk = sink_ref[heads[rb * ATTN_ROWS // tq]]
            parts = [s_ref[rows, n_loc:]]
            if latent:
                q0 = rb * ATTN_ROWS % tq
                parts.insert(0, s_ref[rows, :n_loc] + bias_ref[q0:q0 + ATTN_ROWS, :])
            m = sk
            for s in parts:
                m = jnp.maximum(m, jnp.max(s, axis=-1, keepdims=True))
            den = jnp.exp(sk - m)
            c0 = 0
            for s in parts:
                p = jnp.exp(s - m)
                den = den + jnp.sum(p, axis=-1, keepdims=True)
                p_ref[rows, c0:c0 + s.shape[1]] = p.astype(BF16)
                c0 += s.shape[1]
            inv.append(1.0 / den)
        o = jnp.dot(p_ref[...], vj, preferred_element_type=F32) * jnp.concatenate(inv, axis=0)
        for g, h in enumerate(heads):
            o_ref[0, :, h * HEAD_DIM:(h + 1) * HEAD_DIM] = o[g * tq:(g + 1) * tq].astype(BF16)


def _attention(q, k, v, sink, k_ctx=None, v_ctx=None, cast_weights=()):
    bsz, seq, _ = q.shape
    latent = k_ctx is not None
    tq = WINDOW if latent else seq
    seq_spec = pl.BlockSpec((1, seq, D_KV), lambda b, i: (b, 0, 0))
    in_specs = [
        pl.BlockSpec(memory_space=pltpu.SMEM),
        pl.BlockSpec((1, tq, D_ATTN), lambda b, i: (b, i, 0)),
        seq_spec, seq_spec,
    ]
    args = [sink, q, k, v]
    n_keys = seq
    if latent:
        past = k_ctx.shape[1]
        ctx_spec = pl.BlockSpec((1, past, D_KV), lambda b, i: (b, 0, 0))
        in_specs += [ctx_spec, ctx_spec]
        args += [k_ctx, v_ctx]
        n_keys = 3 * tq + past
    scratch = [pltpu.VMEM((GROUP * tq, n_keys), F32), pltpu.VMEM((GROUP * tq, n_keys), BF16)]
    if latent:
        scratch.append(pltpu.VMEM((tq, 3 * tq), F32))
    out_shape = [jax.ShapeDtypeStruct((bsz, seq, D_ATTN), BF16)]
    out_specs = [pl.BlockSpec((1, tq, D_ATTN), lambda b, i: (b, i, 0))]
    n_inner = seq // tq
    _add_casts(cast_weights, n_inner, bsz * n_inner, in_specs, args, out_specs, out_shape)
    outs = pl.pallas_call(
        functools.partial(_attn_kernel, latent=latent, seq=seq, tq=tq, n_cast=len(cast_weights)),
        out_shape=out_shape,
        grid=(bsz, n_inner),
        in_specs=in_specs,
        out_specs=out_specs,
        scratch_shapes=scratch,
        compiler_params=_cparams("parallel", "parallel"),
        name="attn_latent" if latent else "attn_context",
    )(*args)
    return outs[0], outs[1:]


LRU_HALO = 8
LRU_CHUNK = 128
LRU_HALF = D_LRU // 2


def _softplus(x):
    return jnp.maximum(x, 0.0) + jnp.log(1.0 + jnp.exp(-jnp.abs(x)))


def _gelu_tanh(x):
    return 0.5 * x * (1.0 + jnp.tanh(0.7978845608028654 * (x + 0.044715 * (x * x * x))))


def _scan_tile(a, b, carry, reverse):
    row = lax.broadcasted_iota(jnp.int32, a.shape, 0)
    for d in (1, 2, 4):
        if reverse:
            valid = row < SUB - d
            shift = SUB - d
        else:
            valid = row >= d
            shift = d
        a_prev = jnp.where(valid, pltpu.roll(a, shift, 0), 1.0)
        b_prev = jnp.where(valid, pltpu.roll(b, shift, 0), 0.0)
        b = a * b_prev + b
        a = a * a_prev
    h = a * carry + b
    last = h[0:1, :] if reverse else h[SUB - 1:SUB, :]
    return h, last


def _lru_kernel(lx_ref, lg_ref, h0_ref, cw_ref, cb_ref, wbd_ref, gbias_ref, lam_ref,
                o_ref, fin_ref, xp_ref, win_ref, af_ref, bf_ref, ab_ref, bb_ref, *, seq):
    zeros = jnp.zeros((LRU_HALO, D_LRU), F32)
    xp_ref[0:LRU_HALO, :] = zeros
    xp_ref[LRU_HALO + seq:2 * LRU_HALO + seq, :] = zeros
    xp_ref[LRU_HALO:LRU_HALO + seq, :] = lx_ref[0]
    a_refs = (af_ref, ab_ref)
    b_refs = (bf_ref, bb_ref)
    half_c = (-0.5 * LRU_C) * _softplus(-lam_ref[...])

    def gates(c, carry):
        r0 = pl.multiple_of(c * LRU_CHUNK, LRU_CHUNK)
        win_ref[...] = xp_ref[pl.ds(r0, LRU_CHUNK + 2 * LRU_HALO), :]
        xc = jnp.broadcast_to(cb_ref[...], (LRU_CHUNK, D_LRU))
        for k in range(LRU_CONV_K):
            off = LRU_HALO - 2 + k
            xc = xc + cw_ref[k:k + 1, :] * win_ref[off:off + LRU_CHUNK, :]
        for s in range(2):
            cols = slice(s * LRU_HALF, (s + 1) * LRU_HALF)
            xs = xc[:, cols]
            g = jnp.dot(xs.astype(BF16), wbd_ref[s], preferred_element_type=F32)
            g = g + gbias_ref[s:s + 1, :]
            for d in range(2):
                base = d * 2 * LRU_HALF
                hc = half_c[d:d + 1, cols]
                log_a = hc * jnp.tanh(g[:, base:base + LRU_HALF]) + hc
                i = 0.5 * jnp.tanh(g[:, base + LRU_HALF:base + 2 * LRU_HALF]) + 0.5
                a = jnp.exp(log_a)
                t = jnp.tanh(log_a)
                b = jnp.sqrt(-2.0 * t / (1.0 - t)) * (i * xs)
                a_refs[d][pl.ds(r0, LRU_CHUNK), cols] = a
                b_refs[d][pl.ds(r0, LRU_CHUNK), cols] = b
        return carry

    lax.fori_loop(0, seq // LRU_CHUNK, gates, 0)

    ntile = seq // SUB

    def scan(t, carry):
        cf, cb = carry
        rf = pl.multiple_of(t * SUB, SUB)
        rb = pl.multiple_of((ntile - 1 - t) * SUB, SUB)
        hf, cf = _scan_tile(af_ref[pl.ds(rf, SUB), :], bf_ref[pl.ds(rf, SUB), :], cf, False)
        hb, cb = _scan_tile(ab_ref[pl.ds(rb, SUB), :], bb_ref[pl.ds(rb, SUB), :], cb, True)
        bf_ref[pl.ds(rf, SUB), :] = hf
        bb_ref[pl.ds(rb, SUB), :] = hb
        return cf, cb

    h0 = h0_ref[0]
    cf, cb = lax.fori_loop(0, ntile, scan, (h0[0:1, :], h0[1:2, :]))
    fin_ref[0, 0:1, :] = cf
    fin_ref[0, 1:2, :] = cb

    def finish(c, carry):
        r0 = pl.multiple_of(c * LRU_CHUNK, LRU_CHUNK)
        rows = pl.ds(r0, LRU_CHUNK)
        o_ref[0, rows, :] = ((bf_ref[rows, :] + bb_ref[rows, :]) * _gelu_tanh(lg_ref[0, rows, :])).astype(BF16)
        return carry

    lax.fori_loop(0, seq // LRU_CHUNK, finish, 0)


def _lru_gate_weights(wa, ba, wx, bx):
    per_half = LRU_HALF // LRU_BLK
    on_diagonal = jnp.asarray(np.kron(np.eye(per_half), np.ones((LRU_BLK, LRU_BLK))), F32)

    def dense(w):
        blocks = w.reshape(2, per_half, LRU_BLK, 1, LRU_BLK)
        tiled = jnp.broadcast_to(blocks, (2, per_half, LRU_BLK, per_half, LRU_BLK))
        return tiled.reshape(2, LRU_HALF, LRU_HALF) * on_diagonal

    wbd = jnp.concatenate([dense(wa[0]), dense(wx[0]), dense(wa[1]), dense(wx[1])], axis=2)
    gbias = jnp.stack([ba[0], bx[0], ba[1], bx[1]]).reshape(4, 2, LRU_HALF).transpose(1, 0, 2)
    return (0.5 * wbd).astype(BF16), 0.5 * gbias.reshape(2, 4 * LRU_HALF)


def _recurrent_mixer(lx, lg, h0, cw, cb, wbd, gbias, lam):
    bsz, seq, _ = lx.shape
    seq_spec = pl.BlockSpec((1, seq, D_LRU), lambda i: (i, 0, 0))
    state_spec = pl.BlockSpec((1, 2, D_LRU), lambda i: (i, 0, 0))
    return pl.pallas_call(
        functools.partial(_lru_kernel, seq=seq),
        out_shape=[jax.ShapeDtypeStruct((bsz, seq, D_LRU), BF16),
                   jax.ShapeDtypeStruct((bsz, 2, D_LRU), F32)],
        grid=(bsz,),
        in_specs=[
            seq_spec, seq_spec, state_spec,
            _const_spec((LRU_CONV_K, D_LRU)), _const_spec((1, D_LRU)),
            _const_spec((2, LRU_HALF, 4 * LRU_HALF)), _const_spec((2, 4 * LRU_HALF)),
            _const_spec((2, D_LRU)),
        ],
        out_specs=[seq_spec, state_spec],
        scratch_shapes=[pltpu.VMEM((seq + 2 * LRU_HALO, D_LRU), F32),
                        pltpu.VMEM((LRU_CHUNK + 2 * LRU_HALO, D_LRU), F32)] + [pltpu.VMEM((seq, D_LRU), F32)] * 4,
        compiler_params=_cparams("parallel"),
        name="rglru",
    )(lx, lg, h0, cw, cb.reshape(1, D_LRU), wbd, gbias, lam)


N_PAIRS = EXPERTS_PER_GROUP * (EXPERTS_PER_GROUP - 1) // 2
N_BUCKETS = N_EXPERT_GROUPS * N_PAIRS
PAIR_LO = (0, 0, 0, 1, 1, 2)
PAIR_HI = (1, 2, 3, 2, 3, 3)
ROUTE_ROWS = SUB


def _route(scores, biased):
    rows = [biased[e:e + 1, :] for e in range(N_EXPERTS)]
    group_score = []
    for g in range(N_EXPERT_GROUPS):
        a, b, c, d = rows[4 * g:4 * g + 4]
        hi1, lo1 = jnp.maximum(a, b), jnp.minimum(a, b)
        hi2, lo2 = jnp.maximum(c, d), jnp.minimum(c, d)
        top = jnp.maximum(hi1, hi2)
        second = jnp.maximum(jnp.minimum(hi1, hi2), jnp.maximum(lo1, lo2))
        group_score.append(top + second)
    best = group_score[0]
    g_sel = jnp.zeros_like(best, dtype=jnp.int32)
    for g in range(1, N_EXPERT_GROUPS):
        better = group_score[g] > best
        g_sel = jnp.where(better, g, g_sel)
        best = jnp.where(better, group_score[g], best)
    sel = []
    for e in range(N_EXPERTS):
        g = e // EXPERTS_PER_GROUP
        rank = jnp.zeros_like(g_sel)
        for o in range(g * EXPERTS_PER_GROUP, (g + 1) * EXPERTS_PER_GROUP):
            if o == e:
                continue
            ahead = (rows[o] >= rows[e]) if o < e else (rows[o] > rows[e])
            rank = rank + ahead.astype(jnp.int32)
        sel.append(jnp.where(jnp.where(g_sel == g, rank, 2) < 2, 1, 0))
    zero = jnp.zeros_like(best)
    lo_w, hi_w = zero, zero
    lo_idx = jnp.zeros_like(g_sel)
    hi_idx = jnp.zeros_like(g_sel)
    for g in range(N_EXPERT_GROUPS):
        seen = jnp.zeros_like(g_sel)
        for j in range(EXPERTS_PER_GROUP):
            e = g * EXPERTS_PER_GROUP + j
            order = jnp.where(sel[e] == 1, seen, 2)
            lo_w = jnp.where(order == 0, scores[e:e + 1, :], lo_w)
            hi_w = jnp.where(order == 1, scores[e:e + 1, :], hi_w)
            lo_idx = jnp.where(order == 0, j, lo_idx)
            hi_idx = jnp.where(order == 1, j, hi_idx)
            seen = seen + sel[e]
    pair_base = jnp.where(lo_idx == 0, 0, jnp.where(lo_idx == 1, 3, 5))
    bucket = g_sel * N_PAIRS + pair_base + hi_idx - lo_idx - 1
    total = lo_w + hi_w
    pad = jnp.zeros((ROUTE_ROWS - 3, best.shape[1]), F32)
    return jnp.concatenate([bucket.astype(F32), lo_w / total, hi_w / total, pad], axis=0)


def _modulated_norm(x, g, shift, scale):
    y = x * lax.rsqrt(jnp.mean(x * x, axis=-1, keepdims=True) + EPS) * g
    return y * (1.0 + scale) + shift


META_LANES = 128
ROW_LANES = 2 * D_MODEL + META_LANES
META_W_LO, META_W_HI, META_MOD_ROW = 1, 2, 3


def _outproj_kernel(conv_ref, attn_ref, lru_ref, x_ref, mod_ref, g2_ref, w_ref, wr_ref, br_ref,
                    x1_ref, route_ref, cat_ref, *, row0, row_step):
    cat_ref[:, 0:D_CONV] = conv_ref[0]
    cat_ref[:, D_CONV:D_CONV + D_ATTN] = attn_ref[0]
    cat_ref[:, D_CONV + D_ATTN:] = lru_ref[0]
    mod = mod_ref[0]
    tm = cat_ref.shape[0]
    mix = jnp.dot(cat_ref[...], w_ref[...], preferred_element_type=F32)
    x1 = x_ref[0] + mod[2:3] * mix
    x1_ref[0, :, 0:D_MODEL] = x1
    h2 = _modulated_norm(x1, g2_ref[...], mod[3:4], mod[4:5])
    x1_ref[0, :, D_MODEL:2 * D_MODEL] = h2
    h_hi, h_lo = _split_bf16(h2)
    nt = (((1,), (1,)), ((), ()))
    by_hi = lax.dot_general(wr_ref[...], h_hi, nt, preferred_element_type=F32)
    by_lo = lax.dot_general(wr_ref[0:N_EXPERTS, :], h_lo, nt, preferred_element_type=F32)
    logits = by_hi[0:N_EXPERTS] + by_hi[N_EXPERTS:] + by_lo
    scores = _sigmoid(logits)
    route = _route(scores, scores + br_ref[...])
    route_ref[0] = route
    mod_row = jnp.full((1, tm), row0, jnp.int32) + row_step * pl.program_id(0)
    meta = jnp.concatenate([route[0:META_MOD_ROW], mod_row.astype(F32),
                            jnp.zeros((META_LANES - META_MOD_ROW - 1, tm), F32)], axis=0)
    x1_ref[0, :, 2 * D_MODEL:] = meta.T


def _outproj(conv, attn, lru, x, mods, row0, row_step, g2, w_out_bf, w_router_t, b_router, tm):
    bsz, seq, _ = x.shape
    tok = lambda width: pl.BlockSpec((1, tm, width), lambda b, i: (b, i, 0))
    return pl.pallas_call(
        functools.partial(_outproj_kernel, row0=row0, row_step=row_step),
        out_shape=[jax.ShapeDtypeStruct((bsz, seq, ROW_LANES), F32),
                   jax.ShapeDtypeStruct((bsz, ROUTE_ROWS, seq), F32)],
        grid=(bsz, seq // tm),
        in_specs=[
            tok(D_CONV), tok(D_ATTN), tok(D_LRU), tok(D_MODEL),
            pl.BlockSpec((1, 6, D_MODEL), lambda b, i: (row0 + row_step * b, 0, 0)),
            _const_spec((1, D_MODEL)),
            _const_spec((D_MODEL, D_MODEL)),
            _const_spec((2 * N_EXPERTS, D_MODEL)),
            _const_spec((N_EXPERTS, 1)),
        ],
        out_specs=[tok(ROW_LANES), pl.BlockSpec((1, ROUTE_ROWS, tm), lambda b, i: (b, 0, i))],
        scratch_shapes=[pltpu.VMEM((tm, D_MODEL), BF16)],
        compiler_params=_cparams("parallel", "parallel"),
        name="outproj_router",
    )(conv, attn, lru, x, mods, g2.reshape(1, D_MODEL), w_out_bf, w_router_t, b_router.reshape(N_EXPERTS, 1))


MOE_TILE = 256
MOE_UNROLL = 8


def _moe_plan(bucket, t_ctx, tm):
    t_all = bucket.shape[0]
    n_tiles = (t_all + N_BUCKETS * (tm - 1) + tm - 1) // tm
    ids = jnp.arange(N_BUCKETS, dtype=jnp.int32)
    order = jnp.argsort(bucket, stable=True).astype(jnp.int32)
    member = (bucket[None, :] == ids[:, None]).astype(jnp.int32)
    counts = jnp.sum(member, axis=1)
    counts_ctx = jnp.sum(member[:, :t_ctx], axis=1)
    tiles_per = (counts + tm - 1) // tm
    tile_end = jnp.cumsum(tiles_per)
    tile_start = tile_end - tiles_per
    sorted_start = jnp.cumsum(counts) - counts
    n_used = tile_end[-1]
    j = jnp.arange(n_tiles, dtype=jnp.int32)
    jj = jnp.minimum(j, n_used - 1)
    b_of = jnp.sum((tile_end[None, :] <= jj[:, None]).astype(jnp.int32), axis=1)
    local = jj - tile_start[b_of]
    n_valid = jnp.where(j < n_used, jnp.clip(counts[b_of] - local * tm, 0, tm), 0)
    n_ctx = jnp.clip(counts_ctx[b_of] - local * tm, 0, n_valid)
    first = jnp.where(j < n_used, sorted_start[b_of] + local * tm, 0)
    group, pair = b_of // N_PAIRS, b_of % N_PAIRS
    e_lo = group * EXPERTS_PER_GROUP + jnp.asarray(PAIR_LO, jnp.int32)[pair]
    e_hi = group * EXPERTS_PER_GROUP + jnp.asarray(PAIR_HI, jnp.int32)[pair]
    i32 = lambda a: a.astype(jnp.int32)
    return order, i32(first), i32(e_lo), i32(e_hi), i32(n_valid), i32(n_ctx)


def _moe_kernel(order_ref, first_ref, elo_ref, ehi_ref, nv_ref, nc_ref,
                xc_ref, xl_ref, mods_ref,
                wg_lo, wu_lo, wd_lo, wg_hi, wu_hi, wd_hi,
                oc_ref, ol_ref, xbuf, ybuf, gsem, ssem, *, tm, t_ctx, n_mod_rows):
    i = pl.program_id(0)
    n = pl.num_programs(0)
    slot = lax.rem(i, 2)

    def gather(hbm, t, r, s, size):
        return pltpu.make_async_copy(hbm.at[pl.ds(t, size), :], xbuf.at[s, pl.ds(r, size), :], gsem.at[s])

    def scatter(hbm, t, r, s, size):
        return pltpu.make_async_copy(ybuf.at[s, pl.ds(r, size), :], hbm.at[pl.ds(t, size), :], ssem.at[s])

    def start_rows(j, s, copy, hbm_ctx, hbm_lat):
        base = first_ref[j]

        def ctx_row(r):
            copy(hbm_ctx, order_ref[base + r], r, s, 1).start()

        def lat_row(r):
            copy(hbm_lat, order_ref[base + r] - t_ctx, r, s, 1).start()

        def rows(lo, hi, one_row):
            groups = lax.shift_right_logical(hi - lo, MOE_UNROLL.bit_length() - 1)

            def group(g, c):
                for u in range(MOE_UNROLL):
                    one_row(lo + g * MOE_UNROLL + u)
                return c

            def single(r, c):
                one_row(r)
                return c

            lax.fori_loop(0, groups, group, 0)
            lax.fori_loop(lo + groups * MOE_UNROLL, hi, single, 0)

        rows(0, nc_ref[j], ctx_row)
        rows(nc_ref[j], nv_ref[j], lat_row)

    def wait_rows(j, s, copy, hbm):
        cnt = nv_ref[j]
        bulk = pl.multiple_of(lax.shift_left(lax.shift_right_logical(cnt, 3), 3), SUB)

        @pl.when(bulk > 0)
        def _():
            copy(hbm, 0, 0, s, bulk).wait()

        def one(r, c):
            copy(hbm, 0, 0, s, 1).wait()
            return c

        lax.fori_loop(bulk, cnt, one, 0)

    @pl.when(i == 0)
    def _():
        xbuf[...] = jnp.zeros_like(xbuf)
        start_rows(0, 0, gather, xc_ref, xl_ref)

    @pl.when(i + 1 < n)
    def _():
        start_rows(i + 1, 1 - slot, gather, xc_ref, xl_ref)

    wait_rows(i, slot, gather, xl_ref)

    @pl.when(i >= 2)
    def _():
        wait_rows(i - 2, slot, scatter, ol_ref)

    @pl.when(nv_ref[i] > 0)
    def _():
        x = xbuf[slot, :, 0:D_MODEL]
        h = xbuf[slot, :, D_MODEL:2 * D_MODEL].astype(BF16)
        meta = xbuf[slot, :, 2 * D_MODEL:]
        w_lo = meta[:, META_W_LO:META_W_LO + 1]
        w_hi = meta[:, META_W_HI:META_W_HI + 1]
        mrow = meta[:, META_MOD_ROW:META_MOD_ROW + 1]
        n_rows = x.shape[0]

        def per_row(*ks):
            ids = jnp.broadcast_to(mrow, (n_rows, 128))
            masks = [ids == float(r) for r in range(1, n_mod_rows)]
            cols = [[] for _ in ks]
            for c in range(D_MODEL // 128):
                lanes = slice(c * 128, (c + 1) * 128)
                for q, k in enumerate(ks):
                    v = jnp.broadcast_to(mods_ref[0, k:k + 1, lanes], (n_rows, 128))
                    for r, mask in enumerate(masks, 1):
                        v = jnp.where(mask, mods_ref[r, k:k + 1, lanes], v)
                    cols[q].append(v)
            return [jnp.concatenate(col, axis=1) for col in cols]

        def expert(wg, wu, w):
            act = _silu(jnp.dot(h, wg[0], preferred_element_type=F32))
            act = act * jnp.dot(h, wu[0], preferred_element_type=F32) * w
            return act.astype(BF16)

        y = jnp.dot(expert(wg_lo, wu_lo, w_lo), wd_lo[0], preferred_element_type=F32)
        y = y + jnp.dot(expert(wg_hi, wu_hi, w_hi), wd_hi[0], preferred_element_type=F32)
        (gate,) = per_row(5)
        ybuf[slot] = x + gate * y

    start_rows(i, slot, scatter, oc_ref, ol_ref)

    @pl.when(i == n - 1)
    def _():
        wait_rows(i, slot, scatter, ol_ref)

        @pl.when(i >= 1)
        def _():
            wait_rows(i - 1, 1 - slot, scatter, ol_ref)


def _moe(x1c, x1l, route_c, route_l, mods, wg, wu, wd, n_mod_rows):
    t_ctx, t_lat = x1c.shape[0], x1l.shape[0]
    tm = MOE_TILE
    bucket = jnp.concatenate([route_c[:, 0, :].reshape(t_ctx), route_l[:, 0, :].reshape(t_lat)])
    order, first, e_lo, e_hi, n_valid, n_ctx = _moe_plan(bucket.astype(jnp.int32), t_ctx, tm)
    n_tiles = n_valid.shape[0]
    w_up = lambda sel: pl.BlockSpec((1, D_MODEL, D_EXPERT),
                                    lambda i, order, first, lo, hi, nv, nc: ((lo, hi)[sel][i], 0, 0))
    w_dn = lambda sel: pl.BlockSpec((1, D_EXPERT, D_MODEL),
                                    lambda i, order, first, lo, hi, nv, nc: ((lo, hi)[sel][i], 0, 0))
    hbm = pl.BlockSpec(memory_space=pl.ANY)
    return pl.pallas_call(
        functools.partial(_moe_kernel, tm=tm, t_ctx=t_ctx, n_mod_rows=n_mod_rows),
        out_shape=[jax.ShapeDtypeStruct((t_ctx, D_MODEL), F32), jax.ShapeDtypeStruct((t_lat, D_MODEL), F32)],
        grid_spec=pltpu.PrefetchScalarGridSpec(
            num_scalar_prefetch=6,
            grid=(n_tiles,),
            in_specs=[
                hbm, hbm,
                _const_spec((MOD_ROWS, 6, D_MODEL)),
                w_up(0), w_up(0), w_dn(0), w_up(1), w_up(1), w_dn(1),
            ],
            out_specs=[hbm, hbm],
            scratch_shapes=[
                pltpu.VMEM((2, tm, ROW_LANES), F32), pltpu.VMEM((2, tm, D_MODEL), F32),
                pltpu.SemaphoreType.DMA((2,)), pltpu.SemaphoreType.DMA((2,)),
            ],
        ),
        compiler_params=_cparams("arbitrary"),
        name="moe_pairs",
    )(order, first, e_lo, e_hi, n_valid, n_ctx, x1c, x1l, mods, wg, wu, wd, wg, wu, wd)


PROJ_TILE = 512
CAST_GROUPS = N_EXPERTS


def _mixers(x, p, mods, row0, row_step, h0, casts_in, casts_attn, k_ctx=None, v_ctx=None):
    latent = k_ctx is not None
    bsz, seq, _ = x.shape
    tm = PROJ_TILE if (row_step == 0 or seq % PROJ_TILE == 0) else seq
    rows = bsz * seq
    flat = lambda a: a.reshape(1 if row_step == 0 else bsz, rows if row_step == 0 else seq, a.shape[-1])
    (z, q, k, v, lg, lx), cast_a = _inproj(flat(x), mods, row0, row_step, p["norm1_g"], p["w_in"], p["q_norm_g"],
                                           p["k_norm_g"], latent, tm, casts_in)
    seqs = lambda a: a.reshape(bsz, seq, a.shape[-1])
    z, q, k, v, lg, lx = map(seqs, (z, q, k, v, lg, lx))
    conv = _conv_module(z, p["conv_dw"], p["conv_b"], p["conv_ln_g"], p["conv_ln_b"])
    attn, cast_b = _attention(q, k, v, p["attn_sink"], k_ctx, v_ctx, casts_attn)
    lru, fin = _recurrent_mixer(lx, lg, h0, p["lru_conv_w"], p["lru_conv_b"], p["lru_wbd"], p["lru_gbias"],
                                p["lru_lam"])
    x1, route = _outproj(flat(conv), flat(attn), flat(lru), flat(x), mods, row0, row_step, p["norm2_g"],
                         p["w_out"], p["w_router_t"], p["b_router"], tm)
    return x1, route, k, v, fin, list(cast_a) + list(cast_b)


def kernel(x_prompt, x_sample, c, cache_k, cache_v, state_lru, c_ctx, w_mod, b_mod, norm1_g, norm2_g, w_in,
           conv_dw, conv_b, conv_ln_g, conv_ln_b, q_norm_g, k_norm_g, attn_sink, lru_conv_w, lru_conv_b,
           lru_wa, lru_ba, lru_wx, lru_bx, lru_lam, w_out, w_router, b_router, w_gate_e, w_up_e, w_down_e):
    bsz, seq, _ = x_prompt.shape
    dec_bsz, dec_seq, _ = x_sample.shape
    past = cache_k.shape[2]

    cvec = jnp.zeros((MOD_ROWS, D_MODEL), F32).at[0].set(c_ctx).at[1:1 + dec_bsz].set(c)
    mods_all = _modulation(cvec, w_mod, b_mod).reshape(DEPTH, MOD_ROWS, 6, D_MODEL)

    wr_hi = w_router.T.astype(BF16)
    wr_lo = (w_router.T - wr_hi.astype(F32)).astype(BF16)
    w_router_t = jnp.concatenate([wr_hi, wr_lo], axis=0)
    w_in_bf, w_out_bf = w_in[0].astype(BF16), w_out[0].astype(BF16)
    w_in_groups = w_in.reshape(DEPTH, CAST_GROUPS, D_MODEL // CAST_GROUPS, D_IN)
    w_out_groups = w_out.reshape(DEPTH, CAST_GROUPS, D_MODEL // CAST_GROUPS, D_MODEL)
    layers = []
    for l in range(DEPTH):
        wbd, gbias = _lru_gate_weights(lru_wa[l], lru_ba[l], lru_wx[l], lru_bx[l])
        layers.append({
            "norm1_g": norm1_g[l], "norm2_g": norm2_g[l],
            "conv_dw": conv_dw[l], "conv_b": conv_b[l], "conv_ln_g": conv_ln_g[l], "conv_ln_b": conv_ln_b[l],
            "q_norm_g": q_norm_g[l], "k_norm_g": k_norm_g[l], "attn_sink": attn_sink[l],
            "lru_conv_w": lru_conv_w[l], "lru_conv_b": lru_conv_b[l], "lru_wbd": wbd, "lru_gbias": gbias,
            "lru_lam": lru_lam[l],
            "w_router_t": w_router_t, "b_router": b_router,
        })

    y, z = x_prompt, x_sample
    ks, vs, hs = [], [], []
    h0_ctx = jnp.zeros((bsz, 2, D_LRU), F32)
    for l in range(DEPTH):
        p, mods = dict(layers[l], w_in=w_in_bf, w_out=w_out_bf), mods_all[l]
        y1, route_c, k_l, v_l, h_l, _ = _mixers(y, p, mods, 0, 0, h0_ctx, [], [])
        ks.append(k_l.reshape(bsz, seq, N_KV_HEADS, HEAD_DIM))
        vs.append(v_l.reshape(bsz, seq, N_KV_HEADS, HEAD_DIM))
        hs.append(h_l)
        casts_attn = [(w_gate_e, l), (w_down_e, l)]
        if l + 1 < DEPTH:
            casts_attn += [(w_in_groups, l + 1), (w_out_groups, l + 1)]
        z1, route_l, _, _, _, (wu_bf, wg_bf, wd_bf, *next_proj) = _mixers(
            z, p, mods, 1, 1, state_lru[:, l], [(w_up_e, l)], casts_attn,
            cache_k[:, l].reshape(dec_bsz, past, D_KV).astype(BF16),
            cache_v[:, l].reshape(dec_bsz, past, D_KV).astype(BF16))
        if next_proj:
            w_in_bf = next_proj[0].reshape(D_MODEL, D_IN)
            w_out_bf = next_proj[1].reshape(D_MODEL, D_MODEL)
        y, z = _moe(y1.reshape(bsz * seq, -1), z1.reshape(dec_bsz * dec_seq, -1), route_c, route_l,
                    mods, wg_bf, wu_bf, wd_bf, 1 + dec_bsz)
        y, z = y.reshape(bsz, seq, D_MODEL), z.reshape(dec_bsz, dec_seq, D_MODEL)
    new_cache_k = jnp.stack(ks, axis=1)
    new_cache_v = jnp.stack(vs, axis=1)
    new_state_lru = jnp.stack(hs, axis=1)
    return y, z, new_cache_k, new_cache_v, new_state_lru
```

```python
import functools

import numpy as np
import jax
import jax.numpy as jnp
from jax import lax
from jax.experimental import pallas as pl
from jax.experimental.pallas import tpu as pltpu

D_MODEL = 2048
DEPTH = 2
GRID_W = 64
D_CONV = 512
N_HEADS = 8
N_KV_HEADS = 2
HEAD_DIM = 128
GROUP = N_HEADS // N_KV_HEADS
D_ATTN = N_HEADS * HEAD_DIM
D_KV = N_KV_HEADS * HEAD_DIM
D_LRU = 512
CONV_K = 31
LRU_CONV_K = 4
LRU_BLOCKS = 8
LRU_BLK = D_LRU // LRU_BLOCKS
LRU_C = 8.0
WINDOW = 128
ROPE_BASE = 10000.0
ATTN_SCALE = HEAD_DIM ** -0.5
N_EXPERTS = 16
N_EXPERT_GROUPS = 4
EXPERTS_PER_GROUP = N_EXPERTS // N_EXPERT_GROUPS
D_EXPERT = 512
EPS = 1e-6
NEG = -1e30
D_IN = 2 * D_CONV + D_ATTN + 2 * D_KV + 2 * D_LRU
N_MOD = 6 * D_MODEL
SUB = 8
MOD_ROWS = SUB

V7X_VMEM_BYTES = 64 * 1024 * 1024
VMEM_LIMIT = V7X_VMEM_BYTES - 8 * 1024 * 1024

F32 = jnp.float32
BF16 = jnp.bfloat16


def _cparams(*sem):
    return pltpu.CompilerParams(dimension_semantics=sem, vmem_limit_bytes=VMEM_LIMIT)


def _sigmoid(x):
    return 0.5 * jnp.tanh(0.5 * x) + 0.5


def _silu(x):
    return x * _sigmoid(x)


def _const_spec(shape):
    return pl.BlockSpec(shape, lambda *_: (0,) * len(shape), pipeline_mode=pl.Buffered(1))


def _split_bf16(x):
    hi = x.astype(BF16)
    return hi, (x - hi.astype(F32)).astype(BF16)


def _mod_kernel(c_ref, w_ref, b_ref, o_ref):
    s = _silu(c_ref[...])
    s_hi = s.astype(BF16).astype(F32)
    s_both = jnp.concatenate([s_hi, s - s_hi], axis=0).astype(BF16)
    w_hi, w_lo = _split_bf16(w_ref[0])
    by_hi = jnp.dot(s_both, w_hi, preferred_element_type=F32)
    by_lo = jnp.dot(s_both, w_lo, preferred_element_type=F32)
    o_ref[0] = by_hi[0:MOD_ROWS] + by_hi[MOD_ROWS:] + by_lo[0:MOD_ROWS] + b_ref[0]


def _modulation(cvec, w_mod, b_mod):
    tn = 1536
    return pl.pallas_call(
        _mod_kernel,
        out_shape=jax.ShapeDtypeStruct((DEPTH, MOD_ROWS, N_MOD), F32),
        grid=(DEPTH, N_MOD // tn),
        in_specs=[
            _const_spec((MOD_ROWS, D_MODEL)),
            pl.BlockSpec((1, D_MODEL, tn), lambda l, j: (l, 0, j)),
            pl.BlockSpec((1, 1, tn), lambda l, j: (l, 0, j)),
        ],
        out_specs=pl.BlockSpec((1, MOD_ROWS, tn), lambda l, j: (l, 0, j)),
        compiler_params=_cparams("parallel", "parallel"),
        name="modulation",
    )(cvec, w_mod, b_mod.reshape(DEPTH, 1, N_MOD))


def _head_norm(x, g):
    return x * lax.rsqrt(jnp.mean(x * x, axis=-1, keepdims=True) + EPS) * g


def _rope(x, cos, sin_signed):
    lane = lax.broadcasted_iota(jnp.int32, x.shape, 1)
    partner = jnp.where((lane % 64) < 32, pltpu.roll(x, 96, 1), pltpu.roll(x, 32, 1))
    return x * cos + partner * sin_signed


def _inproj_kernel(*refs, rope, n_cast):
    refs = list(refs)
    cast_dst = [refs.pop() for _ in range(n_cast)][::-1]
    z_ref, q_ref, k_ref, v_ref, lg_ref, lx_ref = refs[-6:]
    del refs[-6:]
    cast_src = [refs.pop() for _ in range(n_cast)][::-1]
    if rope:
        x_ref, mod_ref, g1_ref, w_ref, qg_ref, kg_ref, cos_ref, sin_ref = refs
    else:
        x_ref, mod_ref, g1_ref, w_ref, qg_ref, kg_ref = refs
    for src, dst in zip(cast_src, cast_dst):
        dst[0] = src[0, 0].astype(BF16)
    x = x_ref[0]
    mod = mod_ref[0]
    shift, scale = mod[0:1], mod[1:2]
    y = x * lax.rsqrt(jnp.mean(x * x, axis=-1, keepdims=True) + EPS) * g1_ref[...]
    h = (y * (1.0 + scale) + shift).astype(BF16)

    def proj(c0, c1):
        return jnp.dot(h, w_ref[:, c0:c1], preferred_element_type=F32)

    glu = proj(0, 2 * D_CONV)
    z_ref[0] = glu[:, :D_CONV] * _sigmoid(glu[:, D_CONV:])

    c0 = 2 * D_CONV
    q = proj(c0, c0 + D_ATTN)
    qg = qg_ref[...]
    for hd in range(N_HEADS):
        qh = _head_norm(q[:, hd * HEAD_DIM:(hd + 1) * HEAD_DIM], qg)
        if rope:
            qh = _rope(qh, cos_ref[...], sin_ref[...])
        q_ref[0, :, hd * HEAD_DIM:(hd + 1) * HEAD_DIM] = (qh * ATTN_SCALE).astype(BF16)

    c0 += D_ATTN
    kv = proj(c0, c0 + 2 * D_KV)
    kg = kg_ref[...]
    for hd in range(N_KV_HEADS):
        kh = _head_norm(kv[:, hd * HEAD_DIM:(hd + 1) * HEAD_DIM], kg)
        if rope:
            kh = _rope(kh, cos_ref[...], sin_ref[...])
        k_ref[0, :, hd * HEAD_DIM:(hd + 1) * HEAD_DIM] = kh.astype(k_ref.dtype)
    v_ref[0] = kv[:, D_KV:].astype(v_ref.dtype)

    c0 += 2 * D_KV
    l2 = proj(c0, c0 + 2 * D_LRU)
    lg_ref[0] = l2[:, :D_LRU]
    lx_ref[0] = l2[:, D_LRU:]


def _rope_tables(seq):
    n_freq = HEAD_DIM // 4
    inv = (ROPE_BASE ** (-np.arange(n_freq, dtype=np.float32) / n_freq)).astype(np.float32)
    t = np.arange(seq)
    ang_r = (t // GRID_W).astype(np.float32)[:, None] * inv[None, :]
    ang_c = (t % GRID_W).astype(np.float32)[:, None] * inv[None, :]
    cos = np.concatenate([np.cos(ang_r)] * 2 + [np.cos(ang_c)] * 2, axis=-1)
    sin = np.concatenate([-np.sin(ang_r), np.sin(ang_r), -np.sin(ang_c), np.sin(ang_c)], axis=-1)
    return jnp.asarray(cos, F32), jnp.asarray(sin, F32)


def _add_casts(cast_weights, n_inner, n_steps, in_specs, args, out_specs, out_shape):
    for w, layer in cast_weights:
        _, n_g, rows, cols = w.shape
        parts = n_steps // n_g
        assert parts * n_g == n_steps and rows % parts == 0
        piece = lambda b, i, parts=parts: divmod(b * n_inner + i, parts)
        in_specs.append(pl.BlockSpec((1, 1, rows // parts, cols),
                                     lambda b, i, layer=layer, piece=piece: (layer, *piece(b, i), 0)))
        args.append(w)
        out_specs.append(pl.BlockSpec((1, rows // parts, cols), lambda b, i, piece=piece: (*piece(b, i), 0)))
        out_shape.append(jax.ShapeDtypeStruct((n_g, rows, cols), BF16))


def _inproj(x, mods, row0, row_step, g1, w_in_bf, qg, kg, rope, tm, cast_weights):
    bsz, seq, _ = x.shape
    tok = lambda width: pl.BlockSpec((1, tm, width), lambda b, i: (b, i, 0))
    in_specs = [
        tok(D_MODEL),
        pl.BlockSpec((1, 6, D_MODEL), lambda b, i: (row0 + row_step * b, 0, 0)),
        _const_spec((1, D_MODEL)),
        _const_spec((D_MODEL, D_IN)),
        _const_spec((1, HEAD_DIM)),
        _const_spec((1, HEAD_DIM)),
    ]
    args = [x, mods, g1.reshape(1, D_MODEL), w_in_bf, qg.reshape(1, HEAD_DIM), kg.reshape(1, HEAD_DIM)]
    if rope:
        cos, sin = _rope_tables(seq)
        in_specs += [pl.BlockSpec((tm, HEAD_DIM), lambda b, i: (i, 0))] * 2
        args += [cos, sin]
    widths = (D_CONV, D_ATTN, D_KV, D_KV, D_LRU, D_LRU)
    kv_dtype = BF16 if rope else F32
    dtypes = (F32, BF16, kv_dtype, kv_dtype, F32, F32)
    out_shape = [jax.ShapeDtypeStruct((bsz, seq, w), dt) for w, dt in zip(widths, dtypes)]
    out_specs = [tok(w) for w in widths]
    n_inner = seq // tm
    _add_casts(cast_weights, n_inner, bsz * n_inner, in_specs, args, out_specs, out_shape)
    outs = pl.pallas_call(
        functools.partial(_inproj_kernel, rope=rope, n_cast=len(cast_weights)),
        out_shape=out_shape,
        grid=(bsz, n_inner),
        in_specs=in_specs,
        out_specs=out_specs,
        compiler_params=_cparams("parallel", "parallel"),
        name="inproj_rope" if rope else "inproj",
    )(*args)
    return outs[:6], outs[6:]


CONV_PAD = (CONV_K - 1) // 2
CONV_HALO = 16
CONV_CHUNK = 64
CONV_COLS = 256
CONV_GROUP = 4


def _conv_kernel(z_ref, w_ref, b_ref, g_ref, beta_ref, o_ref, zp_ref, win_ref, acc_ref, *, seq):
    zeros = jnp.zeros((CONV_HALO, D_CONV), F32)
    zp_ref[0:CONV_HALO, :] = zeros
    zp_ref[CONV_HALO + seq:2 * CONV_HALO + seq, :] = zeros
    zp_ref[CONV_HALO:CONV_HALO + seq, :] = z_ref[0]
    n_parts = D_CONV // CONV_COLS

    group_rows = CONV_GROUP * CONV_CHUNK

    def body(c, carry):
        r0 = pl.multiple_of(c * group_rows, group_rows)
        for q in range(CONV_GROUP):
            for p in range(n_parts):
                win_ref[q * n_parts + p] = zp_ref[pl.ds(r0 + q * CONV_CHUNK, CONV_CHUNK + 2 * CONV_HALO),
                                                  p * CONV_COLS:(p + 1) * CONV_COLS]

        def taps(j, inner):
            p = lax.rem(j, n_parts)
            part = jnp.broadcast_to(b_ref[p], (CONV_CHUNK, CONV_COLS))
            for k in range(CONV_K):
                off = CONV_HALO - CONV_PAD + k
                part = part + jnp.tile(w_ref[p, k], (CONV_CHUNK // SUB, 1)) * win_ref[j, off:off + CONV_CHUNK, :]
            acc_ref[j] = part
            return inner

        lax.fori_loop(0, CONV_GROUP * n_parts, taps, 0)
        acc = jnp.concatenate(
            [jnp.concatenate([acc_ref[q * n_parts + p] for p in range(n_parts)], axis=1) for q in range(CONV_GROUP)],
            axis=0)
        mu = jnp.mean(acc, axis=-1, keepdims=True)
        xc = acc - mu
        var = jnp.mean(xc * xc, axis=-1, keepdims=True)
        y = xc * lax.rsqrt(var + EPS) * g_ref[...] + beta_ref[...]
        o_ref[0, pl.ds(r0, group_rows), :] = _silu(y).astype(BF16)
        return carry

    lax.fori_loop(0, seq // group_rows, body, 0)


def _conv_module(z, w, b, g, beta):
    bsz, seq, _ = z.shape
    row = lambda a: a.reshape(1, D_CONV)
    n_parts = D_CONV // CONV_COLS
    w_parts = jnp.broadcast_to(w.reshape(CONV_K, 1, n_parts, CONV_COLS), (CONV_K, SUB, n_parts, CONV_COLS))
    w_parts = w_parts.transpose(2, 0, 1, 3)
    return pl.pallas_call(
        functools.partial(_conv_kernel, seq=seq),
        out_shape=jax.ShapeDtypeStruct((bsz, seq, D_CONV), BF16),
        grid=(bsz,),
        in_specs=[
            pl.BlockSpec((1, seq, D_CONV), lambda i: (i, 0, 0)),
            _const_spec((n_parts, CONV_K, SUB, CONV_COLS)),
            _const_spec((n_parts, 1, CONV_COLS)), _const_spec((1, D_CONV)), _const_spec((1, D_CONV)),
        ],
        out_specs=pl.BlockSpec((1, seq, D_CONV), lambda i: (i, 0, 0)),
        scratch_shapes=[pltpu.VMEM((seq + 2 * CONV_HALO, D_CONV), F32),
                        pltpu.VMEM((CONV_GROUP * n_parts, CONV_CHUNK + 2 * CONV_HALO, CONV_COLS), F32),
                        pltpu.VMEM((CONV_GROUP * n_parts, CONV_CHUNK, CONV_COLS), F32)],
        compiler_params=_cparams("parallel"),
        name="conv_module",
    )(z, w_parts, b.reshape(n_parts, 1, CONV_COLS), row(g), row(beta))


ATTN_ROWS = 64


def _attn_kernel(*refs, latent, seq, tq, n_cast):
    n_in = 6 if latent else 4
    cast_src = refs[n_in:n_in + n_cast]
    cast_dst = refs[n_in + n_cast + 1:n_in + 2 * n_cast + 1]
    refs = refs[:n_in] + refs[n_in + n_cast:n_in + n_cast + 1] + refs[n_in + 2 * n_cast + 1:]
    if latent:
        sink_ref, q_ref, k_ref, v_ref, kc_ref, vc_ref, o_ref, s_ref, p_ref, bias_ref = refs
    else:
        sink_ref, q_ref, k_ref, v_ref, o_ref, s_ref, p_ref = refs
    for src, dst in zip(cast_src, cast_dst):
        dst[0] = src[0, 0].astype(BF16)
    q = q_ref[0]
    n_loc = 3 * tq if latent else 0
    if latent:
        n = pl.program_id(1)
        nblk = seq // tq
        blocks = (jnp.maximum(n - 1, 0), n, jnp.minimum(n + 1, nblk - 1))

        def window(ref):
            parts = [ref[0, pl.ds(pl.multiple_of(i * tq, tq), tq), :] for i in blocks]
            return jnp.concatenate(parts, axis=0)

        k_all = jnp.concatenate([window(k_ref), kc_ref[0]], axis=0).astype(BF16)
        v_all = jnp.concatenate([window(v_ref), vc_ref[0]], axis=0).astype(BF16)
        qpos = n * tq + lax.broadcasted_iota(jnp.int32, (tq, 3 * tq), 0)
        kpos = (n - 1) * tq + lax.broadcasted_iota(jnp.int32, (tq, 3 * tq), 1)
        ok = (jnp.abs(qpos - kpos) <= WINDOW) & (kpos >= 0) & (kpos < seq)
        bias_ref[...] = jnp.where(ok, 0.0, NEG).astype(F32)
    else:
        k_all = k_ref[0].astype(BF16)
        v_all = v_ref[0].astype(BF16)

    for j in range(N_KV_HEADS):
        heads = [j * GROUP + g for g in range(GROUP)]
        qs = jnp.concatenate([q[:, h * HEAD_DIM:(h + 1) * HEAD_DIM] for h in heads], axis=0)
        kj = k_all[:, j * HEAD_DIM:(j + 1) * HEAD_DIM]
        vj = v_all[:, j * HEAD_DIM:(j + 1) * HEAD_DIM]
        s_ref[...] = lax.dot_general(qs, kj, (((1,), (1,)), ((), ())), preferred_element_type=F32)
        inv = []
        for rb in range(GROUP * tq // ATTN_ROWS):
            rows = slice(rb * ATTN_ROWS, (rb + 1) * ATTN_ROWS)
            sk = sink_ref[heads[rb * ATTN_ROWS // tq]]
            parts = [s_ref[rows, n_loc:]]
            if latent:
                q0 = rb * ATTN_ROWS % tq
                parts.insert(0, s_ref[rows, :n_loc] + bias_ref[q0:q0 + ATTN_ROWS, :])
            m = sk
            for s in parts:
                m = jnp.maximum(m, jnp.max(s, axis=-1, keepdims=True))
            den = jnp.exp(sk - m)
            c0 = 0
            for s in parts:
                p = jnp.exp(s - m)
                den = den + jnp.sum(p, axis=-1, keepdims=True)
                p_ref[rows, c0:c0 + s.shape[1]] = p.astype(BF16)
                c0 += s.shape[1]
            inv.append(1.0 / den)
        o = jnp.dot(p_ref[...], vj, preferred_element_type=F32) * jnp.concatenate(inv, axis=0)
        for g, h in enumerate(heads):
            o_ref[0, :, h * HEAD_DIM:(h + 1) * HEAD_DIM] = o[g * tq:(g + 1) * tq].astype(BF16)


def _attention(q, k, v, sink, k_ctx=None, v_ctx=None, cast_weights=()):
    bsz, seq, _ = q.shape
    latent = k_ctx is not None
    tq = WINDOW if latent else seq
    seq_spec = pl.BlockSpec((1, seq, D_KV), lambda b, i: (b, 0, 0))
    in_specs = [
        pl.BlockSpec(memory_space=pltpu.SMEM),
        pl.BlockSpec((1, tq, D_ATTN), lambda b, i: (b, i, 0)),
        seq_spec, seq_spec,
    ]
    args = [sink, q, k, v]
    n_keys = seq
    if latent:
        past = k_ctx.shape[1]
        ctx_spec = pl.BlockSpec((1, past, D_KV), lambda b, i: (b, 0, 0))
        in_specs += [ctx_spec, ctx_spec]
        args += [k_ctx, v_ctx]
        n_keys = 3 * tq + past
    scratch = [pltpu.VMEM((GROUP * tq, n_keys), F32), pltpu.VMEM((GROUP * tq, n_keys), BF16)]
    if latent:
        scratch.append(pltpu.VMEM((tq, 3 * tq), F32))
    out_shape = [jax.ShapeDtypeStruct((bsz, seq, D_ATTN), BF16)]
    out_specs = [pl.BlockSpec((1, tq, D_ATTN), lambda b, i: (b, i, 0))]
    n_inner = seq // tq
    _add_casts(cast_weights, n_inner, bsz * n_inner, in_specs, args, out_specs, out_shape)
    outs = pl.pallas_call(
        functools.partial(_attn_kernel, latent=latent, seq=seq, tq=tq, n_cast=len(cast_weights)),
        out_shape=out_shape,
        grid=(bsz, n_inner),
        in_specs=in_specs,
        out_specs=out_specs,
        scratch_shapes=scratch,
        compiler_params=_cparams("parallel", "parallel"),
        name="attn_latent" if latent else "attn_context",
    )(*args)
    return outs[0], outs[1:]


LRU_HALO = 8
LRU_CHUNK = 128
LRU_HALF = D_LRU // 2


def _softplus(x):
    return jnp.maximum(x, 0.0) + jnp.log(1.0 + jnp.exp(-jnp.abs(x)))


def _gelu_tanh(x):
    return 0.5 * x * (1.0 + jnp.tanh(0.7978845608028654 * (x + 0.044715 * (x * x * x))))


def _scan_tile(a, b, carry, reverse):
    row = lax.broadcasted_iota(jnp.int32, a.shape, 0)
    for d in (1, 2, 4):
        if reverse:
            valid = row < SUB - d
            shift = SUB - d
        else:
            valid = row >= d
            shift = d
        a_prev = jnp.where(valid, pltpu.roll(a, shift, 0), 1.0)
        b_prev = jnp.where(valid, pltpu.roll(b, shift, 0), 0.0)
        b = a * b_prev + b
        a = a * a_prev
    h = a * carry + b
    last = h[0:1, :] if reverse else h[SUB - 1:SUB, :]
    return h, last


def _lru_kernel(lx_ref, lg_ref, h0_ref, cw_ref, cb_ref, wbd_ref, gbias_ref, lam_ref,
                o_ref, fin_ref, xp_ref, win_ref, af_ref, bf_ref, ab_ref, bb_ref, *, seq):
    zeros = jnp.zeros((LRU_HALO, D_LRU), F32)
    xp_ref[0:LRU_HALO, :] = zeros
    xp_ref[LRU_HALO + seq:2 * LRU_HALO + seq, :] = zeros
    xp_ref[LRU_HALO:LRU_HALO + seq, :] = lx_ref[0]
    a_refs = (af_ref, ab_ref)
    b_refs = (bf_ref, bb_ref)
    half_c = (-0.5 * LRU_C) * _softplus(-lam_ref[...])

    def gates(c, carry):
        r0 = pl.multiple_of(c * LRU_CHUNK, LRU_CHUNK)
        win_ref[...] = xp_ref[pl.ds(r0, LRU_CHUNK + 2 * LRU_HALO), :]
        xc = jnp.broadcast_to(cb_ref[...], (LRU_CHUNK, D_LRU))
        for k in range(LRU_CONV_K):
            off = LRU_HALO - 2 + k
            xc = xc + cw_ref[k:k + 1, :] * win_ref[off:off + LRU_CHUNK, :]
        for s in range(2):
            cols = slice(s * LRU_HALF, (s + 1) * LRU_HALF)
            xs = xc[:, cols]
            g = jnp.dot(xs.astype(BF16), wbd_ref[s], preferred_element_type=F32)
            g = g + gbias_ref[s:s + 1, :]
            for d in range(2):
                base = d * 2 * LRU_HALF
                hc = half_c[d:d + 1, cols]
                log_a = hc * jnp.tanh(g[:, base:base + LRU_HALF]) + hc
                i = 0.5 * jnp.tanh(g[:, base + LRU_HALF:base + 2 * LRU_HALF]) + 0.5
                a = jnp.exp(log_a)
                t = jnp.tanh(log_a)
                b = jnp.sqrt(-2.0 * t / (1.0 - t)) * (i * xs)
                a_refs[d][pl.ds(r0, LRU_CHUNK), cols] = a
                b_refs[d][pl.ds(r0, LRU_CHUNK), cols] = b
        return carry

    lax.fori_loop(0, seq // LRU_CHUNK, gates, 0)

    ntile = seq // SUB

    def scan(t, carry):
        cf, cb = carry
        rf = pl.multiple_of(t * SUB, SUB)
        rb = pl.multiple_of((ntile - 1 - t) * SUB, SUB)
        hf, cf = _scan_tile(af_ref[pl.ds(rf, SUB), :], bf_ref[pl.ds(rf, SUB), :], cf, False)
        hb, cb = _scan_tile(ab_ref[pl.ds(rb, SUB), :], bb_ref[pl.ds(rb, SUB), :], cb, True)
        bf_ref[pl.ds(rf, SUB), :] = hf
        bb_ref[pl.ds(rb, SUB), :] = hb
        return cf, cb

    h0 = h0_ref[0]
    cf, cb = lax.fori_loop(0, ntile, scan, (h0[0:1, :], h0[1:2, :]))
    fin_ref[0, 0:1, :] = cf
    fin_ref[0, 1:2, :] = cb

    def finish(c, carry):
        r0 = pl.multiple_of(c * LRU_CHUNK, LRU_CHUNK)
        rows = pl.ds(r0, LRU_CHUNK)
        o_ref[0, rows, :] = ((bf_ref[rows, :] + bb_ref[rows, :]) * _gelu_tanh(lg_ref[0, rows, :])).astype(BF16)
        return carry

    lax.fori_loop(0, seq // LRU_CHUNK, finish, 0)


def _lru_gate_weights(wa, ba, wx, bx):
    per_half = LRU_HALF // LRU_BLK
    on_diagonal = jnp.asarray(np.kron(np.eye(per_half), np.ones((LRU_BLK, LRU_BLK))), F32)

    def dense(w):
        blocks = w.reshape(2, per_half, LRU_BLK, 1, LRU_BLK)
        tiled = jnp.broadcast_to(blocks, (2, per_half, LRU_BLK, per_half, LRU_BLK))
        return tiled.reshape(2, LRU_HALF, LRU_HALF) * on_diagonal

    wbd = jnp.concatenate([dense(wa[0]), dense(wx[0]), dense(wa[1]), dense(wx[1])], axis=2)
    gbias = jnp.stack([ba[0], bx[0], ba[1], bx[1]]).reshape(4, 2, LRU_HALF).transpose(1, 0, 2)
    return (0.5 * wbd).astype(BF16), 0.5 * gbias.reshape(2, 4 * LRU_HALF)


def _recurrent_mixer(lx, lg, h0, cw, cb, wbd, gbias, lam):
    bsz, seq, _ = lx.shape
    seq_spec = pl.BlockSpec((1, seq, D_LRU), lambda i: (i, 0, 0))
    state_spec = pl.BlockSpec((1, 2, D_LRU), lambda i: (i, 0, 0))
    return pl.pallas_call(
        functools.partial(_lru_kernel, seq=seq),
        out_shape=[jax.ShapeDtypeStruct((bsz, seq, D_LRU), BF16),
                   jax.ShapeDtypeStruct((bsz, 2, D_LRU), F32)],
        grid=(bsz,),
        in_specs=[
            seq_spec, seq_spec, state_spec,
            _const_spec((LRU_CONV_K, D_LRU)), _const_spec((1, D_LRU)),
            _const_spec((2, LRU_HALF, 4 * LRU_HALF)), _const_spec((2, 4 * LRU_HALF)),
            _const_spec((2, D_LRU)),
        ],
        out_specs=[seq_spec, state_spec],
        scratch_shapes=[pltpu.VMEM((seq + 2 * LRU_HALO, D_LRU), F32),
                        pltpu.VMEM((LRU_CHUNK + 2 * LRU_HALO, D_LRU), F32)] + [pltpu.VMEM((seq, D_LRU), F32)] * 4,
        compiler_params=_cparams("parallel"),
        name="rglru",
    )(lx, lg, h0, cw, cb.reshape(1, D_LRU), wbd, gbias, lam)


N_PAIRS = EXPERTS_PER_GROUP * (EXPERTS_PER_GROUP - 1) // 2
N_BUCKETS = N_EXPERT_GROUPS * N_PAIRS
PAIR_LO = (0, 0, 0, 1, 1, 2)
PAIR_HI = (1, 2, 3, 2, 3, 3)
ROUTE_ROWS = SUB


def _route(scores, biased):
    rows = [biased[e:e + 1, :] for e in range(N_EXPERTS)]
    group_score = []
    for g in range(N_EXPERT_GROUPS):
        a, b, c, d = rows[4 * g:4 * g + 4]
        hi1, lo1 = jnp.maximum(a, b), jnp.minimum(a, b)
        hi2, lo2 = jnp.maximum(c, d), jnp.minimum(c, d)
        top = jnp.maximum(hi1, hi2)
        second = jnp.maximum(jnp.minimum(hi1, hi2), jnp.maximum(lo1, lo2))
        group_score.append(top + second)
    best = group_score[0]
    g_sel = jnp.zeros_like(best, dtype=jnp.int32)
    for g in range(1, N_EXPERT_GROUPS):
        better = group_score[g] > best
        g_sel = jnp.where(better, g, g_sel)
        best = jnp.where(better, group_score[g], best)
    sel = []
    for e in range(N_EXPERTS):
        g = e // EXPERTS_PER_GROUP
        rank = jnp.zeros_like(g_sel)
        for o in range(g * EXPERTS_PER_GROUP, (g + 1) * EXPERTS_PER_GROUP):
            if o == e:
                continue
            ahead = (rows[o] >= rows[e]) if o < e else (rows[o] > rows[e])
            rank = rank + ahead.astype(jnp.int32)
        sel.append(jnp.where(jnp.where(g_sel == g, rank, 2) < 2, 1, 0))
    zero = jnp.zeros_like(best)
    lo_w, hi_w = zero, zero
    lo_idx = jnp.zeros_like(g_sel)
    hi_idx = jnp.zeros_like(g_sel)
    for g in range(N_EXPERT_GROUPS):
        seen = jnp.zeros_like(g_sel)
        for j in range(EXPERTS_PER_GROUP):
            e = g * EXPERTS_PER_GROUP + j
            order = jnp.where(sel[e] == 1, seen, 2)
            lo_w = jnp.where(order == 0, scores[e:e + 1, :], lo_w)
            hi_w = jnp.where(order == 1, scores[e:e + 1, :], hi_w)
            lo_idx = jnp.where(order == 0, j, lo_idx)
            hi_idx = jnp.where(order == 1, j, hi_idx)
            seen = seen + sel[e]
    pair_base = jnp.where(lo_idx == 0, 0, jnp.where(lo_idx == 1, 3, 5))
    bucket = g_sel * N_PAIRS + pair_base + hi_idx - lo_idx - 1
    total = lo_w + hi_w
    pad = jnp.zeros((ROUTE_ROWS - 3, best.shape[1]), F32)
    return jnp.concatenate([bucket.astype(F32), lo_w / total, hi_w / total, pad], axis=0)


def _modulated_norm(x, g, shift, scale):
    y = x * lax.rsqrt(jnp.mean(x * x, axis=-1, keepdims=True) + EPS) * g
    return y * (1.0 + scale) + shift


META_LANES = 128
ROW_LANES = 2 * D_MODEL + META_LANES
META_W_LO, META_W_HI, META_MOD_ROW = 1, 2, 3


def _outproj_kernel(conv_ref, attn_ref, lru_ref, x_ref, mod_ref, g2_ref, w_ref, wr_ref, br_ref,
                    x1_ref, route_ref, cat_ref, *, row0, row_step):
    cat_ref[:, 0:D_CONV] = conv_ref[0]
    cat_ref[:, D_CONV:D_CONV + D_ATTN] = attn_ref[0]
    cat_ref[:, D_CONV + D_ATTN:] = lru_ref[0]
    mod = mod_ref[0]
    tm = cat_ref.shape[0]
    mix = jnp.dot(cat_ref[...], w_ref[...], preferred_element_type=F32)
    x1 = x_ref[0] + mod[2:3] * mix
    x1_ref[0, :, 0:D_MODEL] = x1
    h2 = _modulated_norm(x1, g2_ref[...], mod[3:4], mod[4:5])
    x1_ref[0, :, D_MODEL:2 * D_MODEL] = h2
    h_hi, h_lo = _split_bf16(h2)
    nt = (((1,), (1,)), ((), ()))
    by_hi = lax.dot_general(wr_ref[...], h_hi, nt, preferred_element_type=F32)
    by_lo = lax.dot_general(wr_ref[0:N_EXPERTS, :], h_lo, nt, preferred_element_type=F32)
    logits = by_hi[0:N_EXPERTS] + by_hi[N_EXPERTS:] + by_lo
    scores = _sigmoid(logits)
    route = _route(scores, scores + br_ref[...])
    route_ref[0] = route
    mod_row = jnp.full((1, tm), row0, jnp.int32) + row_step * pl.program_id(0)
    meta = jnp.concatenate([route[0:META_MOD_ROW], mod_row.astype(F32),
                            jnp.zeros((META_LANES - META_MOD_ROW - 1, tm), F32)], axis=0)
    x1_ref[0, :, 2 * D_MODEL:] = meta.T


def _outproj(conv, attn, lru, x, mods, row0, row_step, g2, w_out_bf, w_router_t, b_router, tm):
    bsz, seq, _ = x.shape
    tok = lambda width: pl.BlockSpec((1, tm, width), lambda b, i: (b, i, 0))
    return pl.pallas_call(
        functools.partial(_outproj_kernel, row0=row0, row_step=row_step),
        out_shape=[jax.ShapeDtypeStruct((bsz, seq, ROW_LANES), F32),
                   jax.ShapeDtypeStruct((bsz, ROUTE_ROWS, seq), F32)],
        grid=(bsz, seq // tm),
        in_specs=[
            tok(D_CONV), tok(D_ATTN), tok(D_LRU), tok(D_MODEL),
            pl.BlockSpec((1, 6, D_MODEL), lambda b, i: (row0 + row_step * b, 0, 0)),
            _const_spec((1, D_MODEL)),
            _const_spec((D_MODEL, D_MODEL)),
            _const_spec((2 * N_EXPERTS, D_MODEL)),
            _const_spec((N_EXPERTS, 1)),
        ],
        out_specs=[tok(ROW_LANES), pl.BlockSpec((1, ROUTE_ROWS, tm), lambda b, i: (b, 0, i))],
        scratch_shapes=[pltpu.VMEM((tm, D_MODEL), BF16)],
        compiler_params=_cparams("parallel", "parallel"),
        name="outproj_router",
    )(conv, attn, lru, x, mods, g2.reshape(1, D_MODEL), w_out_bf, w_router_t, b_router.reshape(N_EXPERTS, 1))


MOE_TILE = 256
MOE_UNROLL = 8


def _moe_plan(bucket, t_ctx, tm):
    t_all = bucket.shape[0]
    n_tiles = (t_all + N_BUCKETS * (tm - 1) + tm - 1) // tm
    ids = jnp.arange(N_BUCKETS, dtype=jnp.int32)
    order = jnp.argsort(bucket, stable=True).astype(jnp.int32)
    member = (bucket[None, :] == ids[:, None]).astype(jnp.int32)
    counts = jnp.sum(member, axis=1)
    counts_ctx = jnp.sum(member[:, :t_ctx], axis=1)
    tiles_per = (counts + tm - 1) // tm
    tile_end = jnp.cumsum(tiles_per)
    tile_start = tile_end - tiles_per
    sorted_start = jnp.cumsum(counts) - counts
    n_used = tile_end[-1]
    j = jnp.arange(n_tiles, dtype=jnp.int32)
    jj = jnp.minimum(j, n_used - 1)
    b_of = jnp.sum((tile_end[None, :] <= jj[:, None]).astype(jnp.int32), axis=1)
    local = jj - tile_start[b_of]
    n_valid = jnp.where(j < n_used, jnp.clip(counts[b_of] - local * tm, 0, tm), 0)
    n_ctx = jnp.clip(counts_ctx[b_of] - local * tm, 0, n_valid)
    first = jnp.where(j < n_used, sorted_start[b_of] + local * tm, 0)
    group, pair = b_of // N_PAIRS, b_of % N_PAIRS
    e_lo = group * EXPERTS_PER_GROUP + jnp.asarray(PAIR_LO, jnp.int32)[pair]
    e_hi = group * EXPERTS_PER_GROUP + jnp.asarray(PAIR_HI, jnp.int32)[pair]
    i32 = lambda a: a.astype(jnp.int32)
    return order, i32(first), i32(e_lo), i32(e_hi), i32(n_valid), i32(n_ctx)


def _moe_kernel(order_ref, first_ref, elo_ref, ehi_ref, nv_ref, nc_ref,
                xc_ref, xl_ref, mods_ref,
                wg_lo, wu_lo, wd_lo, wg_hi, wu_hi, wd_hi,
                oc_ref, ol_ref, xbuf, ybuf, gsem, ssem, *, tm, t_ctx, n_mod_rows):
    i = pl.program_id(0)
    n = pl.num_programs(0)
    slot = lax.rem(i, 2)

    def gather(hbm, t, r, s, size):
        return pltpu.make_async_copy(hbm.at[pl.ds(t, size), :], xbuf.at[s, pl.ds(r, size), :], gsem.at[s])

    def scatter(hbm, t, r, s, size):
        return pltpu.make_async_copy(ybuf.at[s, pl.ds(r, size), :], hbm.at[pl.ds(t, size), :], ssem.at[s])

    def start_rows(j, s, copy, hbm_ctx, hbm_lat):
        base = first_ref[j]

        def ctx_row(r, priority):
            copy(hbm_ctx, order_ref[base + r], r, s, 1).start(priority=priority)

        def lat_row(r, priority):
            copy(hbm_lat, order_ref[base + r] - t_ctx, r, s, 1).start(priority=priority)

        def rows(lo, hi, one_row):
            groups = lax.shift_right_logical(hi - lo, MOE_UNROLL.bit_length() - 1)

            def group(g, c):
                for u in range(MOE_UNROLL):
                    one_row(lo + g * MOE_UNROLL + u, u % 2)
                return c

            def single(r, c):
                one_row(r, 0)
                return c

            lax.fori_loop(0, groups, group, 0)
            lax.fori_loop(lo + groups * MOE_UNROLL, hi, single, 0)

        rows(0, nc_ref[j], ctx_row)
        rows(nc_ref[j], nv_ref[j], lat_row)

    def wait_rows(j, s, copy, hbm):
        cnt = nv_ref[j]
        bulk = pl.multiple_of(lax.shift_left(lax.shift_right_logical(cnt, 3), 3), SUB)

        @pl.when(bulk > 0)
        def _():
            copy(hbm, 0, 0, s, bulk).wait()

        def one(r, c):
            copy(hbm, 0, 0, s, 1).wait()
            return c

        lax.fori_loop(bulk, cnt, one, 0)

    @pl.when(i == 0)
    def _():
        xbuf[...] = jnp.zeros_like(xbuf)
        start_rows(0, 0, gather, xc_ref, xl_ref)

    @pl.when(i + 1 < n)
    def _():
        start_rows(i + 1, 1 - slot, gather, xc_ref, xl_ref)

    wait_rows(i, slot, gather, xl_ref)

    @pl.when(i >= 2)
    def _():
        wait_rows(i - 2, slot, scatter, ol_ref)

    @pl.when(nv_ref[i] > 0)
    def _():
        x = xbuf[slot, :, 0:D_MODEL]
        h = xbuf[slot, :, D_MODEL:2 * D_MODEL].astype(BF16)
        meta = xbuf[slot, :, 2 * D_MODEL:]
        w_lo = meta[:, META_W_LO:META_W_LO + 1]
        w_hi = meta[:, META_W_HI:META_W_HI + 1]
        mrow = meta[:, META_MOD_ROW:META_MOD_ROW + 1]
        n_rows = x.shape[0]

        def per_row(*ks):
            ids = jnp.broadcast_to(mrow, (n_rows, 128))
            masks = [ids == float(r) for r in range(1, n_mod_rows)]
            cols = [[] for _ in ks]
            for c in range(D_MODEL // 128):
                lanes = slice(c * 128, (c + 1) * 128)
                for q, k in enumerate(ks):
                    v = jnp.broadcast_to(mods_ref[0, k:k + 1, lanes], (n_rows, 128))
                    for r, mask in enumerate(masks, 1):
                        v = jnp.where(mask, mods_ref[r, k:k + 1, lanes], v)
                    cols[q].append(v)
            return [jnp.concatenate(col, axis=1) for col in cols]

        def expert(wg, wu, w):
            act = _silu(jnp.dot(h, wg[0], preferred_element_type=F32))
            act = act * jnp.dot(h, wu[0], preferred_element_type=F32) * w
            return act.astype(BF16)

        y = jnp.dot(expert(wg_lo, wu_lo, w_lo), wd_lo[0], preferred_element_type=F32)
        y = y + jnp.dot(expert(wg_hi, wu_hi, w_hi), wd_hi[0], preferred_element_type=F32)
        (gate,) = per_row(5)
        ybuf[slot] = x + gate * y

    start_rows(i, slot, scatter, oc_ref, ol_ref)

    @pl.when(i == n - 1)
    def _():
        wait_rows(i, slot, scatter, ol_ref)

        @pl.when(i >= 1)
        def _():
            wait_rows(i - 1, 1 - slot, scatter, ol_ref)


def _moe(x1c, x1l, route_c, route_l, mods, wg, wu, wd, n_mod_rows):
    t_ctx, t_lat = x1c.shape[0], x1l.shape[0]
    tm = MOE_TILE
    bucket = jnp.concatenate([route_c[:, 0, :].reshape(t_ctx), route_l[:, 0, :].reshape(t_lat)])
    order, first, e_lo, e_hi, n_valid, n_ctx = _moe_plan(bucket.astype(jnp.int32), t_ctx, tm)
    n_tiles = n_valid.shape[0]
    w_up = lambda sel: pl.BlockSpec((1, D_MODEL, D_EXPERT),
                                    lambda i, order, first, lo, hi, nv, nc: ((lo, hi)[sel][i], 0, 0))
    w_dn = lambda sel: pl.BlockSpec((1, D_EXPERT, D_MODEL),
                                    lambda i, order, first, lo, hi, nv, nc: ((lo, hi)[sel][i], 0, 0))
    hbm = pl.BlockSpec(memory_space=pl.ANY)
    return pl.pallas_call(
        functools.partial(_moe_kernel, tm=tm, t_ctx=t_ctx, n_mod_rows=n_mod_rows),
        out_shape=[jax.ShapeDtypeStruct((t_ctx, D_MODEL), F32), jax.ShapeDtypeStruct((t_lat, D_MODEL), F32)],
        grid_spec=pltpu.PrefetchScalarGridSpec(
            num_scalar_prefetch=6,
            grid=(n_tiles,),
            in_specs=[
                hbm, hbm,
                _const_spec((MOD_ROWS, 6, D_MODEL)),
                w_up(0), w_up(0), w_dn(0), w_up(1), w_up(1), w_dn(1),
            ],
            out_specs=[hbm, hbm],
            scratch_shapes=[
                pltpu.VMEM((2, tm, ROW_LANES), F32), pltpu.VMEM((2, tm, D_MODEL), F32),
                pltpu.SemaphoreType.DMA((2,)), pltpu.SemaphoreType.DMA((2,)),
            ],
        ),
        compiler_params=_cparams("arbitrary"),
        name="moe_pairs",
    )(order, first, e_lo, e_hi, n_valid, n_ctx, x1c, x1l, mods, wg, wu, wd, wg, wu, wd)


PROJ_TILE = 512
CAST_GROUPS = N_EXPERTS


def _mixers(x, p, mods, row0, row_step, h0, casts_in, casts_attn, k_ctx=None, v_ctx=None):
    latent = k_ctx is not None
    bsz, seq, _ = x.shape
    tm = PROJ_TILE if (row_step == 0 or seq % PROJ_TILE == 0) else seq
    rows = bsz * seq
    flat = lambda a: a.reshape(1 if row_step == 0 else bsz, rows if row_step == 0 else seq, a.shape[-1])
    (z, q, k, v, lg, lx), cast_a = _inproj(flat(x), mods, row0, row_step, p["norm1_g"], p["w_in"], p["q_norm_g"],
                                           p["k_norm_g"], latent, tm, casts_in)
    seqs = lambda a: a.reshape(bsz, seq, a.shape[-1])
    z, q, k, v, lg, lx = map(seqs, (z, q, k, v, lg, lx))
    conv = _conv_module(z, p["conv_dw"], p["conv_b"], p["conv_ln_g"], p["conv_ln_b"])
    attn, cast_b = _attention(q, k, v, p["attn_sink"], k_ctx, v_ctx, casts_attn)
    lru, fin = _recurrent_mixer(lx, lg, h0, p["lru_conv_w"], p["lru_conv_b"], p["lru_wbd"], p["lru_gbias"],
                                p["lru_lam"])
    x1, route = _outproj(flat(conv), flat(attn), flat(lru), flat(x), mods, row0, row_step, p["norm2_g"],
                         p["w_out"], p["w_router_t"], p["b_router"], tm)
    return x1, route, k, v, fin, list(cast_a) + list(cast_b)


def kernel(x_prompt, x_sample, c, cache_k, cache_v, state_lru, c_ctx, w_mod, b_mod, norm1_g, norm2_g, w_in,
           conv_dw, conv_b, conv_ln_g, conv_ln_b, q_norm_g, k_norm_g, attn_sink, lru_conv_w, lru_conv_b,
           lru_wa, lru_ba, lru_wx, lru_bx, lru_lam, w_out, w_router, b_router, w_gate_e, w_up_e, w_down_e):
    bsz, seq, _ = x_prompt.shape
    dec_bsz, dec_seq, _ = x_sample.shape
    past = cache_k.shape[2]

    cvec = jnp.zeros((MOD_ROWS, D_MODEL), F32).at[0].set(c_ctx).at[1:1 + dec_bsz].set(c)
    mods_all = _modulation(cvec, w_mod, b_mod).reshape(DEPTH, MOD_ROWS, 6, D_MODEL)

    wr_hi = w_router.T.astype(BF16)
    wr_lo = (w_router.T - wr_hi.astype(F32)).astype(BF16)
    w_router_t = jnp.concatenate([wr_hi, wr_lo], axis=0)
    w_in_bf, w_out_bf = w_in[0].astype(BF16), w_out[0].astype(BF16)
    w_in_groups = w_in.reshape(DEPTH, CAST_GROUPS, D_MODEL // CAST_GROUPS, D_IN)
    w_out_groups = w_out.reshape(DEPTH, CAST_GROUPS, D_MODEL // CAST_GROUPS, D_MODEL)
    layers = []
    for l in range(DEPTH):
        wbd, gbias = _lru_gate_weights(lru_wa[l], lru_ba[l], lru_wx[l], lru_bx[l])
        layers.append({
            "norm1_g": norm1_g[l], "norm2_g": norm2_g[l],
            "conv_dw": conv_dw[l], "conv_b": conv_b[l], "conv_ln_g": conv_ln_g[l], "conv_ln_b": conv_ln_b[l],
            "q_norm_g": q_norm_g[l], "k_norm_g": k_norm_g[l], "attn_sink": attn_sink[l],
            "lru_conv_w": lru_conv_w[l], "lru_conv_b": lru_conv_b[l], "lru_wbd": wbd, "lru_gbias": gbias,
            "lru_lam": lru_lam[l],
            "w_router_t": w_router_t, "b_router": b_router,
        })

    y, z = x_prompt, x_sample
    ks, vs, hs = [], [], []
    h0_ctx = jnp.zeros((bsz, 2, D_LRU), F32)
    for l in range(DEPTH):
        p, mods = dict(layers[l], w_in=w_in_bf, w_out=w_out_bf), mods_all[l]
        y1, route_c, k_l, v_l, h_l, _ = _mixers(y, p, mods, 0, 0, h0_ctx, [], [])
        ks.append(k_l.reshape(bsz, seq, N_KV_HEADS, HEAD_DIM))
        vs.append(v_l.reshape(bsz, seq, N_KV_HEADS, HEAD_DIM))
        hs.append(h_l)
        casts_attn = [(w_gate_e, l), (w_down_e, l)]
        if l + 1 < DEPTH:
            casts_attn += [(w_in_groups, l + 1), (w_out_groups, l + 1)]
        z1, route_l, _, _, _, (wu_bf, wg_bf, wd_bf, *next_proj) = _mixers(
            z, p, mods, 1, 1, state_lru[:, l], [(w_up_e, l)], casts_attn,
            cache_k[:, l].reshape(dec_bsz, past, D_KV).astype(BF16),
            cache_v[:, l].reshape(dec_bsz, past, D_KV).astype(BF16))
        if next_proj:
            w_in_bf = next_proj[0].reshape(D_MODEL, D_IN)
            w_out_bf = next_proj[1].reshape(D_MODEL, D_MODEL)
        y, z = _moe(y1.reshape(bsz * seq, -1), z1.reshape(dec_bsz * dec_seq, -1), route_c, route_l,
                    mods, wg_bf, wu_bf, wd_bf, 1 + dec_bsz)
        y, z = y.reshape(bsz, seq, D_MODEL), z.reshape(dec_bsz, dec_seq, D_MODEL)
    new_cache_k = jnp.stack(ks, axis=1)
    new_cache_v = jnp.stack(vs, axis=1)
    new_state_lru = jnp.stack(hs, axis=1)
    return y, z, new_cache_k, new_cache_v, new_state_lru
```
